```python
import jax, jax.numpy as jnp
from jax import lax
import numpy as np

D_MODEL = 1024
BATCH = 16
SEQ = 256
DEPTH = 2
DEC_BATCH = 8
DEC_SEQ = 1024
PAST_LEN = 512

GRID_W = 64
N_HEADS = 16
HEAD_DIM = D_MODEL // N_HEADS
ATTN_SCALE = HEAD_DIM ** -0.5
WIN_ROWS = 8
WIN_COLS = 16
Q_BLOCK = 128
D_LRU = D_MODEL
LRU_BLOCKS = 8
LRU_BLOCK_W = D_LRU // LRU_BLOCKS
CONV_W = 4
LRU_C = 8.0
D_FF = -(-(8 * D_MODEL) // (3 * 256)) * 256
N_MOD = 6
N_ATTN = (DEPTH + 1) // 2
N_LRU = DEPTH // 2
RMS_EPS = 1e-6
NEG_INF = -1e30

kernel_name = 'hybrid_na_rglru_flow_step'


def _rms_norm(x, g):
    xf = x.astype(jnp.float32)
    y = xf * lax.rsqrt(jnp.mean(xf * xf, axis=-1, keepdims=True) + RMS_EPS)
    return (y * g.astype(jnp.float32)).astype(x.dtype)


def _adaln(cond, w_mod, b_mod):
    m = jax.nn.silu(cond) @ w_mod + b_mod
    return jnp.split(m, N_MOD, axis=-1)


def _modulate(x, g, shift, scale):
    return _rms_norm(x, g) * (1 + scale[:, None, :]) + shift[:, None, :]


def _swiglu(x, w_gu, w_down):
    gate, up = jnp.split(x @ w_gu, 2, axis=-1)
    return (jax.nn.silu(gate) * up) @ w_down


def _qkv(xm, w_qkv):
    B, T, _ = xm.shape
    q, k, v = jnp.split(xm @ w_qkv, 3, axis=-1)
    shp = (B, T, N_HEADS, HEAD_DIM)
    return q.reshape(shp), k.reshape(shp), v.reshape(shp)


def _context_self_attention(q, k, v):
    B, L, H, hd = q.shape
    nb = L // Q_BLOCK
    qb = jnp.moveaxis(q.reshape(B, nb, Q_BLOCK, H, hd), 1, 0)

    def block(qblk):
        s = jnp.einsum('bqhd,bkhd->bhqk', qblk * ATTN_SCALE, k).astype(jnp.float32)
        p = jax.nn.softmax(s, axis=-1).astype(v.dtype)
        return jnp.einsum('bhqk,bkhd->bqhd', p, v)

    o = lax.map(block, qb)
    return jnp.moveaxis(o, 0, 1).reshape(B, L, H * hd)


def _neighbourhood_attention(q, k, v, ck, cv, rpb):
    B, T, H, hd = q.shape
    rows = T // GRID_W
    kh = min(WIN_ROWS, rows)
    r = jnp.arange(rows)
    row_start = jnp.clip(r - kh // 2, 0, rows - kh)
    row_idx = row_start[:, None] + jnp.arange(kh)[None, :]
    col = jnp.arange(GRID_W)
    col_start = jnp.clip(col - WIN_COLS // 2, 0, GRID_W - WIN_COLS)
    in_win = (col[None, :] >= col_start[:, None]) & (col[None, :] < col_start[:, None] + WIN_COLS)
    dr = row_idx - r[:, None] + (WIN_ROWS - 1)
    dc = jnp.clip(col[None, :] - col[:, None], -(WIN_COLS - 1), WIN_COLS - 1) + (WIN_COLS - 1)
    bias = rpb.astype(jnp.float32)[:, dr[:, None, :, None], dc[None, :, None, :]]
    bias = jnp.where(in_win[None, None, :, None, :], bias, NEG_INF)
    qg = q.reshape(B, rows, GRID_W, H, hd) * ATTN_SCALE
    kg = k.reshape(B, rows, GRID_W, H, hd)[:, row_idx]
    vg = v.reshape(B, rows, GRID_W, H, hd)[:, row_idx]
    s_lat = jnp.einsum('brqhd,brjkhd->bhrqjk', qg, kg).astype(jnp.float32) + bias[None]
    s_ctx = jnp.einsum('brqhd,bchd->bhrqc', qg, ck).astype(jnp.float32)
    n_lat = kh * GRID_W
    s = jnp.concatenate([s_lat.reshape(B, H, rows, GRID_W, n_lat), s_ctx], axis=-1)
    p = jax.nn.softmax(s, axis=-1).astype(v.dtype)
    p_lat = p[..., :n_lat].reshape(B, H, rows, GRID_W, kh, GRID_W)
    p_ctx = p[..., n_lat:]
    o = (jnp.einsum('bhrqjk,brjkhd->brqhd', p_lat, vg)
         + jnp.einsum('bhrqc,bchd->brqhd', p_ctx, cv))
    return o.reshape(B, T, H * hd)


def _centred_depthwise_conv(x, w, b):
    T = x.shape[1]
    left = CONV_W // 2
    xp = jnp.pad(x, ((0, 0), (left, CONV_W - 1 - left), (0, 0)))
    y = b
    for j in range(CONV_W):
        y = y + xp[:, j:j + T] * w[j]
    return y


def _rglru_coeffs(xc, w_a, b_a, w_i, b_i, lam):
    B, T, D = xc.shape
    xb = xc.reshape(B, T, LRU_BLOCKS, LRU_BLOCK_W)
    r = jax.nn.sigmoid(jnp.einsum('btnk,nkj->btnj', xb, w_a).reshape(B, T, D) + b_a)
    i = jax.nn.sigmoid(jnp.einsum('btnk,nkj->btnj', xb, w_i).reshape(B, T, D) + b_i)
    log_a = -LRU_C * r * jax.nn.softplus(-lam.astype(jnp.float32))
    a = jnp.exp(log_a)
    b = jnp.sqrt(-jnp.expm1(2.0 * log_a)) * (i * xc)
    return a.astype(jnp.float32), b.astype(jnp.float32)


def _linear_scan(a, b, h0, reverse):
    def step(h, ab):
        a_t, b_t = ab
        h = a_t * h + b_t
        return h, h

    h_last, hs = lax.scan(step, h0, (jnp.swapaxes(a, 0, 1), jnp.swapaxes(b, 0, 1)), reverse=reverse)
    return jnp.swapaxes(hs, 0, 1), h_last


def _lru_mixer(xm, h0, w_in, conv_w, conv_b, w_a, b_a, w_i, b_i, lam, w_out):
    gate, xr = jnp.split(xm @ w_in, 2, axis=-1)
    xc = _centred_depthwise_conv(xr, conv_w, conv_b).astype(jnp.float32)
    h0 = h0.astype(jnp.float32)
    a_f, b_f = _rglru_coeffs(xc, w_a[0], b_a[0], w_i[0], b_i[0], lam[0])
    a_b, b_b = _rglru_coeffs(xc, w_a[1], b_a[1], w_i[1], b_i[1], lam[1])
    hs_f, hT_f = _linear_scan(a_f, b_f, h0[:, 0], reverse=False)
    hs_b, hT_b = _linear_scan(a_b, b_b, h0[:, 1], reverse=True)
    y = (hs_f + hs_b).astype(xm.dtype) * jax.nn.gelu(gate)
    return y @ w_out, jnp.stack([hT_f, hT_b], axis=1)


def setup_inputs(seed: int = 0) -> dict:
    key = jax.random.key(seed)
    ks = jax.random.split(key, 32)
    f32 = jnp.float32

    def nrm(k, shape, scale):
        return jax.random.normal(k, shape, f32) * scale

    D = D_MODEL
    u = jax.random.uniform(ks[20], (N_LRU, 2, D_LRU), f32, 0.9, 0.999)
    a0 = u ** (1.0 / LRU_C)
    return {
        'x_prompt': nrm(ks[0], (BATCH, SEQ, D), 1.0),
        'x_sample': nrm(ks[1], (DEC_BATCH, DEC_SEQ, D), 1.0),
        'c': nrm(ks[2], (DEC_BATCH, D), 1.0),
        'cache_k': nrm(ks[3], (DEC_BATCH, N_ATTN, PAST_LEN, N_HEADS, HEAD_DIM), 1.0),
        'cache_v': nrm(ks[4], (DEC_BATCH, N_ATTN, PAST_LEN, N_HEADS, HEAD_DIM), 1.0),
        'state_h': nrm(ks[5], (DEC_BATCH, N_LRU, 2, D_LRU), 0.5),
        'c_ctx': nrm(ks[6], (D,), 1.0),
        'norm_g': 1.0 + nrm(ks[7], (DEPTH, 2, D), 0.02),
        'w_mod': nrm(ks[8], (DEPTH, D, N_MOD * D), 0.5 * D ** -0.5),
        'b_mod': nrm(ks[9], (DEPTH, N_MOD * D), 0.02),
        'attn_w_qkv': nrm(ks[10], (N_ATTN, D, 3 * D), D ** -0.5),
        'attn_w_o': nrm(ks[11], (N_ATTN, D, D), D ** -0.5),
        'attn_rpb': nrm(ks[12], (N_ATTN, N_HEADS, 2 * WIN_ROWS - 1, 2 * WIN_COLS - 1), 0.1),
        'lru_w_in': nrm(ks[13], (N_LRU, D, 2 * D_LRU), D ** -0.5),
        'lru_conv_w': nrm(ks[14], (N_LRU, CONV_W, D_LRU), CONV_W ** -0.5),
        'lru_conv_b': nrm(ks[15], (N_LRU, D_LRU), 0.02),
        'lru_w_a': nrm(ks[16], (N_LRU, 2, LRU_BLOCKS, LRU_BLOCK_W, LRU_BLOCK_W), LRU_BLOCK_W ** -0.5),
        'lru_b_a': nrm(ks[17], (N_LRU, 2, D_LRU), 0.02),
        'lru_w_i': nrm(ks[18], (N_LRU, 2, LRU_BLOCKS, LRU_BLOCK_W, LRU_BLOCK_W), LRU_BLOCK_W ** -0.5),
        'lru_b_i': nrm(ks[19], (N_LRU, 2, D_LRU), 0.02),
        'lru_lam': jnp.log(a0) - jnp.log1p(-a0),
        'lru_w_out': nrm(ks[21], (N_LRU, D_LRU, D), D_LRU ** -0.5),
        'ffn_w_gu': nrm(ks[22], (DEPTH, D, 2 * D_FF), D ** -0.5),
        'ffn_w_down': nrm(ks[23], (DEPTH, D_FF, D), D_FF ** -0.5),
        'final_g': 1.0 + nrm(ks[24], (D,), 0.02),
    }


def reference(x_prompt, x_sample, c, cache_k, cache_v, state_h, c_ctx, norm_g, w_mod, b_mod,
              attn_w_qkv, attn_w_o, attn_rpb, lru_w_in, lru_conv_w, lru_conv_b, lru_w_a, lru_b_a,
              lru_w_i, lru_b_i, lru_lam, lru_w_out, ffn_w_gu, ffn_w_down, final_g):
    ctx = x_prompt
    lat = x_sample
    new_k, new_v, new_h = [], [], []
    for i in range(DEPTH):
        j = i // 2
        sh1_c, sc1_c, g1_c, sh2_c, sc2_c, g2_c = _adaln(c_ctx[None, :], w_mod[i], b_mod[i])
        sh1_l, sc1_l, g1_l, sh2_l, sc2_l, g2_l = _adaln(c, w_mod[i], b_mod[i])
        xm_c = _modulate(ctx, norm_g[i, 0], sh1_c, sc1_c)
        xm_l = _modulate(lat, norm_g[i, 0], sh1_l, sc1_l)
        if i % 2 == 0:
            q_c, k_c, v_c = _qkv(xm_c, attn_w_qkv[j])
            o_c = _context_self_attention(q_c, k_c, v_c) @ attn_w_o[j]
            q_l, k_l, v_l = _qkv(xm_l, attn_w_qkv[j])
            o_l = _neighbourhood_attention(q_l, k_l, v_l, cache_k[:, j], cache_v[:, j], attn_rpb[j]) @ attn_w_o[j]
            new_k.append(k_c)
            new_v.append(v_c)
        else:
            lru_p = (lru_w_in[j], lru_conv_w[j], lru_conv_b[j], lru_w_a[j], lru_b_a[j],
                     lru_w_i[j], lru_b_i[j], lru_lam[j], lru_w_out[j])
            h0_c = jnp.zeros((ctx.shape[0], 2, D_LRU), jnp.float32)
            o_c, h_c = _lru_mixer(xm_c, h0_c, *lru_p)
            o_l, _ = _lru_mixer(xm_l, state_h[:, j], *lru_p)
            new_h.append(h_c)
        ctx = ctx + g1_c[:, None, :] * o_c
        lat = lat + g1_l[:, None, :] * o_l
        f_c = _swiglu(_modulate(ctx, norm_g[i, 1], sh2_c, sc2_c), ffn_w_gu[i], ffn_w_down[i])
        f_l = _swiglu(_modulate(lat, norm_g[i, 1], sh2_l, sc2_l), ffn_w_gu[i], ffn_w_down[i])
        ctx = ctx + g2_c[:, None, :] * f_c
        lat = lat + g2_l[:, None, :] * f_l
    y_prompt = _rms_norm(ctx, final_g)
    y_sample = _rms_norm(lat, final_g)
    return (y_prompt, y_sample, jnp.stack(new_k, axis=1), jnp.stack(new_v, axis=1), jnp.stack(new_h, axis=1))
```

```python
import functools

import jax
import jax.numpy as jnp
from jax import lax
from jax.experimental import pallas as pl
from jax.experimental.pallas import tpu as pltpu

F32 = jnp.float32
BF16 = jnp.bfloat16

D = 1024
B_CTX, T_CTX = 16, 256
B_LAT, T_LAT = 8, 1024
PAST = 512
GRID_W = 64
GRID_ROWS = T_LAT // GRID_W
N_HEADS = 16
HEAD_DIM = D // N_HEADS
ATTN_SCALE = HEAD_DIM ** -0.5
WIN_ROWS, WIN_COLS = 8, 16
N_DR = 2 * WIN_ROWS - 1
N_DC = 2 * WIN_COLS - 1
LRU_BLOCKS = 8
LRU_BW = D // LRU_BLOCKS
CONV_W = 4
LRU_C = 8.0
D_FF = 2816
N_MOD = 6
RMS_EPS = 1e-6
NEG_INF = -1e30

N_LAT_ROWS = B_LAT * T_LAT
N_CTX_ROWS = B_CTX * T_CTX
N_ROWS = N_LAT_ROWS + N_CTX_ROWS
N_COND = 16

TM = 512
N_LAT_TILES = N_LAT_ROWS // TM
N_TILES = N_ROWS // TM
TILES_PER_LAT_SAMPLE = T_LAT // TM
MOD_TN = 1536
FF_CHUNK = 256
Q_BLK_ROWS = 4
K_WIN_ROWS = 12
SCAN_CT = 32
VMEM_LIMIT = 56 * 1024 * 1024


def _sigmoid(x):
    return 0.5 * jnp.tanh(0.5 * x) + 0.5


def _modulate(x, g, shift, scale):
    ms = jnp.mean(x * x, axis=-1, keepdims=True)
    return (x * lax.rsqrt(ms + RMS_EPS)) * g * (1.0 + scale) + shift


def _cond_row_of_tile(i):
    return jnp.where(i < N_LAT_TILES, i // TILES_PER_LAT_SAMPLE, B_LAT)


def _const_spec(shape):
    nd = len(shape)
    return pl.BlockSpec(shape, lambda *_: (0,) * nd, pipeline_mode=pl.Buffered(1))


def _params(sem):
    return pltpu.CompilerParams(dimension_semantics=sem, vmem_limit_bytes=VMEM_LIMIT)


def _mod_kernel(cond_ref, w_ref, b_ref, o_ref):
    c = cond_ref[...]
    s = (c * _sigmoid(c)).astype(BF16)
    o_ref[...] = jnp.dot(s, w_ref[...].astype(BF16), preferred_element_type=F32) + b_ref[...]


def _mod_call(cond, w_mod, b_mod):
    depth = w_mod.shape[0]
    n = N_MOD * D
    return pl.pallas_call(
        _mod_kernel,
        grid=(depth, n // MOD_TN),
        in_specs=[
            pl.BlockSpec((N_COND, D), lambda l, j: (0, 0)),
            pl.BlockSpec((None, D, MOD_TN), lambda l, j: (l, 0, j)),
            pl.BlockSpec((None, 1, MOD_TN), lambda l, j: (l, 0, j)),
        ],
        out_specs=pl.BlockSpec((None, N_COND, MOD_TN), lambda l, j: (l, 0, j)),
        out_shape=jax.ShapeDtypeStruct((depth, N_COND, n), F32),
        compiler_params=_params(("arbitrary", "arbitrary")),
        name="adaln_mod",
    )(cond, w_mod, b_mod.reshape(depth, 1, n))


def _bias_table_kernel(r_ref, o_ref):
    n = GRID_W * GRID_W
    j = lax.broadcasted_iota(jnp.int32, (32, n), 0)
    idx = lax.broadcasted_iota(jnp.int32, (32, n), 1)
    cq = jnp.right_shift(idx, GRID_W.bit_length() - 1)
    ck = jnp.bitwise_and(idx, GRID_W - 1)
    onehot = jnp.where(ck - cq + (WIN_COLS - 1) == j, 1.0, 0.0).astype(BF16)
    f = r_ref[...]
    hi = f.astype(BF16)
    r1 = f - hi.astype(F32)
    mid = r1.astype(BF16)
    lo = (r1 - mid.astype(F32)).astype(BF16)
    t = (jnp.dot(hi, onehot, preferred_element_type=F32)
         + jnp.dot(mid, onehot, preferred_element_type=F32)
         + jnp.dot(lo, onehot, preferred_element_type=F32))
    cq1 = cq[:1]
    ck1 = ck[:1]
    cs = jnp.clip(cq1 - WIN_COLS // 2, 0, GRID_W - WIN_COLS)
    in_win = jnp.logical_and(ck1 >= cs, ck1 < cs + WIN_COLS)
    o_ref[...] = jnp.where(in_win, t, NEG_INF)


def _bias_pair_table(rpb):
    r = jnp.pad(rpb.reshape(N_HEADS * N_DR, N_DC), ((0, 0), (0, 32 - N_DC)))
    t = pl.pallas_call(
        _bias_table_kernel,
        out_shape=jax.ShapeDtypeStruct((N_HEADS * N_DR, GRID_W * GRID_W), F32),
        compiler_params=pltpu.CompilerParams(vmem_limit_bytes=VMEM_LIMIT),
        name="rpb_table",
    )(r)
    t = t.reshape(N_HEADS, N_DR, GRID_W, GRID_W)
    neg = jnp.full((N_HEADS, 1, GRID_W, GRID_W), NEG_INF, F32)
    full = jnp.concatenate([t[:, :N_DR - 1], t[:, 1:]], axis=-1)
    first = jnp.concatenate([neg, t[:, 3:4]], axis=-1)
    last = jnp.concatenate([t[:, 10:11], neg], axis=-1)
    return jnp.concatenate([full, first, last], axis=1)


def _pair_type(r, kr_left):
    rs = min(max(r - WIN_ROWS // 2, 0), GRID_ROWS - WIN_ROWS)
    left_in = rs <= kr_left < rs + WIN_ROWS
    right_in = rs <= kr_left + 1 < rs + WIN_ROWS
    if left_in and right_in:
        dr = kr_left - r
        assert -(WIN_ROWS - 1) <= dr <= WIN_ROWS - 2
        return dr + WIN_ROWS - 1
    if right_in:
        assert kr_left + 1 - r == -(WIN_ROWS // 2)
        return 14
    if left_in:
        assert kr_left - r == WIN_ROWS // 2 - 1
        return 15
    return None


def _qkv_kernel(x_ref, m_ref, g_ref, w_ref, q_ref, k_ref, v_ref, kf_ref, vf_ref):
    xm = _modulate(x_ref[...], g_ref[...], m_ref[0], m_ref[1])
    qkv = jnp.dot(xm.astype(BF16), w_ref[...], preferred_element_type=F32)
    q_ref[...] = (qkv[:, :D] * ATTN_SCALE).astype(BF16)
    k = qkv[:, D:2 * D]
    v = qkv[:, 2 * D:]
    k_ref[...] = k.astype(BF16)
    v_ref[...] = v.astype(BF16)
    kf_ref[...] = k
    vf_ref[...] = v


def _qkv_call(x, mod, g, w_qkv):
    row = pl.BlockSpec((TM, D), lambda i: (i, 0))
    ctx_row = pl.BlockSpec((TM, D), lambda i: (jnp.maximum(i - N_LAT_TILES, 0), 0))
    return pl.pallas_call(
        _qkv_kernel,
        grid=(N_TILES,),
        in_specs=[
            row,
            pl.BlockSpec((None, N_MOD, 1, D), lambda i: (_cond_row_of_tile(i), 0, 0, 0)),
            _const_spec((1, D)),
            _const_spec((D, 3 * D)),
        ],
        out_specs=[row, row, row, ctx_row, ctx_row],
        out_shape=[jax.ShapeDtypeStruct((N_ROWS, D), BF16)] * 3
        + [jax.ShapeDtypeStruct((N_CTX_ROWS, D), F32)] * 2,
        compiler_params=_params(("arbitrary",)),
        name="qkv_proj",
    )(x, mod, g, w_qkv)


def _softmax_pv(scores, values):
    m = functools.reduce(jnp.maximum, [s.max(axis=-1, keepdims=True) for s in scores])
    ps = [jnp.exp(s - m) for s in scores]
    l = functools.reduce(jnp.add, [p.sum(axis=-1, keepdims=True) for p in ps])
    o = functools.reduce(jnp.add, [jnp.dot(p.astype(BF16), v, preferred_element_type=F32)
                                   for p, v in zip(ps, values)])
    return o / l


def _qk(q, k):
    return lax.dot_general(q, k, (((1,), (1,)), ((), ())), preferred_element_type=F32)


def _ctx_attn_kernel(q_ref, k_ref, v_ref, o_ref):
    for h in range(N_HEADS):
        ls = slice(h * HEAD_DIM, (h + 1) * HEAD_DIM)
        o = _softmax_pv([_qk(q_ref[:, ls], k_ref[:, ls])], [v_ref[:, ls]])
        o_ref[:, ls] = o.astype(BF16)


def _ctx_attn_call(q, k, v):
    blk = pl.BlockSpec((T_CTX, D), lambda b: (N_LAT_ROWS // T_CTX + b, 0))
    return pl.pallas_call(
        _ctx_attn_kernel,
        grid=(B_CTX,),
        in_specs=[blk, blk, blk],
        out_specs=pl.BlockSpec((T_CTX, D), lambda b: (b, 0)),
        out_shape=jax.ShapeDtypeStruct((N_CTX_ROWS, D), BF16),
        compiler_params=_params(("arbitrary",)),
        name="ctx_attn",
    )(q, k, v)


def _nbr_attn_kernel(q_ref, k_ref, v_ref, ck_ref, cv_ref, pt_ref, o_ref):
    ck = ck_ref[...].astype(BF16)
    cv = cv_ref[...].astype(BF16)
    neg_blk = jnp.full((GRID_W, 2 * GRID_W), NEG_INF, F32)
    q_rows = Q_BLK_ROWS * GRID_W
    for hh in range(2):
        ls = slice(hh * HEAD_DIM, (hh + 1) * HEAD_DIM)
        for i in range(GRID_ROWS // Q_BLK_ROWS):
            w0 = min(max(Q_BLK_ROWS * i - WIN_ROWS // 2, 0), GRID_ROWS - K_WIN_ROWS)
            w0 -= w0 % 2
            ks = slice(w0 * GRID_W, (w0 + K_WIN_ROWS) * GRID_W)
            qb = q_ref[i * q_rows:(i + 1) * q_rows, ls]
            bias_rows = []
            for qr in range(Q_BLK_ROWS):
                r = Q_BLK_ROWS * i + qr
                rs = min(max(r - WIN_ROWS // 2, 0), GRID_ROWS - WIN_ROWS)
                assert w0 <= rs and rs + WIN_ROWS <= w0 + K_WIN_ROWS
                blks = []
                for m in range(K_WIN_ROWS // 2):
                    typ = _pair_type(r, w0 + 2 * m)
                    blks.append(neg_blk if typ is None else pt_ref[hh, typ])
                bias_rows.append(jnp.concatenate(blks, axis=1))
            s_lat = _qk(qb, k_ref[ks, ls]) + jnp.concatenate(bias_rows, axis=0)
            s_ctx = _qk(qb, ck[:, ls])
            o = _softmax_pv([s_lat, s_ctx], [v_ref[ks, ls], cv[:, ls]])
            o_ref[i * q_rows:(i + 1) * q_rows, ls] = o.astype(BF16)


def _nbr_attn_call(q, k, v, cache_k, cache_v, pair_table):
    hw = 2 * HEAD_DIM
    blk = pl.BlockSpec((T_LAT, hw), lambda b, hp: (b, hp))
    cache = pl.BlockSpec((None, PAST, hw), lambda b, hp: (b, 0, hp))
    return pl.pallas_call(
        _nbr_attn_kernel,
        grid=(B_LAT, N_HEADS // 2),
        in_specs=[blk, blk, blk, cache, cache,
                  pl.BlockSpec((2, 16, GRID_W, 2 * GRID_W), lambda b, hp: (hp, 0, 0, 0))],
        out_specs=blk,
        out_shape=jax.ShapeDtypeStruct((N_LAT_ROWS, D), BF16),
        compiler_params=_params(("arbitrary", "arbitrary")),
        name="nbr_attn",
    )(q, k, v, cache_k, cache_v, pair_table)


def _post_kernel(final_norm, x_ref, ol_ref, oc_ref, m_ref, g_ref, wo_ref, wgu_ref, wd_ref, fg_ref,
                 out_ref, o_scr, acc_ref):
    i = pl.program_id(0)

    @pl.when(i < N_LAT_TILES)
    def _():
        o_scr[...] = ol_ref[...]

    @pl.when(i >= N_LAT_TILES)
    def _():
        o_scr[...] = oc_ref[...]

    mix = jnp.dot(o_scr[...], wo_ref[...], preferred_element_type=F32)
    x1 = x_ref[...] + m_ref[2] * mix
    h = _modulate(x1, g_ref[...], m_ref[3], m_ref[4]).astype(BF16)
    for c in range(D_FF // FF_CHUNK):
        cs = slice(c * FF_CHUNK, (c + 1) * FF_CHUNK)
        us = slice(D_FF + c * FF_CHUNK, D_FF + (c + 1) * FF_CHUNK)
        gate = jnp.dot(h, wgu_ref[:, cs], preferred_element_type=F32)
        up = jnp.dot(h, wgu_ref[:, us], preferred_element_type=F32)
        a = (gate * _sigmoid(gate) * up).astype(BF16)
        part = jnp.dot(a, wd_ref[cs, :], preferred_element_type=F32)
        if c == 0:
            acc_ref[...] = part
        else:
            acc_ref[...] += part
    x2 = x1 + m_ref[5] * acc_ref[...]
    if final_norm:
        ms = jnp.mean(x2 * x2, axis=-1, keepdims=True)
        x2 = x2 * lax.rsqrt(ms + RMS_EPS) * fg_ref[...]
    out_ref[...] = x2


def _post_call(x, o_lat, o_ctx, mod, g, w_o, w_gu, w_down, final_g, final_norm):
    row = pl.BlockSpec((TM, D), lambda i: (i, 0))
    return pl.pallas_call(
        functools.partial(_post_kernel, final_norm),
        grid=(N_TILES,),
        in_specs=[
            row,
            pl.BlockSpec((TM, D), lambda i: (jnp.minimum(i, N_LAT_TILES - 1), 0)),
            pl.BlockSpec((TM, D), lambda i: (jnp.maximum(i - N_LAT_TILES, 0), 0)),
            pl.BlockSpec((None, N_MOD, 1, D), lambda i: (_cond_row_of_tile(i), 0, 0, 0)),
            _const_spec((1, D)),
            _const_spec((D, D)),
            _const_spec((D, 2 * D_FF)),
            _const_spec((D_FF, D)),
            _const_spec((1, D)),
        ],
        out_specs=row,
        out_shape=jax.ShapeDtypeStruct((N_ROWS, D), F32),
        scratch_shapes=[pltpu.VMEM((TM, D), BF16), pltpu.VMEM((TM, D), F32)],
        compiler_params=_params(("arbitrary",)),
        name="mix_out_ffn",
    )(x, o_lat, o_ctx, mod, g, w_o, w_gu, w_down, final_g)


def _lru_in_kernel(x_ref, m_ref, g_ref, w_ref, gate_ref, xr_ref):
    xm = _modulate(x_ref[...], g_ref[...], m_ref[0], m_ref[1])
    y = jnp.dot(xm.astype(BF16), w_ref[...], preferred_element_type=F32)
    gate_ref[...] = jax.nn.gelu(y[:, :D]).astype(BF16)
    xr_ref[...] = y[:, D:]


def _lru_in_call(x, mod, g, w_in):
    row = pl.BlockSpec((TM, D), lambda i: (i, 0))
    return pl.pallas_call(
        _lru_in_kernel,
        grid=(N_TILES,),
        in_specs=[
            row,
            pl.BlockSpec((None, N_MOD, 1, D), lambda i: (_cond_row_of_tile(i), 0, 0, 0)),
            _const_spec((1, D)),
            _const_spec((D, 2 * D)),
        ],
        out_specs=[row, row],
        out_shape=[jax.ShapeDtypeStruct((N_ROWS, D), BF16), jax.ShapeDtypeStruct((N_ROWS, D), F32)],
        compiler_params=_params(("arbitrary",)),
        name="lru_in_proj",
    )(x, mod, g, w_in)


def _lru_scan_kernel(nb, nt, xr_ref, gg_ref, cw_ref, cb_ref, wa_ref, wi_ref, ba_ref, bi_ref, lam_ref,
                     h0_ref, y_ref, ht_ref, xt_ref, hf_ref, hb_ref, af_ref, bf_ref, ab_ref, bb_ref):
    left = CONV_W // 2
    pad_rows = (CONV_W - 1) * nb
    xt_ref[pl.ds(0, left * nb), :] = jnp.zeros((left * nb, LRU_BW), F32)
    xt_ref[pl.ds((left + nt) * nb, pad_rows - left * nb), :] = jnp.zeros((pad_rows - left * nb, LRU_BW), F32)
    for s in range(nb):
        xt_ref[pl.ds(left * nb + s, nt, stride=nb), :] = xr_ref[pl.ds(s * nt, nt), :]

    cw = cw_ref[...]
    cb = cb_ref[...]
    w_f = jnp.concatenate([wa_ref[0], wi_ref[0]], axis=1).astype(BF16)
    w_b = jnp.concatenate([wa_ref[1], wi_ref[1]], axis=1).astype(BF16)
    neg_lam = -lam_ref[...]
    sp = jnp.maximum(neg_lam, 0.0) + jnp.log1p(jnp.exp(-jnp.abs(neg_lam)))
    crows = SCAN_CT * nb

    def coeffs(t0, w, d):
        xc = cb
        for j in range(CONV_W):
            xc = xc + xt_ref[pl.ds(pl.multiple_of((t0 + j) * nb, 8), crows), :] * cw[j:j + 1]
        pre = jnp.dot(xc.astype(BF16), w, preferred_element_type=F32)
        r = _sigmoid(pre[:, :LRU_BW] + ba_ref[d:d + 1])
        gi = _sigmoid(pre[:, LRU_BW:] + bi_ref[d:d + 1])
        a = jnp.exp((-LRU_C) * r * sp[d:d + 1])
        b = jnp.sqrt(1.0 - a * a) * (gi * xc)
        return a, b

    n_chunks = nt // SCAN_CT

    def chunk(c, carry):
        hf, hb = carry
        tf = c * SCAN_CT
        tb = (n_chunks - 1 - c) * SCAN_CT
        a, b = coeffs(tf, w_f, 0)
        af_ref[...] = a
        bf_ref[...] = b
        a, b = coeffs(tb, w_b, 1)
        ab_ref[...] = a
        bb_ref[...] = b
        for s in range(SCAN_CT):
            fs = slice(s * nb, (s + 1) * nb)
            hf = af_ref[fs, :] * hf + bf_ref[fs, :]
            hf_ref[pl.ds(pl.multiple_of((tf + s) * nb, 8), nb), :] = hf
            u = SCAN_CT - 1 - s
            us = slice(u * nb, (u + 1) * nb)
            hb = ab_ref[us, :] * hb + bb_ref[us, :]
            hb_ref[pl.ds(pl.multiple_of((tb + u) * nb, 8), nb), :] = hb
        return hf, hb

    hf, hb = lax.fori_loop(0, n_chunks, chunk, (h0_ref[0], h0_ref[1]))
    ht_ref[0] = hf
    ht_ref[1] = hb

    for s in range(nb):
        hsum = hf_ref[pl.ds(s, nt, stride=nb), :] + hb_ref[pl.ds(s, nt, stride=nb), :]
        y_ref[pl.ds(s * nt, nt), :] = (hsum * gg_ref[pl.ds(s * nt, nt), :].astype(F32)).astype(BF16)


def _lru_scan_call(nb, nt, row_block0, gg, xr, h0, conv_w, conv_b, w_a, w_i, b_a, b_i, lam):
    rows = nb * nt
    col = pl.BlockSpec((rows, LRU_BW), lambda n: (row_block0, n))
    vec2 = pl.BlockSpec((2, LRU_BW), lambda n: (0, n))
    wblk = pl.BlockSpec((2, None, LRU_BW, LRU_BW), lambda n: (0, n, 0, 0))
    state = pl.BlockSpec((2, nb, LRU_BW), lambda n: (0, 0, n))
    tm_rows = (nt + CONV_W - 1) * nb
    crows = SCAN_CT * nb
    return pl.pallas_call(
        functools.partial(_lru_scan_kernel, nb, nt),
        grid=(LRU_BLOCKS,),
        in_specs=[
            col, col,
            pl.BlockSpec((CONV_W, LRU_BW), lambda n: (0, n)),
            pl.BlockSpec((1, LRU_BW), lambda n: (0, n)),
            wblk, wblk, vec2, vec2, vec2, state,
        ],
        out_specs=[pl.BlockSpec((rows, LRU_BW), lambda n: (0, n)), state],
        out_shape=[jax.ShapeDtypeStruct((rows, D), BF16), jax.ShapeDtypeStruct((2, nb, D), F32)],
        scratch_shapes=[pltpu.VMEM((tm_rows, LRU_BW), F32),
                        pltpu.VMEM((rows, LRU_BW), F32),
                        pltpu.VMEM((rows, LRU_BW), F32)]
        + [pltpu.VMEM((crows, LRU_BW), F32)] * 4,
        compiler_params=_params(("arbitrary",)),
        name=f"lru_scan_{nb}x{nt}",
    )(xr, gg, conv_w, conv_b, w_a, w_i, b_a, b_i, lam, h0)


def kernel(x_prompt, x_sample, c, cache_k, cache_v, state_h, c_ctx, norm_g, w_mod, b_mod, attn_w_qkv, attn_w_o,
           attn_rpb, lru_w_in, lru_conv_w, lru_conv_b, lru_w_a, lru_b_a, lru_w_i, lru_b_i, lru_lam, lru_w_out,
           ffn_w_gu, ffn_w_down, final_g):
    x = jnp.concatenate([x_sample.reshape(N_LAT_ROWS, D), x_prompt.reshape(N_CTX_ROWS, D)], axis=0)
    cond = jnp.concatenate([c, c_ctx[None, :], jnp.zeros((N_COND - B_LAT - 1, D), F32)], axis=0)
    mod = _mod_call(cond, w_mod, b_mod).reshape(2, N_COND, N_MOD, 1, D)
    final_g2 = final_g.reshape(1, D)

    q, k, v, k_ctx, v_ctx = _qkv_call(x, mod[0], norm_g[0, 0].reshape(1, D), attn_w_qkv[0].astype(BF16))
    o_ctx = _ctx_attn_call(q, k, v)
    pair_table = _bias_pair_table(attn_rpb[0])
    o_lat = _nbr_attn_call(q, k, v, cache_k[:, 0].reshape(B_LAT, PAST, D), cache_v[:, 0].reshape(B_LAT, PAST, D),
                           pair_table)
    x = _post_call(x, o_lat, o_ctx, mod[0], norm_g[0, 1].reshape(1, D), attn_w_o[0].astype(BF16),
                   ffn_w_gu[0].astype(BF16), ffn_w_down[0].astype(BF16), final_g2, False)

    gg, xr = _lru_in_call(x, mod[1], norm_g[1, 0].reshape(1, D), lru_w_in[0].astype(BF16))
    lru_p = (lru_conv_w[0], lru_conv_b[0].reshape(1, D), lru_w_a[0], lru_w_i[0], lru_b_a[0], lru_b_i[0], lru_lam[0])
    y_lat, _ = _lru_scan_call(B_LAT, T_LAT, 0, gg, xr, jnp.swapaxes(state_h[:, 0], 0, 1), *lru_p)
    y_ctx, h_ctx = _lru_scan_call(B_CTX, T_CTX, N_LAT_ROWS // N_CTX_ROWS, gg, xr,
                                  jnp.zeros((2, B_CTX, D), F32), *lru_p)
    y = _post_call(x, y_lat, y_ctx, mod[1], norm_g[1, 1].reshape(1, D), lru_w_out[0].astype(BF16),
                   ffn_w_gu[1].astype(BF16), ffn_w_down[1].astype(BF16), final_g2, True)

    y_sample = y[:N_LAT_ROWS].reshape(B_LAT, T_LAT, D)
    y_prompt = y[N_LAT_ROWS:].reshape(B_CTX, T_CTX, D)
    new_k = k_ctx.reshape(B_CTX, 1, T_CTX, N_HEADS, HEAD_DIM)
    new_v = v_ctx.reshape(B_CTX, 1, T_CTX, N_HEADS, HEAD_DIM)
    new_h = jnp.swapaxes(h_ctx, 0, 1)[:, None]
    return y_prompt, y_sample, new_k, new_v, new_h
```

```python
import functools
import math

import jax
import jax.numpy as jnp
from jax import lax
from jax.experimental import pallas as pl
from jax.experimental.pallas import tpu as pltpu

F32 = jnp.float32
BF16 = jnp.bfloat16

D = 1024
B_CTX, T_CTX = 16, 256
B_LAT, T_LAT = 8, 1024
PAST = 512
GRID_W = 64
GRID_ROWS = T_LAT // GRID_W
N_HEADS = 16
HEAD_DIM = D // N_HEADS
ATTN_SCALE = HEAD_DIM ** -0.5
LOG2E = math.log2(math.e)
WIN_ROWS, WIN_COLS = 8, 16
N_DR = 2 * WIN_ROWS - 1
N_DC = 2 * WIN_COLS - 1
LRU_BLOCKS = 8
LRU_BW = D // LRU_BLOCKS
CONV_W = 4
LRU_C = 8.0
D_FF = 2816
N_MOD = 6
RMS_EPS = 1e-6
NEG_INF = -1e30

N_LAT_ROWS = B_LAT * T_LAT
N_CTX_ROWS = B_CTX * T_CTX
N_ROWS = N_LAT_ROWS + N_CTX_ROWS
N_COND = 16

TM = 512
N_LAT_TILES = N_LAT_ROWS // TM
N_CTX_TILES = N_CTX_ROWS // TM
N_TILES = N_LAT_TILES + N_CTX_TILES
TILES_PER_LAT_SAMPLE = T_LAT // TM
MOD_TN = 1536
FF_CHUNK = 256
Q_BLK_ROWS = 4
N_PAIR_TYPES = 16
CAST_CHUNKS = 16
SCAN_CT = 32
VMEM_LIMIT = 56 * 1024 * 1024


def _sigmoid(x):
    return 0.5 * jnp.tanh(0.5 * x) + 0.5


def _modulate(x, g, shift, scale):
    ms = jnp.mean(x * x, axis=-1, keepdims=True)
    return (x * lax.rsqrt(ms + RMS_EPS)) * g * (1.0 + scale) + shift


def _cond_row_of_tile(i):
    return jnp.where(i < N_LAT_TILES, i // TILES_PER_LAT_SAMPLE, B_LAT)


def _const_spec(shape):
    nd = len(shape)
    return pl.BlockSpec(shape, lambda *_: (0,) * nd, pipeline_mode=pl.Buffered(1))


def _params(sem):
    return pltpu.CompilerParams(dimension_semantics=sem, vmem_limit_bytes=VMEM_LIMIT)


_ROW_SPEC = pl.BlockSpec((TM, D), lambda i: (i, 0))
_LAT_SPEC = pl.BlockSpec((TM, D), lambda i: (jnp.minimum(i, N_LAT_TILES - 1), 0))
_CTX_SPEC = pl.BlockSpec((TM, D), lambda i: (jnp.maximum(i - N_LAT_TILES, 0), 0))
_MOD_SPEC = pl.BlockSpec((None, N_MOD, 1, D), lambda i: (_cond_row_of_tile(i), 0, 0, 0))


def _load_pair_tile(lat_ref, ctx_ref, scr_ref):
    i = pl.program_id(0)

    @pl.when(i < N_LAT_TILES)
    def _():
        scr_ref[...] = lat_ref[...]

    @pl.when(i >= N_LAT_TILES)
    def _():
        scr_ref[...] = ctx_ref[...]

    return scr_ref[...]


def _store_pair_tile(lat_ref, ctx_ref, val):
    i = pl.program_id(0)

    @pl.when(i < N_LAT_TILES)
    def _():
        lat_ref[...] = val

    @pl.when(i >= N_LAT_TILES)
    def _():
        ctx_ref[...] = val


def _cast_specs(w):
    rows, cols = w.shape
    spec = pl.BlockSpec((rows // CAST_CHUNKS, cols), lambda i: (jnp.minimum(i, CAST_CHUNKS - 1), 0))
    return spec, jax.ShapeDtypeStruct((rows, cols), BF16)


def _cast_chunks(in_refs, out_refs):
    @pl.when(pl.program_id(0) < CAST_CHUNKS)
    def _():
        for src, dst in zip(in_refs, out_refs):
            dst[...] = src[...].astype(BF16)


def _mod_kernel(cond_ref, w_ref, b_ref, o_ref):
    c = cond_ref[...]
    s = (c * _sigmoid(c)).astype(BF16)
    o_ref[...] = jnp.dot(s, w_ref[...].astype(BF16), preferred_element_type=F32) + b_ref[...]


def _mod_call(cond, w_mod, b_mod):
    depth = w_mod.shape[0]
    n = N_MOD * D
    return pl.pallas_call(
        _mod_kernel,
        grid=(depth, n // MOD_TN),
        in_specs=[
            pl.BlockSpec((N_COND, D), lambda l, j: (0, 0)),
            pl.BlockSpec((None, D, MOD_TN), lambda l, j: (l, 0, j)),
            pl.BlockSpec((None, 1, MOD_TN), lambda l, j: (l, 0, j)),
        ],
        out_specs=pl.BlockSpec((None, N_COND, MOD_TN), lambda l, j: (l, 0, j)),
        out_shape=jax.ShapeDtypeStruct((depth, N_COND, n), F32),
        compiler_params=_params(("arbitrary", "arbitrary")),
        name="adaln_mod",
    )(cond, w_mod, b_mod.reshape(depth, 1, n))


def _bias_table_kernel(r_ref, o_ref):
    n = GRID_W * GRID_W
    j = lax.broadcasted_iota(jnp.int32, (32, n), 0)
    idx = lax.broadcasted_iota(jnp.int32, (32, n), 1)
    ck = jnp.right_shift(idx, GRID_W.bit_length() - 1)
    cq = jnp.bitwise_and(idx, GRID_W - 1)
    onehot = jnp.where(ck - cq + (WIN_COLS - 1) == j, 1.0, 0.0).astype(BF16)
    f = r_ref[...]
    hi = f.astype(BF16)
    r1 = f - hi.astype(F32)
    mid = r1.astype(BF16)
    lo = (r1 - mid.astype(F32)).astype(BF16)
    t = (jnp.dot(hi, onehot, preferred_element_type=F32)
         + jnp.dot(mid, onehot, preferred_element_type=F32)
         + jnp.dot(lo, onehot, preferred_element_type=F32))
    cq1 = cq[:1]
    ck1 = ck[:1]
    cs = jnp.clip(cq1 - WIN_COLS // 2, 0, GRID_W - WIN_COLS)
    in_win = jnp.logical_and(ck1 >= cs, ck1 < cs + WIN_COLS)
    o_ref[...] = jnp.where(in_win, t * LOG2E, NEG_INF)


def _bias_pair_table(rpb):
    r = jnp.pad(rpb.reshape(N_HEADS * N_DR, N_DC), ((0, 0), (0, 32 - N_DC)))
    t = pl.pallas_call(
        _bias_table_kernel,
        out_shape=jax.ShapeDtypeStruct((N_HEADS * N_DR, GRID_W * GRID_W), F32),
        compiler_params=pltpu.CompilerParams(vmem_limit_bytes=VMEM_LIMIT),
        name="rpb_table",
    )(r)
    t = t.reshape(N_HEADS, N_DR, GRID_W, GRID_W)
    neg = jnp.full((N_HEADS, 1, GRID_W, GRID_W), NEG_INF, F32)
    both = jnp.concatenate([t[:, 1:], t[:, :N_DR - 1]], axis=-1)
    left_only = jnp.concatenate([t[:, 3:4], neg], axis=-1)
    right_only = jnp.concatenate([neg, t[:, 10:11]], axis=-1)
    return jnp.concatenate([both, left_only, right_only], axis=1)


def _row_window_start(r):
    return min(max(r - WIN_ROWS // 2, 0), GRID_ROWS - WIN_ROWS)


def _pair_type(kr, r0):
    in0 = _row_window_start(r0) <= kr < _row_window_start(r0) + WIN_ROWS
    in1 = _row_window_start(r0 + 1) <= kr < _row_window_start(r0 + 1) + WIN_ROWS
    if in0 and in1:
        dl = kr - r0
        assert -6 <= dl <= 7
        return dl + 6
    if in0:
        assert kr - r0 == -(WIN_ROWS // 2)
        return 14
    if in1:
        assert kr - (r0 + 1) == WIN_ROWS // 2 - 1
        return 15
    return None


def _qkv_kernel(n_cast, xl_ref, xc_ref, m_ref, g_ref, w_ref, *refs):
    cast_in = refs[:n_cast]
    q_ref, k_ref, v_ref, kf_ref, vf_ref = refs[n_cast:n_cast + 5]
    cast_out = refs[n_cast + 5:2 * n_cast + 5]
    x_scr = refs[2 * n_cast + 5]
    _cast_chunks(cast_in, cast_out)
    x = _load_pair_tile(xl_ref, xc_ref, x_scr)
    xm = _modulate(x, g_ref[...], m_ref[0], m_ref[1])
    qkv = jnp.dot(xm.astype(BF16), w_ref[...], preferred_element_type=F32)
    q_ref[...] = (qkv[:, :D] * (ATTN_SCALE * LOG2E)).astype(BF16)
    k = qkv[:, D:2 * D]
    v = qkv[:, 2 * D:]
    k_ref[...] = k.astype(BF16)
    v_ref[...] = v.astype(BF16)
    kf_ref[...] = k
    vf_ref[...] = v


def _qkv_call(x_lat, x_ctx, mod, g, w_qkv, cast_weights):
    cast_specs, cast_shapes = zip(*[_cast_specs(w) for w in cast_weights])
    return pl.pallas_call(
        functools.partial(_qkv_kernel, len(cast_weights)),
        grid=(N_TILES,),
        in_specs=[_LAT_SPEC, _CTX_SPEC, _MOD_SPEC, _const_spec((1, D)), _const_spec((D, 3 * D)), *cast_specs],
        out_specs=[_ROW_SPEC, _ROW_SPEC, _ROW_SPEC, _CTX_SPEC, _CTX_SPEC, *cast_specs],
        out_shape=[jax.ShapeDtypeStruct((N_ROWS, D), BF16)] * 3
        + [jax.ShapeDtypeStruct((N_CTX_ROWS, D), F32)] * 2 + list(cast_shapes),
        scratch_shapes=[pltpu.VMEM((TM, D), F32)],
        compiler_params=_params(("arbitrary",)),
        name="qkv_proj",
    )(x_lat, x_ctx, mod, g, w_qkv, *cast_weights)


def _kq(k, q):
    return lax.dot_general(k, q, (((1,), (1,)), ((), ())), preferred_element_type=F32)


ATTN_SLOTS = 3


def _attn_pipeline(n_units, scores_fn, probs_fn, out_fn):
    assert ATTN_SLOTS == 3
    col_max = {n: scores_fn(n) for n in range(min(2, n_units))}
    col_sum = {}
    outs = []
    for n in range(n_units):
        if n + 2 < n_units:
            col_max[n + 2] = scores_fn(n + 2)
        col_sum[n] = probs_fn(n, col_max[n])
        if n >= 1:
            outs.append(out_fn(n - 1, col_sum[n - 1]))
    outs.append(out_fn(n_units - 1, col_sum[n_units - 1]))
    return outs


def _store_scores(s_ref, slot, row0, s):
    s_ref[slot, row0:row0 + s.shape[0], :] = s
    return s.max(axis=0, keepdims=True)


def _store_probs(s_ref, p_ref, slot, row0, rows, m):
    p = jnp.exp2(s_ref[slot, row0:row0 + rows, :] - m)
    p_ref[slot, row0:row0 + rows, :] = p.astype(BF16)
    return p.sum(axis=0, keepdims=True)


def _head_masks(n_q):
    lane = lax.broadcasted_iota(jnp.int32, (n_q, 2 * HEAD_DIM), 1)
    return lane < HEAD_DIM, lane >= HEAD_DIM


def _ctx_attn_kernel(q_ref, k_ref, v_ref, o_ref, s_ref, p_ref):
    v_t = v_ref[...].T
    masks = _head_masks(T_CTX)

    def scores(h):
        cols = slice((h // 2) * 2 * HEAD_DIM, (h // 2 + 1) * 2 * HEAD_DIM)
        q = q_ref[:, cols]
        qm = jnp.where(masks[h % 2], q, jnp.zeros_like(q))
        return _store_scores(s_ref, h % ATTN_SLOTS, 0, _kq(k_ref[:, cols], qm))

    def probs(h, m):
        return _store_probs(s_ref, p_ref, h % ATTN_SLOTS, 0, T_CTX, m)

    def out(h, l):
        o = jnp.dot(v_t[h * HEAD_DIM:(h + 1) * HEAD_DIM, :], p_ref[h % ATTN_SLOTS], preferred_element_type=F32)
        return o / l

    outs = _attn_pipeline(N_HEADS, scores, probs, out)
    o_ref[...] = jnp.concatenate(outs, axis=0).T.astype(BF16)


def _ctx_attn_call(q, k, v):
    blk = pl.BlockSpec((T_CTX, D), lambda b: (N_LAT_ROWS // T_CTX + b, 0))
    return pl.pallas_call(
        _ctx_attn_kernel,
        grid=(B_CTX,),
        in_specs=[blk, blk, blk],
        out_specs=pl.BlockSpec((T_CTX, D), lambda b: (b, 0)),
        out_shape=jax.ShapeDtypeStruct((N_CTX_ROWS, D), BF16),
        scratch_shapes=[pltpu.VMEM((ATTN_SLOTS, T_CTX, T_CTX), F32), pltpu.VMEM((ATTN_SLOTS, T_CTX, T_CTX), BF16)],
        compiler_params=_params(("arbitrary",)),
        name="ctx_attn",
    )(q, k, v)


def _key_rows_of_block(i):
    r_first = Q_BLK_ROWS * i
    return _row_window_start(r_first), _row_window_start(r_first + Q_BLK_ROWS - 1) + WIN_ROWS


N_Q_BLOCKS = GRID_ROWS // Q_BLK_ROWS
MAX_LAT_KEYS = max(kr1 - kr0 for kr0, kr1 in map(_key_rows_of_block, range(N_Q_BLOCKS))) * GRID_W


def _nbr_attn_kernel(q_ref, k_ref, v_ref, ck_ref, cv_ref, pt_ref, o_ref, s_ref, p_ref):
    v_t = v_ref[...].T
    cv_t = cv_ref[...].T
    n_q = Q_BLK_ROWS * GRID_W
    masks = _head_masks(n_q)
    neg_blk = jnp.full((GRID_W, 2 * GRID_W), NEG_INF, F32)

    def unit(n):
        return (n // 2, n % 2, n % ATTN_SLOTS) + _key_rows_of_block(n // 2)

    def scores(n):
        i, hh, slot, kr0, kr1 = unit(n)
        r_first = Q_BLK_ROWS * i
        q = q_ref[r_first * GRID_W:(r_first + Q_BLK_ROWS) * GRID_W, :]
        qm = jnp.where(masks[hh], q, jnp.zeros_like(q))
        bias_rows = []
        for kr in range(kr0, kr1):
            blks = []
            for r0 in range(r_first, r_first + Q_BLK_ROWS, 2):
                typ = _pair_type(kr, r0)
                blks.append(neg_blk if typ is None else pt_ref[hh, typ])
            bias_rows.append(jnp.concatenate(blks, axis=1))
        s_lat = _kq(k_ref[kr0 * GRID_W:kr1 * GRID_W, :], qm) + jnp.concatenate(bias_rows, axis=0)
        m_lat = _store_scores(s_ref, slot, 0, s_lat)
        m_ctx = _store_scores(s_ref, slot, MAX_LAT_KEYS, _kq(ck_ref[...], qm))
        return jnp.maximum(m_lat, m_ctx)

    def probs(n, m):
        _, _, slot, kr0, kr1 = unit(n)
        return (_store_probs(s_ref, p_ref, slot, 0, (kr1 - kr0) * GRID_W, m)
                + _store_probs(s_ref, p_ref, slot, MAX_LAT_KEYS, PAST, m))

    def out(n, l):
        _, hh, slot, kr0, kr1 = unit(n)
        hs = slice(hh * HEAD_DIM, (hh + 1) * HEAD_DIM)
        lk = (kr1 - kr0) * GRID_W
        o = (jnp.dot(v_t[hs, kr0 * GRID_W:kr1 * GRID_W], p_ref[slot, 0:lk, :], preferred_element_type=F32)
             + jnp.dot(cv_t[hs, :], p_ref[slot, MAX_LAT_KEYS:MAX_LAT_KEYS + PAST, :], preferred_element_type=F32))
        return o / l

    outs = _attn_pipeline(2 * N_Q_BLOCKS, scores, probs, out)
    blocks = [jnp.concatenate(outs[2 * i:2 * i + 2], axis=0) for i in range(N_Q_BLOCKS)]
    o_ref[...] = jnp.concatenate(blocks, axis=1).T.astype(BF16)


def _nbr_attn_call(q, k, v, cache_k, cache_v, pair_table):
    hw = 2 * HEAD_DIM
    blk = pl.BlockSpec((T_LAT, hw), lambda b, hp: (b, hp))
    cache = pl.BlockSpec((None, PAST, hw), lambda b, hp: (b, 0, hp))
    return pl.pallas_call(
        _nbr_attn_kernel,
        grid=(B_LAT, N_HEADS // 2),
        in_specs=[blk, blk, blk, cache, cache,
                  pl.BlockSpec((2, N_PAIR_TYPES, GRID_W, 2 * GRID_W), lambda b, hp: (hp, 0, 0, 0))],
        out_specs=blk,
        out_shape=jax.ShapeDtypeStruct((N_LAT_ROWS, D), BF16),
        scratch_shapes=[pltpu.VMEM((ATTN_SLOTS, MAX_LAT_KEYS + PAST, Q_BLK_ROWS * GRID_W), F32),
                        pltpu.VMEM((ATTN_SLOTS, MAX_LAT_KEYS + PAST, Q_BLK_ROWS * GRID_W), BF16)],
        compiler_params=_params(("arbitrary", "arbitrary")),
        name="nbr_attn",
    )(q, k, v, cache_k, cache_v, pair_table)


def _post_kernel(paired_x, paired_out, final_norm, n_cast, *refs):
    n_x = 2 if paired_x else 1
    n_out = 2 if paired_out else 1
    x_refs = refs[:n_x]
    ol_ref, oc_ref, m_ref, g_ref, wo_ref, wgu_ref, wd_ref, fg_ref = refs[n_x:n_x + 8]
    cast_in = refs[n_x + 8:n_x + 8 + n_cast]
    out_refs = refs[n_x + 8 + n_cast:n_x + 8 + n_cast + n_out]
    cast_out = refs[n_x + 8 + n_cast + n_out:n_x + 8 + 2 * n_cast + n_out]
    scr = refs[n_x + 8 + 2 * n_cast + n_out:]
    o_scr, acc_ref = scr[0], scr[1]

    _cast_chunks(cast_in, cast_out)
    x = _load_pair_tile(x_refs[0], x_refs[1], scr[2]) if paired_x else x_refs[0][...]
    o = _load_pair_tile(ol_ref, oc_ref, o_scr)
    mix = jnp.dot(o, wo_ref[...], preferred_element_type=F32)
    x1 = x + m_ref[2] * mix
    h = _modulate(x1, g_ref[...], m_ref[3], m_ref[4]).astype(BF16)
    for c in range(D_FF // FF_CHUNK):
        cs = slice(c * FF_CHUNK, (c + 1) * FF_CHUNK)
        us = slice(D_FF + c * FF_CHUNK, D_FF + (c + 1) * FF_CHUNK)
        gate = jnp.dot(h, wgu_ref[:, cs], preferred_element_type=F32)
        up = jnp.dot(h, wgu_ref[:, us], preferred_element_type=F32)
        a = (gate * _sigmoid(gate) * up).astype(BF16)
        part = jnp.dot(a, wd_ref[cs, :], preferred_element_type=F32)
        if c == 0:
            acc_ref[...] = part
        else:
            acc_ref[...] += part
    x2 = x1 + m_ref[5] * acc_ref[...]
    if final_norm:
        ms = jnp.mean(x2 * x2, axis=-1, keepdims=True)
        x2 = x2 * lax.rsqrt(ms + RMS_EPS) * fg_ref[...]
    if paired_out:
        _store_pair_tile(out_refs[0], out_refs[1], x2)
    else:
        out_refs[0][...] = x2


def _post_call(xs, o_lat, o_ctx, mod, g, w_o, w_gu, w_down, final_g, final_norm, cast_weights=()):
    paired_x = len(xs) == 2
    paired_out = final_norm
    cast_specs, cast_shapes = zip(*[_cast_specs(w) for w in cast_weights]) if cast_weights else ((), ())
    if paired_out:
        out_specs = [_LAT_SPEC, _CTX_SPEC]
        out_shape = [jax.ShapeDtypeStruct((N_LAT_ROWS, D), F32), jax.ShapeDtypeStruct((N_CTX_ROWS, D), F32)]
    else:
        out_specs = [_ROW_SPEC]
        out_shape = [jax.ShapeDtypeStruct((N_ROWS, D), F32)]
    scratch = [pltpu.VMEM((TM, D), BF16), pltpu.VMEM((TM, D), F32)]
    if paired_x:
        scratch.append(pltpu.VMEM((TM, D), F32))
    return pl.pallas_call(
        functools.partial(_post_kernel, paired_x, paired_out, final_norm, len(cast_weights)),
        grid=(N_TILES,),
        in_specs=[
            *([_LAT_SPEC, _CTX_SPEC] if paired_x else [_ROW_SPEC]),
            _LAT_SPEC, _CTX_SPEC, _MOD_SPEC,
            _const_spec((1, D)),
            _const_spec((D, D)),
            _const_spec((D, 2 * D_FF)),
            _const_spec((D_FF, D)),
            _const_spec((1, D)),
            *cast_specs,
        ],
        out_specs=[*out_specs, *cast_specs],
        out_shape=[*out_shape, *cast_shapes],
        scratch_shapes=scratch,
        compiler_params=_params(("arbitrary",)),
        name="mix_out_ffn",
    )(*xs, o_lat, o_ctx, mod, g, w_o, w_gu, w_down, final_g, *cast_weights)


def _lru_in_kernel(x_ref, m_ref, g_ref, w_ref, gate_ref, xr_ref):
    xm = _modulate(x_ref[...], g_ref[...], m_ref[0], m_ref[1])
    y = jnp.dot(xm.astype(BF16), w_ref[...], preferred_element_type=F32)
    gate_ref[...] = jax.nn.gelu(y[:, :D]).astype(BF16)
    xr_ref[...] = y[:, D:]


def _lru_in_call(x, mod, g, w_in):
    return pl.pallas_call(
        _lru_in_kernel,
        grid=(N_TILES,),
        in_specs=[_ROW_SPEC, _MOD_SPEC, _const_spec((1, D)), _const_spec((D, 2 * D))],
        out_specs=[_ROW_SPEC, _ROW_SPEC],
        out_shape=[jax.ShapeDtypeStruct((N_ROWS, D), BF16), jax.ShapeDtypeStruct((N_ROWS, D), F32)],
        compiler_params=_params(("arbitrary",)),
        name="lru_in_proj",
    )(x, mod, g, w_in)


def _lru_scan_kernel(nb, nt, xr_ref, gg_ref, cw_ref, cb_ref, wa_ref, wi_ref, ba_ref, bi_ref, lam_ref,
                     h0_ref, y_ref, ht_ref, xt_ref, hf_ref, hb_ref, af_ref, bf_ref, ab_ref, bb_ref):
    left = CONV_W // 2
    pad_rows = (CONV_W - 1) * nb
    xt_ref[pl.ds(0, left * nb), :] = jnp.zeros((left * nb, LRU_BW), F32)
    xt_ref[pl.ds((left + nt) * nb, pad_rows - left * nb), :] = jnp.zeros((pad_rows - left * nb, LRU_BW), F32)
    for s in range(nb):
        xt_ref[pl.ds(left * nb + s, nt, stride=nb), :] = xr_ref[pl.ds(s * nt, nt), :]

    cw = cw_ref[...]
    cb = cb_ref[...]
    w_f = jnp.concatenate([wa_ref[0], wi_ref[0]], axis=1).astype(BF16)
    w_b = jnp.concatenate([wa_ref[1], wi_ref[1]], axis=1).astype(BF16)
    neg_lam = -lam_ref[...]
    sp = jnp.maximum(neg_lam, 0.0) + jnp.log1p(jnp.exp(-jnp.abs(neg_lam)))
    crows = SCAN_CT * nb

    def coeffs(t0, w, d):
        xc = cb
        for j in range(CONV_W):
            xc = xc + xt_ref[pl.ds(pl.multiple_of((t0 + j) * nb, 8), crows), :] * cw[j:j + 1]
        pre = jnp.dot(xc.astype(BF16), w, preferred_element_type=F32)
        r = _sigmoid(pre[:, :LRU_BW] + ba_ref[d:d + 1])
        gi = _sigmoid(pre[:, LRU_BW:] + bi_ref[d:d + 1])
        a = jnp.exp((-LRU_C) * r * sp[d:d + 1])
        b = jnp.sqrt(1.0 - a * a) * (gi * xc)
        return a, b

    n_chunks = nt // SCAN_CT

    def chunk(c, carry):
        hf, hb = carry
        tf = c * SCAN_CT
        tb = (n_chunks - 1 - c) * SCAN_CT
        a, b = coeffs(tf, w_f, 0)
        af_ref[...] = a
        bf_ref[...] = b
        a, b = coeffs(tb, w_b, 1)
        ab_ref[...] = a
        bb_ref[...] = b
        for s in range(SCAN_CT):
            fs = slice(s * nb, (s + 1) * nb)
            hf = af_ref[fs, :] * hf + bf_ref[fs, :]
            hf_ref[pl.ds(pl.multiple_of((tf + s) * nb, 8), nb), :] = hf
            u = SCAN_CT - 1 - s
            us = slice(u * nb, (u + 1) * nb)
            hb = ab_ref[us, :] * hb + bb_ref[us, :]
            hb_ref[pl.ds(pl.multiple_of((tb + u) * nb, 8), nb), :] = hb
        return hf, hb

    hf, hb = lax.fori_loop(0, n_chunks, chunk, (h0_ref[0], h0_ref[1]))
    ht_ref[0] = hf
    ht_ref[1] = hb

    for s in range(nb):
        hsum = hf_ref[pl.ds(s, nt, stride=nb), :] + hb_ref[pl.ds(s, nt, stride=nb), :]
        y_ref[pl.ds(s * nt, nt), :] = (hsum * gg_ref[pl.ds(s * nt, nt), :].astype(F32)).astype(BF16)


def _lru_scan_call(nb, nt, row_block0, gg, xr, h0, conv_w, conv_b, w_a, w_i, b_a, b_i, lam):
    rows = nb * nt
    col = pl.BlockSpec((rows, LRU_BW), lambda n: (row_block0, n))
    vec2 = pl.BlockSpec((2, LRU_BW), lambda n: (0, n))
    wblk = pl.BlockSpec((2, None, LRU_BW, LRU_BW), lambda n: (0, n, 0, 0))
    state = pl.BlockSpec((2, nb, LRU_BW), lambda n: (0, 0, n))
    tm_rows = (nt + CONV_W - 1) * nb
    crows = SCAN_CT * nb
    return pl.pallas_call(
        functools.partial(_lru_scan_kernel, nb, nt),
        grid=(LRU_BLOCKS,),
        in_specs=[
            col, col,
            pl.BlockSpec((CONV_W, LRU_BW), lambda n: (0, n)),
            pl.BlockSpec((1, LRU_BW), lambda n: (0, n)),
            wblk, wblk, vec2, vec2, vec2, state,
        ],
        out_specs=[pl.BlockSpec((rows, LRU_BW), lambda n: (0, n)), state],
        out_shape=[jax.ShapeDtypeStruct((rows, D), BF16), jax.ShapeDtypeStruct((2, nb, D), F32)],
        scratch_shapes=[pltpu.VMEM((tm_rows, LRU_BW), F32),
                        pltpu.VMEM((rows, LRU_BW), F32),
                        pltpu.VMEM((rows, LRU_BW), F32)]
        + [pltpu.VMEM((crows, LRU_BW), F32)] * 4,
        compiler_params=_params(("arbitrary",)),
        name=f"lru_scan_{nb}x{nt}",
    )(xr, gg, conv_w, conv_b, w_a, w_i, b_a, b_i, lam, h0)


def kernel(x_prompt, x_sample, c, cache_k, cache_v, state_h, c_ctx, norm_g, w_mod, b_mod, attn_w_qkv, attn_w_o,
           attn_rpb, lru_w_in, lru_conv_w, lru_conv_b, lru_w_a, lru_b_a, lru_w_i, lru_b_i, lru_lam, lru_w_out,
           ffn_w_gu, ffn_w_down, final_g):
    x_lat = x_sample.reshape(N_LAT_ROWS, D)
    x_ctx = x_prompt.reshape(N_CTX_ROWS, D)
    cond = jnp.concatenate([c, c_ctx[None, :], jnp.zeros((N_COND - B_LAT - 1, D), F32)], axis=0)
    mod = _mod_call(cond, w_mod, b_mod).reshape(2, N_COND, N_MOD, 1, D)
    final_g2 = final_g.reshape(1, D)

    q, k, v, k_ctx, v_ctx, w_o, w_gu0, w_down0 = _qkv_call(
        x_lat, x_ctx, mod[0], norm_g[0, 0].reshape(1, D), attn_w_qkv[0].astype(BF16),
        (attn_w_o[0], ffn_w_gu[0], ffn_w_down[0]))
    o_ctx = _ctx_attn_call(q, k, v)
    pair_table = _bias_pair_table(attn_rpb[0])
    o_lat = _nbr_attn_call(q, k, v, cache_k[:, 0].astype(BF16).reshape(B_LAT, PAST, D),
                           cache_v[:, 0].astype(BF16).reshape(B_LAT, PAST, D), pair_table)
    x1, w_in, w_out, w_gu1, w_down1 = _post_call(
        (x_lat, x_ctx), o_lat, o_ctx, mod[0], norm_g[0, 1].reshape(1, D), w_o, w_gu0, w_down0, final_g2, False,
        (lru_w_in[0], lru_w_out[0], ffn_w_gu[1], ffn_w_down[1]))

    gg, xr = _lru_in_call(x1, mod[1], norm_g[1, 0].reshape(1, D), w_in)
    lru_p = (lru_conv_w[0], lru_conv_b[0].reshape(1, D), lru_w_a[0], lru_w_i[0], lru_b_a[0], lru_b_i[0], lru_lam[0])
    y_lat, _ = _lru_scan_call(B_LAT, T_LAT, 0, gg, xr, jnp.swapaxes(state_h[:, 0], 0, 1), *lru_p)
    y_ctx, h_ctx = _lru_scan_call(B_CTX, T_CTX, N_LAT_ROWS // N_CTX_ROWS, gg, xr,
                                  jnp.zeros((2, B_CTX, D), F32), *lru_p)
    out_lat, out_ctx = _post_call((x1,), y_lat, y_ctx, mod[1], norm_g[1, 1].reshape(1, D), w_out, w_gu1, w_down1,
                                  final_g2, True)

    y_sample = out_lat.reshape(B_LAT, T_LAT, D)
    y_prompt = out_ctx.reshape(B_CTX, T_CTX, D)
    new_k = k_ctx.reshape(B_CTX, 1, T_CTX, N_HEADS, HEAD_DIM)
    new_v = v_ctx.reshape(B_CTX, 1, T_CTX, N_HEADS, HEAD_DIM)
    new_h = jnp.swapaxes(h_ctx, 0, 1)[:, None]
    return y_prompt, y_sample, new_k, new_v, new_h
```

```python
import functools
import math

import jax
import jax.numpy as jnp
from jax import lax
from jax.experimental import pallas as pl
from jax.experimental.pallas import tpu as pltpu

F32 = jnp.float32
BF16 = jnp.bfloat16

D = 1024
B_CTX, T_CTX = 16, 256
B_LAT, T_LAT = 8, 1024
PAST = 512
GRID_W = 64
GRID_ROWS = T_LAT // GRID_W
N_HEADS = 16
HEAD_DIM = D // N_HEADS
ATTN_SCALE = HEAD_DIM ** -0.5
LOG2E = math.log2(math.e)
WIN_ROWS, WIN_COLS = 8, 16
N_DR = 2 * WIN_ROWS - 1
N_DC = 2 * WIN_COLS - 1
LRU_BLOCKS = 8
LRU_BW = D // LRU_BLOCKS
CONV_W = 4
LRU_C = 8.0
D_FF = 2816
N_MOD = 6
RMS_EPS = 1e-6
NEG_INF = -1e30
LANES = 128

N_LAT_ROWS = B_LAT * T_LAT
N_CTX_ROWS = B_CTX * T_CTX
N_ROWS = N_LAT_ROWS + N_CTX_ROWS
N_COND = 16

TM = 512
N_LAT_TILES = N_LAT_ROWS // TM
N_CTX_TILES = N_CTX_ROWS // TM
N_TILES = N_LAT_TILES + N_CTX_TILES
TILES_PER_LAT_SAMPLE = T_LAT // TM
MOD_TN = 1536
FF_CHUNK = 256
Q_BLK_ROWS = 4
N_PAIR_TYPES = 16
CAST_CHUNKS = 16
SCAN_CT = 32
VMEM_LIMIT = 56 * 1024 * 1024


def _sigmoid(x):
    return 0.5 * jnp.tanh(0.5 * x) + 0.5


def _modulate(x, g, shift, scale):
    ms = jnp.mean(x * x, axis=-1, keepdims=True)
    return (x * lax.rsqrt(ms + RMS_EPS)) * g * (1.0 + scale) + shift


def _cond_row_of_tile(i):
    return jnp.where(i < N_LAT_TILES, i // TILES_PER_LAT_SAMPLE, B_LAT)


def _const_spec(shape):
    nd = len(shape)
    return pl.BlockSpec(shape, lambda *_: (0,) * nd, pipeline_mode=pl.Buffered(1))


def _params(sem):
    return pltpu.CompilerParams(dimension_semantics=sem, vmem_limit_bytes=VMEM_LIMIT)


_ROW_SPEC = pl.BlockSpec((TM, D), lambda i: (i, 0))
_LAT_SPEC = pl.BlockSpec((TM, D), lambda i: (jnp.minimum(i, N_LAT_TILES - 1), 0))
_CTX_SPEC = pl.BlockSpec((TM, D), lambda i: (jnp.maximum(i - N_LAT_TILES, 0), 0))
_MOD_SPEC = pl.BlockSpec((None, N_MOD, 1, D), lambda i: (_cond_row_of_tile(i), 0, 0, 0))


def _cast_specs(w, layer):
    _, rows, cols = w.shape
    chunk = lambda i: jnp.minimum(i, CAST_CHUNKS - 1)
    return (pl.BlockSpec((None, rows // CAST_CHUNKS, cols), lambda i: (layer, chunk(i), 0)),
            pl.BlockSpec((rows // CAST_CHUNKS, cols), lambda i: (chunk(i), 0)),
            jax.ShapeDtypeStruct((rows, cols), BF16))


def _cast_chunks(in_refs, out_refs):
    @pl.when(pl.program_id(0) < CAST_CHUNKS)
    def _():
        for src, dst in zip(in_refs, out_refs):
            dst[...] = src[...].astype(BF16)


def _mod_kernel(cond_ref, w_ref, b_ref, o_ref):
    c = cond_ref[...]
    s = (c * _sigmoid(c)).astype(BF16)
    o_ref[...] = jnp.dot(s, w_ref[...].astype(BF16), preferred_element_type=F32) + b_ref[...]


def _mod_call(cond, w_mod, b_mod):
    depth = w_mod.shape[0]
    n = N_MOD * D
    return pl.pallas_call(
        _mod_kernel,
        grid=(depth, n // MOD_TN),
        in_specs=[
            pl.BlockSpec((N_COND, D), lambda l, j: (0, 0)),
            pl.BlockSpec((None, D, MOD_TN), lambda l, j: (l, 0, j)),
            pl.BlockSpec((None, 1, MOD_TN), lambda l, j: (l, 0, j)),
        ],
        out_specs=pl.BlockSpec((None, N_COND, MOD_TN), lambda l, j: (l, 0, j)),
        out_shape=jax.ShapeDtypeStruct((depth, N_COND, n), F32),
        compiler_params=_params(("arbitrary", "arbitrary")),
        name="adaln_mod",
    )(cond, w_mod, b_mod.reshape(depth, 1, n))


def _pair_table_kernel(r_ref, o_ref):
    shape = (GRID_W, 2 * GRID_W)
    ck = lax.broadcasted_iota(jnp.int32, shape, 0)
    lane = lax.broadcasted_iota(jnp.int32, shape, 1)
    cq = jnp.bitwise_and(lane, GRID_W - 1)
    cs = jnp.clip(cq - WIN_COLS // 2, 0, GRID_W - WIN_COLS)
    in_win = jnp.logical_and(ck >= cs, ck < cs + WIN_COLS)

    def toeplitz(d, lane0):
        base = jnp.broadcast_to(r_ref[d:d + 1, :], shape)
        return pltpu.roll(base, (lane0 - (WIN_COLS - 1)) % LANES, 1, stride=1, stride_axis=0)

    for t in range(N_PAIR_TYPES):
        d_left, d_right = (t + 1, t) if t < 14 else ((3, None) if t == 14 else (None, 10))
        neg = jnp.full(shape, NEG_INF, F32)
        left = neg if d_left is None else toeplitz(d_left, 0)
        right = neg if d_right is None else toeplitz(d_right, GRID_W)
        val = jnp.where(lane < GRID_W, left, right)
        o_ref[t] = jnp.where(in_win, val * LOG2E, NEG_INF)


def _pair_table_call(rpb):
    assert 2 * GRID_W == LANES
    r = jnp.pad(rpb[:, :, ::-1], ((0, 0), (0, 0), (0, LANES - N_DC)))
    return pl.pallas_call(
        _pair_table_kernel,
        grid=(N_HEADS,),
        in_specs=[pl.BlockSpec((None, N_DR, LANES), lambda h: (h, 0, 0))],
        out_specs=pl.BlockSpec((None, N_PAIR_TYPES, GRID_W, LANES), lambda h: (h, 0, 0, 0)),
        out_shape=jax.ShapeDtypeStruct((N_HEADS, N_PAIR_TYPES, GRID_W, LANES), F32),
        compiler_params=_params(("arbitrary",)),
        name="rpb_table",
    )(r)


def _row_window_start(r):
    return min(max(r - WIN_ROWS // 2, 0), GRID_ROWS - WIN_ROWS)


def _pair_type(kr, r0):
    in0 = _row_window_start(r0) <= kr < _row_window_start(r0) + WIN_ROWS
    in1 = _row_window_start(r0 + 1) <= kr < _row_window_start(r0 + 1) + WIN_ROWS
    if in0 and in1:
        dl = kr - r0
        assert -6 <= dl <= 7
        return dl + 6
    if in0:
        assert kr - r0 == -(WIN_ROWS // 2)
        return 14
    if in1:
        assert kr - (r0 + 1) == WIN_ROWS // 2 - 1
        return 15
    return None


def _qkv_kernel(n_cast, xl_ref, xc_ref, m_ref, g_ref, w_ref, *refs):
    cast_in = refs[:n_cast]
    q_ref, k_ref, v_ref, kf_ref, vf_ref, x_ref = refs[n_cast:n_cast + 6]
    cast_out = refs[n_cast + 6:]
    _cast_chunks(cast_in, cast_out)
    x = jnp.where(pl.program_id(0) < N_LAT_TILES, xl_ref[...], xc_ref[...])
    x_ref[...] = x
    xm = _modulate(x, g_ref[...], m_ref[0], m_ref[1])
    qkv = jnp.dot(xm.astype(BF16), w_ref[...], preferred_element_type=F32)
    q_ref[...] = (qkv[:, :D] * (ATTN_SCALE * LOG2E)).astype(BF16)
    k = qkv[:, D:2 * D]
    v = qkv[:, 2 * D:]
    k_ref[...] = k.astype(BF16)
    v_ref[...] = v.astype(BF16)
    kf_ref[...] = k
    vf_ref[...] = v


def _qkv_call(x_lat, x_ctx, mod, g, w_qkv, cast_weights):
    cast_in, cast_out, cast_shapes = zip(*[_cast_specs(w, l) for w, l in cast_weights])
    return pl.pallas_call(
        functools.partial(_qkv_kernel, len(cast_weights)),
        grid=(N_TILES,),
        in_specs=[_LAT_SPEC, _CTX_SPEC, _MOD_SPEC, _const_spec((1, D)), _const_spec((D, 3 * D)), *cast_in],
        out_specs=[_ROW_SPEC, _ROW_SPEC, _ROW_SPEC, _CTX_SPEC, _CTX_SPEC, _ROW_SPEC, *cast_out],
        out_shape=[jax.ShapeDtypeStruct((N_ROWS, D), BF16)] * 3
        + [jax.ShapeDtypeStruct((N_CTX_ROWS, D), F32)] * 2
        + [jax.ShapeDtypeStruct((N_ROWS, D), F32)] + list(cast_shapes),
        compiler_params=_params(("arbitrary",)),
        name="qkv_proj",
    )(x_lat, x_ctx, mod, g, w_qkv, *[w for w, _ in cast_weights])


ATTN_SLOTS = 3


def _kq(k, q):
    return lax.dot_general(k, q, (((1,), (1,)), ((), ())), preferred_element_type=F32)


def _attn_pipeline(n_units, scores_fn, probs_fn, out_fn):
    assert ATTN_SLOTS == 3
    col_max = {n: scores_fn(n) for n in range(min(2, n_units))}
    col_sum = {}
    outs = []
    for n in range(n_units):
        if n + 2 < n_units:
            col_max[n + 2] = scores_fn(n + 2)
        col_sum[n] = probs_fn(n, col_max[n])
        if n >= 1:
            outs.append(out_fn(n - 1, col_sum[n - 1]))
    outs.append(out_fn(n_units - 1, col_sum[n_units - 1]))
    return outs


def _store_scores(s_ref, slot, row0, s):
    s_ref[slot, row0:row0 + s.shape[0], :] = s
    return s.max(axis=0, keepdims=True)


def _store_probs(s_ref, p_ref, slot, row0, rows, m):
    p = jnp.exp2(s_ref[slot, row0:row0 + rows, :] - m)
    p_ref[slot, row0:row0 + rows, :] = p.astype(BF16)
    return p.sum(axis=0, keepdims=True)


def _head_masks(n_q):
    lane = lax.broadcasted_iota(jnp.int32, (n_q, 2 * HEAD_DIM), 1)
    return lane < HEAD_DIM, lane >= HEAD_DIM


def _ctx_attn_kernel(q_ref, k_ref, v_ref, o_ref, s_ref, p_ref):
    v_t = v_ref[...].T
    masks = _head_masks(T_CTX)

    def scores(h):
        cols = slice((h // 2) * 2 * HEAD_DIM, (h // 2 + 1) * 2 * HEAD_DIM)
        q = q_ref[:, cols]
        qm = jnp.where(masks[h % 2], q, jnp.zeros_like(q))
        return _store_scores(s_ref, h % ATTN_SLOTS, 0, _kq(k_ref[:, cols], qm))

    def probs(h, m):
        return _store_probs(s_ref, p_ref, h % ATTN_SLOTS, 0, T_CTX, m)

    def out(h, l):
        o = jnp.dot(v_t[h * HEAD_DIM:(h + 1) * HEAD_DIM, :], p_ref[h % ATTN_SLOTS], preferred_element_type=F32)
        return o / l

    outs = _attn_pipeline(N_HEADS, scores, probs, out)
    o_ref[...] = jnp.concatenate(outs, axis=0).T.astype(BF16)


def _ctx_attn_call(q, k, v):
    blk = pl.BlockSpec((T_CTX, D), lambda b: (N_LAT_ROWS // T_CTX + b, 0))
    return pl.pallas_call(
        _ctx_attn_kernel,
        grid=(B_CTX,),
        in_specs=[blk, blk, blk],
        out_specs=blk,
        out_shape=jax.ShapeDtypeStruct((N_ROWS, D), BF16),
        scratch_shapes=[pltpu.VMEM((ATTN_SLOTS, T_CTX, T_CTX), F32), pltpu.VMEM((ATTN_SLOTS, T_CTX, T_CTX), BF16)],
        compiler_params=_params(("arbitrary",)),
        name="ctx_attn",
    )(q, k, v)


def _key_rows_of_block(i):
    r_first = Q_BLK_ROWS * i
    return _row_window_start(r_first), _row_window_start(r_first + Q_BLK_ROWS - 1) + WIN_ROWS


N_Q_BLOCKS = GRID_ROWS // Q_BLK_ROWS
MAX_LAT_KEYS = max(kr1 - kr0 for kr0, kr1 in map(_key_rows_of_block, range(N_Q_BLOCKS))) * GRID_W


def _nbr_attn_kernel(q_ref, k_ref, v_ref, ck_ref, cv_ref, pt_ref, o_in_ref, o_ref, s_ref, p_ref):
    del o_in_ref
    v_t = v_ref[...].T
    ck = ck_ref[...].astype(BF16)
    cv_t = cv_ref[...].astype(BF16).T
    n_q = Q_BLK_ROWS * GRID_W
    masks = _head_masks(n_q)
    neg_blk = jnp.full((GRID_W, 2 * GRID_W), NEG_INF, F32)

    def unit(n):
        return (n // 2, n % 2, n % ATTN_SLOTS) + _key_rows_of_block(n // 2)

    def scores(n):
        i, hh, slot, kr0, kr1 = unit(n)
        r_first = Q_BLK_ROWS * i
        q = q_ref[r_first * GRID_W:(r_first + Q_BLK_ROWS) * GRID_W, :]
        qm = jnp.where(masks[hh], q, jnp.zeros_like(q))
        bias_rows = []
        for kr in range(kr0, kr1):
            blks = []
            for r0 in range(r_first, r_first + Q_BLK_ROWS, 2):
                typ = _pair_type(kr, r0)
                blks.append(neg_blk if typ is None else pt_ref[hh, typ])
            bias_rows.append(jnp.concatenate(blks, axis=1))
        s_lat = _kq(k_ref[kr0 * GRID_W:kr1 * GRID_W, :], qm) + jnp.concatenate(bias_rows, axis=0)
        m_lat = _store_scores(s_ref, slot, 0, s_lat)
        m_ctx = _store_scores(s_ref, slot, MAX_LAT_KEYS, _kq(ck, qm))
        return jnp.maximum(m_lat, m_ctx)

    def probs(n, m):
        _, _, slot, kr0, kr1 = unit(n)
        return (_store_probs(s_ref, p_ref, slot, 0, (kr1 - kr0) * GRID_W, m)
                + _store_probs(s_ref, p_ref, slot, MAX_LAT_KEYS, PAST, m))

    def out(n, l):
        _, hh, slot, kr0, kr1 = unit(n)
        hs = slice(hh * HEAD_DIM, (hh + 1) * HEAD_DIM)
        lk = (kr1 - kr0) * GRID_W
        o = (jnp.dot(v_t[hs, kr0 * GRID_W:kr1 * GRID_W], p_ref[slot, 0:lk, :], preferred_element_type=F32)
             + jnp.dot(cv_t[hs, :], p_ref[slot, MAX_LAT_KEYS:MAX_LAT_KEYS + PAST, :], preferred_element_type=F32))
        return o / l

    outs = _attn_pipeline(2 * N_Q_BLOCKS, scores, probs, out)
    blocks = [jnp.concatenate(outs[2 * i:2 * i + 2], axis=0) for i in range(N_Q_BLOCKS)]
    o_ref[...] = jnp.concatenate(blocks, axis=1).T.astype(BF16)


def _nbr_attn_call(q, k, v, cache_k, cache_v, pair_table, o_ctx):
    hw = 2 * HEAD_DIM
    blk = pl.BlockSpec((T_LAT, hw), lambda b, hp: (b, hp))
    cache = pl.BlockSpec((None, PAST, hw), lambda b, hp: (b, 0, hp))
    return pl.pallas_call(
        _nbr_attn_kernel,
        grid=(B_LAT, N_HEADS // 2),
        in_specs=[blk, blk, blk, cache, cache,
                  pl.BlockSpec((2, N_PAIR_TYPES, GRID_W, 2 * GRID_W), lambda b, hp: (hp, 0, 0, 0)),
                  pl.BlockSpec(memory_space=pl.ANY)],
        out_specs=blk,
        out_shape=jax.ShapeDtypeStruct((N_ROWS, D), BF16),
        input_output_aliases={6: 0},
        scratch_shapes=[pltpu.VMEM((ATTN_SLOTS, MAX_LAT_KEYS + PAST, Q_BLK_ROWS * GRID_W), F32),
                        pltpu.VMEM((ATTN_SLOTS, MAX_LAT_KEYS + PAST, Q_BLK_ROWS * GRID_W), BF16)],
        compiler_params=_params(("arbitrary", "arbitrary")),
        name="nbr_attn",
    )(q, k, v, cache_k, cache_v, pair_table, o_ctx)


def _post_kernel(final_norm, n_cast, x_ref, o_ref, m_ref, g_ref, wo_ref, wgu_ref, wd_ref, fg_ref, *refs):
    n_out = 2 if final_norm else 1
    cast_in = refs[:n_cast]
    out_refs = refs[n_cast:n_cast + n_out]
    cast_out = refs[n_cast + n_out:2 * n_cast + n_out]
    acc_ref, x1_ref = refs[2 * n_cast + n_out:]

    _cast_chunks(cast_in, cast_out)
    mix = jnp.dot(o_ref[...], wo_ref[...], preferred_element_type=F32)
    x1 = x_ref[...] + m_ref[2] * mix
    x1_ref[...] = x1
    h = _modulate(x1, g_ref[...], m_ref[3], m_ref[4]).astype(BF16)
    for c in range(D_FF // FF_CHUNK):
        cs = slice(c * FF_CHUNK, (c + 1) * FF_CHUNK)
        us = slice(D_FF + c * FF_CHUNK, D_FF + (c + 1) * FF_CHUNK)
        gate = jnp.dot(h, wgu_ref[:, cs], preferred_element_type=F32)
        up = jnp.dot(h, wgu_ref[:, us], preferred_element_type=F32)
        a = (gate * _sigmoid(gate) * up).astype(BF16)
        part = jnp.dot(a, wd_ref[cs, :], preferred_element_type=F32)
        if c == 0:
            acc_ref[...] = part
        else:
            acc_ref[...] += part
    def finish(out_ref):
        x2 = x1_ref[...] + m_ref[5] * acc_ref[...]
        if final_norm:
            ms = jnp.mean(x2 * x2, axis=-1, keepdims=True)
            x2 = x2 * lax.rsqrt(ms + RMS_EPS) * fg_ref[...]
        out_ref[...] = x2

    if not final_norm:
        finish(out_refs[0])
        return
    is_lat = pl.program_id(0) < N_LAT_TILES
    pl.when(is_lat)(lambda: finish(out_refs[0]))
    pl.when(jnp.logical_not(is_lat))(lambda: finish(out_refs[1]))


def _post_call(x, o, mod, g, w_o, w_gu, w_down, final_g, final_norm, cast_weights=()):
    cast_in, cast_out, cast_shapes = (zip(*[_cast_specs(w, l) for w, l in cast_weights])
                                      if cast_weights else ((), (), ()))
    if final_norm:
        out_specs = [_LAT_SPEC, _CTX_SPEC]
        out_shape = [jax.ShapeDtypeStruct((N_LAT_ROWS, D), F32), jax.ShapeDtypeStruct((N_CTX_ROWS, D), F32)]
    else:
        out_specs = [_ROW_SPEC]
        out_shape = [jax.ShapeDtypeStruct((N_ROWS, D), F32)]
    return pl.pallas_call(
        functools.partial(_post_kernel, final_norm, len(cast_weights)),
        grid=(N_TILES,),
        in_specs=[
            _ROW_SPEC, _ROW_SPEC, _MOD_SPEC,
            _const_spec((1, D)),
            _const_spec((D, D)),
            _const_spec((D, 2 * D_FF)),
            _const_spec((D_FF, D)),
            _const_spec((1, D)),
            *cast_in,
        ],
        out_specs=[*out_specs, *cast_out],
        out_shape=[*out_shape, *cast_shapes],
        scratch_shapes=[pltpu.VMEM((TM, D), F32), pltpu.VMEM((TM, D), F32)],
        compiler_params=_params(("arbitrary",)),
        name="mix_out_ffn",
    )(x, o, mod, g, w_o, w_gu, w_down, final_g, *[w for w, _ in cast_weights])


def _lru_in_kernel(x_ref, m_ref, g_ref, w_ref, gate_ref, xr_ref):
    xm = _modulate(x_ref[...], g_ref[...], m_ref[0], m_ref[1])
    y = jnp.dot(xm.astype(BF16), w_ref[...], preferred_element_type=F32)
    gate_ref[...] = jax.nn.gelu(y[:, :D]).astype(BF16)
    xr_ref[...] = y[:, D:]


def _lru_in_call(x, mod, g, w_in):
    return pl.pallas_call(
        _lru_in_kernel,
        grid=(N_TILES,),
        in_specs=[_ROW_SPEC, _MOD_SPEC, _const_spec((1, D)), _const_spec((D, 2 * D))],
        out_specs=[_ROW_SPEC, _ROW_SPEC],
        out_shape=[jax.ShapeDtypeStruct((N_ROWS, D), BF16), jax.ShapeDtypeStruct((N_ROWS, D), F32)],
        compiler_params=_params(("arbitrary",)),
        name="lru_in_proj",
    )(x, mod, g, w_in)


def _lru_scan_kernel(nb, nt, aliased, xr_ref, gg_ref, cw_ref, cb_ref, wa_ref, wi_ref, ba_ref, bi_ref, lam_ref,
                     h0_ref, *refs):
    y_ref, ht_ref, xt_ref, hf_ref, hb_ref, af_ref, bf_ref, ab_ref, bb_ref = refs[1:] if aliased else refs
    left = CONV_W // 2
    pad_rows = (CONV_W - 1) * nb
    xt_ref[pl.ds(0, left * nb), :] = jnp.zeros((left * nb, LRU_BW), F32)
    xt_ref[pl.ds((left + nt) * nb, pad_rows - left * nb), :] = jnp.zeros((pad_rows - left * nb, LRU_BW), F32)
    for s in range(nb):
        xt_ref[pl.ds(left * nb + s, nt, stride=nb), :] = xr_ref[pl.ds(s * nt, nt), :]

    cw = cw_ref[...]
    cb = cb_ref[...]
    w_f = jnp.concatenate([wa_ref[0], wi_ref[0]], axis=1).astype(BF16)
    w_b = jnp.concatenate([wa_ref[1], wi_ref[1]], axis=1).astype(BF16)
    neg_lam = -lam_ref[...]
    sp = jnp.maximum(neg_lam, 0.0) + jnp.log1p(jnp.exp(-jnp.abs(neg_lam)))
    crows = SCAN_CT * nb

    def coeffs(t0, w, d):
        xc = cb
        for j in range(CONV_W):
            xc = xc + xt_ref[pl.ds(pl.multiple_of((t0 + j) * nb, 8), crows), :] * cw[j:j + 1]
        pre = jnp.dot(xc.astype(BF16), w, preferred_element_type=F32)
        r = _sigmoid(pre[:, :LRU_BW] + ba_ref[d:d + 1])
        gi = _sigmoid(pre[:, LRU_BW:] + bi_ref[d:d + 1])
        a = jnp.exp((-LRU_C) * r * sp[d:d + 1])
        b = jnp.sqrt(1.0 - a * a) * (gi * xc)
        return a, b

    n_chunks = nt // SCAN_CT

    def chunk(c, carry):
        hf, hb = carry
        tf = c * SCAN_CT
        tb = (n_chunks - 1 - c) * SCAN_CT
        a, b = coeffs(tf, w_f, 0)
        af_ref[...] = a
        bf_ref[...] = b
        a, b = coeffs(tb, w_b, 1)
        ab_ref[...] = a
        bb_ref[...] = b
        for s in range(SCAN_CT):
            fs = slice(s * nb, (s + 1) * nb)
            hf = af_ref[fs, :] * hf + bf_ref[fs, :]
            hf_ref[pl.ds(pl.multiple_of((tf + s) * nb, 8), nb), :] = hf
            u = SCAN_CT - 1 - s
            us = slice(u * nb, (u + 1) * nb)
            hb = ab_ref[us, :] * hb + bb_ref[us, :]
            hb_ref[pl.ds(pl.multiple_of((tb + u) * nb, 8), nb), :] = hb
        return hf, hb

    hf, hb = lax.fori_loop(0, n_chunks, chunk, (h0_ref[0], h0_ref[1]))
    ht_ref[0] = hf
    ht_ref[1] = hb

    for s in range(nb):
        hsum = hf_ref[pl.ds(s, nt, stride=nb), :] + hb_ref[pl.ds(s, nt, stride=nb), :]
        y_ref[pl.ds(s * nt, nt), :] = (hsum * gg_ref[pl.ds(s * nt, nt), :].astype(F32)).astype(BF16)


def _lru_scan_call(nb, nt, row_block0, gg, xr, h0, conv_w, conv_b, w_a, w_i, b_a, b_i, lam, y_prev=None):
    rows = nb * nt
    col = pl.BlockSpec((rows, LRU_BW), lambda n: (row_block0, n))
    vec2 = pl.BlockSpec((2, LRU_BW), lambda n: (0, n))
    wblk = pl.BlockSpec((2, None, LRU_BW, LRU_BW), lambda n: (0, n, 0, 0))
    state = pl.BlockSpec((2, nb, LRU_BW), lambda n: (0, 0, n))
    tm_rows = (nt + CONV_W - 1) * nb
    crows = SCAN_CT * nb
    aliased = y_prev is not None
    return pl.pallas_call(
        functools.partial(_lru_scan_kernel, nb, nt, aliased),
        grid=(LRU_BLOCKS,),
        in_specs=[
            col, col,
            pl.BlockSpec((CONV_W, LRU_BW), lambda n: (0, n)),
            pl.BlockSpec((1, LRU_BW), lambda n: (0, n)),
            wblk, wblk, vec2, vec2, vec2, state,
        ] + ([pl.BlockSpec(memory_space=pl.ANY)] if aliased else []),
        out_specs=[col, state],
        out_shape=[jax.ShapeDtypeStruct((N_ROWS, D), BF16), jax.ShapeDtypeStruct((2, nb, D), F32)],
        input_output_aliases={10: 0} if aliased else {},
        scratch_shapes=[pltpu.VMEM((tm_rows, LRU_BW), F32),
                        pltpu.VMEM((rows, LRU_BW), F32),
                        pltpu.VMEM((rows, LRU_BW), F32)]
        + [pltpu.VMEM((crows, LRU_BW), F32)] * 4,
        compiler_params=_params(("arbitrary",)),
        name=f"lru_scan_{nb}x{nt}",
    )(xr, gg, conv_w, conv_b, w_a, w_i, b_a, b_i, lam, h0, *([y_prev] if aliased else []))


def kernel(x_prompt, x_sample, c, cache_k, cache_v, state_h, c_ctx, norm_g, w_mod, b_mod, attn_w_qkv, attn_w_o,
           attn_rpb, lru_w_in, lru_conv_w, lru_conv_b, lru_w_a, lru_b_a, lru_w_i, lru_b_i, lru_lam, lru_w_out,
           ffn_w_gu, ffn_w_down, final_g):
    x_lat = x_sample.reshape(N_LAT_ROWS, D)
    x_ctx = x_prompt.reshape(N_CTX_ROWS, D)
    cond = jnp.concatenate([c, c_ctx[None, :], jnp.zeros((N_COND - B_LAT - 1, D), F32)], axis=0)
    mod = _mod_call(cond, w_mod, b_mod).reshape(2, N_COND, N_MOD, 1, D)
    final_g2 = final_g.reshape(1, D)

    q, k, v, k_ctx, v_ctx, x, w_o, w_gu0, w_down0 = _qkv_call(
        x_lat, x_ctx, mod[0], norm_g[0, 0].reshape(1, D), attn_w_qkv[0].astype(BF16),
        ((attn_w_o, 0), (ffn_w_gu, 0), (ffn_w_down, 0)))
    o = _ctx_attn_call(q, k, v)
    o = _nbr_attn_call(q, k, v, cache_k[:, 0].reshape(B_LAT, PAST, D), cache_v[:, 0].reshape(B_LAT, PAST, D),
                       _pair_table_call(attn_rpb[0]), o)
    x, w_in, w_out, w_gu1, w_down1 = _post_call(
        x, o, mod[0], norm_g[0, 1].reshape(1, D), w_o, w_gu0, w_down0, final_g2, False,
        ((lru_w_in, 0), (lru_w_out, 0), (ffn_w_gu, 1), (ffn_w_down, 1)))

    gg, xr = _lru_in_call(x, mod[1], norm_g[1, 0].reshape(1, D), w_in)
    lru_p = (lru_conv_w[0], lru_conv_b[0].reshape(1, D), lru_w_a[0], lru_w_i[0], lru_b_a[0], lru_b_i[0], lru_lam[0])
    y, _ = _lru_scan_call(B_LAT, T_LAT, 0, gg, xr, jnp.swapaxes(state_h[:, 0], 0, 1), *lru_p)
    y, h_ctx = _lru_scan_call(B_CTX, T_CTX, N_LAT_ROWS // N_CTX_ROWS, gg, xr,
                              jnp.zeros((2, B_CTX, D), F32), *lru_p, y_prev=y)
    out_lat, out_ctx = _post_call(x, y, mod[1], norm_g[1, 1].reshape(1, D), w_out, w_gu1, w_down1, final_g2, True)

    y_sample = out_lat.reshape(B_LAT, T_LAT, D)
    y_prompt = out_ctx.reshape(B_CTX, T_CTX, D)
    new_k = k_ctx.reshape(B_CTX, 1, T_CTX, N_HEADS, HEAD_DIM)
    new_v = v_ctx.reshape(B_CTX, 1, T_CTX, N_HEADS, HEAD_DIM)
    new_h = jnp.swapaxes(h_ctx, 0, 1)[:, None]
    return y_prompt, y_sample, new_k, new_v, new_h
```

```python
import functools
import math

import jax
import jax.numpy as jnp
from jax import lax
from jax.experimental import pallas as pl
from jax.experimental.pallas import tpu as pltpu

F32 = jnp.float32
BF16 = jnp.bfloat16

D = 1024
B_CTX, T_CTX = 16, 256
B_LAT, T_LAT = 8, 1024
PAST = 512
GRID_W = 64
GRID_ROWS = T_LAT // GRID_W
N_HEADS = 16
HEAD_DIM = D // N_HEADS
ATTN_SCALE = HEAD_DIM ** -0.5
LOG2E = math.log2(math.e)
WIN_ROWS, WIN_COLS = 8, 16
N_DR = 2 * WIN_ROWS - 1
N_DC = 2 * WIN_COLS - 1
LRU_BLOCKS = 8
LRU_BW = D // LRU_BLOCKS
CONV_W = 4
LRU_C = 8.0
D_FF = 2816
N_MOD = 6
RMS_EPS = 1e-6
NEG_INF = -1e30
LANES = 128

N_LAT_ROWS = B_LAT * T_LAT
N_CTX_ROWS = B_CTX * T_CTX
N_ROWS = N_LAT_ROWS + N_CTX_ROWS
N_COND = 16

TM = 512
N_LAT_TILES = N_LAT_ROWS // TM
N_CTX_TILES = N_CTX_ROWS // TM
N_TILES = N_LAT_TILES + N_CTX_TILES
TILES_PER_LAT_SAMPLE = T_LAT // TM
MOD_TN = 1536
FF_CHUNK = 256
Q_BLK_ROWS = 4
N_PAIR_TYPES = 16
PAIR_TABLE_HEADS = 4
CAST_CHUNKS = 16
SCAN_CT = 32
VMEM_LIMIT = 56 * 1024 * 1024


def _sigmoid(x):
    return 0.5 * jnp.tanh(0.5 * x) + 0.5


def _modulate(x, g, shift, scale):
    ms = jnp.mean(x * x, axis=-1, keepdims=True)
    return (x * lax.rsqrt(ms + RMS_EPS)) * g * (1.0 + scale) + shift


def _cond_row_of_tile(i):
    return jnp.where(i < N_LAT_TILES, i // TILES_PER_LAT_SAMPLE, B_LAT)


def _const_spec(shape):
    nd = len(shape)
    return pl.BlockSpec(shape, lambda *_: (0,) * nd, pipeline_mode=pl.Buffered(1))


def _params(sem):
    return pltpu.CompilerParams(dimension_semantics=sem, vmem_limit_bytes=VMEM_LIMIT)


_ROW_SPEC = pl.BlockSpec((TM, D), lambda i: (i, 0))
_LAT_SPEC = pl.BlockSpec((TM, D), lambda i: (jnp.minimum(i, N_LAT_TILES - 1), 0))
_CTX_SPEC = pl.BlockSpec((TM, D), lambda i: (jnp.maximum(i - N_LAT_TILES, 0), 0))
_MOD_SPEC = pl.BlockSpec((None, N_MOD, 1, D), lambda i: (_cond_row_of_tile(i), 0, 0, 0))


def _cast_specs(w, layer):
    _, rows, cols = w.shape
    chunk = lambda i: jnp.minimum(i, CAST_CHUNKS - 1)
    return (pl.BlockSpec((None, rows // CAST_CHUNKS, cols), lambda i: (layer, chunk(i), 0)),
            pl.BlockSpec((rows // CAST_CHUNKS, cols), lambda i: (chunk(i), 0)),
            jax.ShapeDtypeStruct((rows, cols), BF16))


def _cast_chunks(in_refs, out_refs):
    @pl.when(pl.program_id(0) < CAST_CHUNKS)
    def _():
        for src, dst in zip(in_refs, out_refs):
            dst[...] = src[...].astype(BF16)


def _mod_kernel(cond_ref, w_ref, b_ref, o_ref):
    c = cond_ref[...]
    s = (c * _sigmoid(c)).astype(BF16)
    o_ref[...] = jnp.dot(s, w_ref[...].astype(BF16), preferred_element_type=F32) + b_ref[...]


def _mod_call(cond, w_mod, b_mod):
    depth = w_mod.shape[0]
    n = N_MOD * D
    return pl.pallas_call(
        _mod_kernel,
        grid=(depth, n // MOD_TN),
        in_specs=[
            pl.BlockSpec((N_COND, D), lambda l, j: (0, 0)),
            pl.BlockSpec((None, D, MOD_TN), lambda l, j: (l, 0, j)),
            pl.BlockSpec((None, 1, MOD_TN), lambda l, j: (l, 0, j)),
        ],
        out_specs=pl.BlockSpec((None, N_COND, MOD_TN), lambda l, j: (l, 0, j)),
        out_shape=jax.ShapeDtypeStruct((depth, N_COND, n), F32),
        compiler_params=_params(("arbitrary", "arbitrary")),
        name="adaln_mod",
    )(cond, w_mod, b_mod.reshape(depth, 1, n))


def _pair_table_kernel(r_ref, o_ref):
    shape = (GRID_W, 2 * GRID_W)
    ck = lax.broadcasted_iota(jnp.int32, shape, 0)
    lane = lax.broadcasted_iota(jnp.int32, shape, 1)
    cq = jnp.bitwise_and(lane, GRID_W - 1)
    cs = jnp.clip(cq - WIN_COLS // 2, 0, GRID_W - WIN_COLS)
    in_win = jnp.logical_and(ck >= cs, ck < cs + WIN_COLS)

    def toeplitz(h, d, lane0):
        base = jnp.broadcast_to(r_ref[h, d:d + 1, :], shape)
        return pltpu.roll(base, (lane0 - (WIN_COLS - 1)) % LANES, 1, stride=1, stride_axis=0)

    neg = jnp.full(shape, NEG_INF, F32)
    for h in range(PAIR_TABLE_HEADS):
        for t in range(N_PAIR_TYPES):
            d_left, d_right = (t + 1, t) if t < 14 else ((3, None) if t == 14 else (None, 10))
            left = neg if d_left is None else toeplitz(h, d_left, 0)
            right = neg if d_right is None else toeplitz(h, d_right, GRID_W)
            val = jnp.where(lane < GRID_W, left, right)
            o_ref[h, t] = jnp.where(in_win, val * LOG2E, NEG_INF)


def _pair_table_call(rpb):
    assert 2 * GRID_W == LANES
    r = jnp.pad(rpb[:, :, ::-1], ((0, 0), (0, 0), (0, LANES - N_DC)))
    return pl.pallas_call(
        _pair_table_kernel,
        grid=(N_HEADS // PAIR_TABLE_HEADS,),
        in_specs=[pl.BlockSpec((PAIR_TABLE_HEADS, N_DR, LANES), lambda h: (h, 0, 0))],
        out_specs=pl.BlockSpec((PAIR_TABLE_HEADS, N_PAIR_TYPES, GRID_W, LANES), lambda h: (h, 0, 0, 0)),
        out_shape=jax.ShapeDtypeStruct((N_HEADS, N_PAIR_TYPES, GRID_W, LANES), F32),
        compiler_params=_params(("arbitrary",)),
        name="rpb_table",
    )(r)


def _row_window_start(r):
    return min(max(r - WIN_ROWS // 2, 0), GRID_ROWS - WIN_ROWS)


def _pair_type(kr, r0):
    in0 = _row_window_start(r0) <= kr < _row_window_start(r0) + WIN_ROWS
    in1 = _row_window_start(r0 + 1) <= kr < _row_window_start(r0 + 1) + WIN_ROWS
    if in0 and in1:
        dl = kr - r0
        assert -6 <= dl <= 7
        return dl + 6
    if in0:
        assert kr - r0 == -(WIN_ROWS // 2)
        return 14
    if in1:
        assert kr - (r0 + 1) == WIN_ROWS // 2 - 1
        return 15
    return None


def _qkv_kernel(n_cast, xl_ref, xc_ref, m_ref, g_ref, w_ref, *refs):
    cast_in = refs[:n_cast]
    q_ref, k_ref, v_ref, kf_ref, vf_ref, x_ref = refs[n_cast:n_cast + 6]
    cast_out = refs[n_cast + 6:]
    _cast_chunks(cast_in, cast_out)
    x = jnp.where(pl.program_id(0) < N_LAT_TILES, xl_ref[...], xc_ref[...])
    x_ref[...] = x
    xm = _modulate(x, g_ref[...], m_ref[0], m_ref[1])
    qkv = jnp.dot(xm.astype(BF16), w_ref[...], preferred_element_type=F32)
    q_ref[...] = (qkv[:, :D] * (ATTN_SCALE * LOG2E)).astype(BF16)
    k = qkv[:, D:2 * D]
    v = qkv[:, 2 * D:]
    k_ref[...] = k.astype(BF16)
    v_ref[...] = v.astype(BF16)
    kf_ref[...] = k
    vf_ref[...] = v


def _qkv_call(x_lat, x_ctx, mod, g, w_qkv, cast_weights):
    cast_in, cast_out, cast_shapes = zip(*[_cast_specs(w, l) for w, l in cast_weights])
    return pl.pallas_call(
        functools.partial(_qkv_kernel, len(cast_weights)),
        grid=(N_TILES,),
        in_specs=[_LAT_SPEC, _CTX_SPEC, _MOD_SPEC, _const_spec((1, D)), _const_spec((D, 3 * D)), *cast_in],
        out_specs=[_ROW_SPEC, _ROW_SPEC, _ROW_SPEC, _CTX_SPEC, _CTX_SPEC, _ROW_SPEC, *cast_out],
        out_shape=[jax.ShapeDtypeStruct((N_ROWS, D), BF16)] * 3
        + [jax.ShapeDtypeStruct((N_CTX_ROWS, D), F32)] * 2
        + [jax.ShapeDtypeStruct((N_ROWS, D), F32)] + list(cast_shapes),
        compiler_params=_params(("arbitrary",)),
        name="qkv_proj",
    )(x_lat, x_ctx, mod, g, w_qkv, *[w for w, _ in cast_weights])


ATTN_SLOTS = 3


def _kq(k, q):
    return lax.dot_general(k, q, (((1,), (1,)), ((), ())), preferred_element_type=F32)


def _attn_pipeline(n_units, scores_fn, probs_fn, out_fn):
    assert ATTN_SLOTS == 3
    col_max = {n: scores_fn(n) for n in range(min(2, n_units))}
    col_sum = {}
    outs = []
    for n in range(n_units):
        if n + 2 < n_units:
            col_max[n + 2] = scores_fn(n + 2)
        col_sum[n] = probs_fn(n, col_max[n])
        if n >= 1:
            outs.append(out_fn(n - 1, col_sum[n - 1]))
    outs.append(out_fn(n_units - 1, col_sum[n_units - 1]))
    return outs


def _store_scores(s_ref, slot, row0, s):
    s_ref[slot, row0:row0 + s.shape[0], :] = s
    return s.max(axis=0, keepdims=True)


def _store_probs(s_ref, p_ref, slot, row0, rows, m):
    p = jnp.exp2(s_ref[slot, row0:row0 + rows, :] - m)
    p_ref[slot, row0:row0 + rows, :] = p.astype(BF16)
    return p.sum(axis=0, keepdims=True)


def _head_masks(n_q):
    lane = lax.broadcasted_iota(jnp.int32, (n_q, 2 * HEAD_DIM), 1)
    return lane < HEAD_DIM, lane >= HEAD_DIM


def _ctx_attn_kernel(q_ref, k_ref, v_ref, o_ref, s_ref, p_ref):
    v_t = v_ref[...].T
    masks = _head_masks(T_CTX)

    def scores(h):
        cols = slice((h // 2) * 2 * HEAD_DIM, (h // 2 + 1) * 2 * HEAD_DIM)
        q = q_ref[:, cols]
        qm = jnp.where(masks[h % 2], q, jnp.zeros_like(q))
        return _store_scores(s_ref, h % ATTN_SLOTS, 0, _kq(k_ref[:, cols], qm))

    def probs(h, m):
        return _store_probs(s_ref, p_ref, h % ATTN_SLOTS, 0, T_CTX, m)

    def out(h, l):
        o = jnp.dot(v_t[h * HEAD_DIM:(h + 1) * HEAD_DIM, :], p_ref[h % ATTN_SLOTS], preferred_element_type=F32)
        return o / l

    outs = _attn_pipeline(N_HEADS, scores, probs, out)
    o_ref[...] = jnp.concatenate(outs, axis=0).T.astype(BF16)


def _ctx_attn_call(q, k, v):
    blk = pl.BlockSpec((T_CTX, D), lambda b: (N_LAT_ROWS // T_CTX + b, 0))
    return pl.pallas_call(
        _ctx_attn_kernel,
        grid=(B_CTX,),
        in_specs=[blk, blk, blk],
        out_specs=blk,
        out_shape=jax.ShapeDtypeStruct((N_ROWS, D), BF16),
        scratch_shapes=[pltpu.VMEM((ATTN_SLOTS, T_CTX, T_CTX), F32), pltpu.VMEM((ATTN_SLOTS, T_CTX, T_CTX), BF16)],
        compiler_params=_params(("arbitrary",)),
        name="ctx_attn",
    )(q, k, v)


def _key_rows_of_block(i):
    r_first = Q_BLK_ROWS * i
    return _row_window_start(r_first), _row_window_start(r_first + Q_BLK_ROWS - 1) + WIN_ROWS


N_Q_BLOCKS = GRID_ROWS // Q_BLK_ROWS
MAX_LAT_KEYS = max(kr1 - kr0 for kr0, kr1 in map(_key_rows_of_block, range(N_Q_BLOCKS))) * GRID_W


def _nbr_attn_kernel(q_ref, k_ref, v_ref, ck_ref, cv_ref, pt_ref, o_in_ref, o_ref, s_ref, p_ref):
    del o_in_ref
    v_t = v_ref[...].T
    ck = ck_ref[...].astype(BF16)
    cv_t = cv_ref[...].astype(BF16).T
    n_q = Q_BLK_ROWS * GRID_W
    masks = _head_masks(n_q)
    neg_blk = jnp.full((GRID_W, 2 * GRID_W), NEG_INF, F32)

    def unit(n):
        return (n // 2, n % 2, n % ATTN_SLOTS) + _key_rows_of_block(n // 2)

    def scores(n):
        i, hh, slot, kr0, kr1 = unit(n)
        r_first = Q_BLK_ROWS * i
        q = q_ref[r_first * GRID_W:(r_first + Q_BLK_ROWS) * GRID_W, :]
        qm = jnp.where(masks[hh], q, jnp.zeros_like(q))
        bias_rows = []
        for kr in range(kr0, kr1):
            blks = []
            for r0 in range(r_first, r_first + Q_BLK_ROWS, 2):
                typ = _pair_type(kr, r0)
                blks.append(neg_blk if typ is None else pt_ref[hh, typ])
            bias_rows.append(jnp.concatenate(blks, axis=1))
        s_lat = _kq(k_ref[kr0 * GRID_W:kr1 * GRID_W, :], qm) + jnp.concatenate(bias_rows, axis=0)
        m_lat = _store_scores(s_ref, slot, 0, s_lat)
        m_ctx = _store_scores(s_ref, slot, MAX_LAT_KEYS, _kq(ck, qm))
        return jnp.maximum(m_lat, m_ctx)

    def probs(n, m):
        _, _, slot, kr0, kr1 = unit(n)
        return (_store_probs(s_ref, p_ref, slot, 0, (kr1 - kr0) * GRID_W, m)
                + _store_probs(s_ref, p_ref, slot, MAX_LAT_KEYS, PAST, m))

    def out(n, l):
        _, hh, slot, kr0, kr1 = unit(n)
        hs = slice(hh * HEAD_DIM, (hh + 1) * HEAD_DIM)
        lk = (kr1 - kr0) * GRID_W
        o = (jnp.dot(v_t[hs, kr0 * GRID_W:kr1 * GRID_W], p_ref[slot, 0:lk, :], preferred_element_type=F32)
             + jnp.dot(cv_t[hs, :], p_ref[slot, MAX_LAT_KEYS:MAX_LAT_KEYS + PAST, :], preferred_element_type=F32))
        return o / l

    outs = _attn_pipeline(2 * N_Q_BLOCKS, scores, probs, out)
    blocks = [jnp.concatenate(outs[2 * i:2 * i + 2], axis=0) for i in range(N_Q_BLOCKS)]
    o_ref[...] = jnp.concatenate(blocks, axis=1).T.astype(BF16)


def _nbr_attn_call(q, k, v, cache_k, cache_v, pair_table, o_ctx):
    hw = 2 * HEAD_DIM
    blk = pl.BlockSpec((T_LAT, hw), lambda b, hp: (b, hp))
    cache = pl.BlockSpec((None, PAST, hw), lambda b, hp: (b, 0, hp))
    return pl.pallas_call(
        _nbr_attn_kernel,
        grid=(B_LAT, N_HEADS // 2),
        in_specs=[blk, blk, blk, cache, cache,
                  pl.BlockSpec((2, N_PAIR_TYPES, GRID_W, 2 * GRID_W), lambda b, hp: (hp, 0, 0, 0)),
                  pl.BlockSpec(memory_space=pl.ANY)],
        out_specs=blk,
        out_shape=jax.ShapeDtypeStruct((N_ROWS, D), BF16),
        input_output_aliases={6: 0},
        scratch_shapes=[pltpu.VMEM((ATTN_SLOTS, MAX_LAT_KEYS + PAST, Q_BLK_ROWS * GRID_W), F32),
                        pltpu.VMEM((ATTN_SLOTS, MAX_LAT_KEYS + PAST, Q_BLK_ROWS * GRID_W), BF16)],
        compiler_params=_params(("arbitrary", "arbitrary")),
        name="nbr_attn",
    )(q, k, v, cache_k, cache_v, pair_table, o_ctx)


def _post_kernel(final_norm, n_cast, x_ref, o_ref, m_ref, g_ref, wo_ref, wgu_ref, wd_ref, fg_ref, *refs):
    n_out = 2 if final_norm else 1
    cast_in = refs[:n_cast]
    out_refs = refs[n_cast:n_cast + n_out]
    cast_out = refs[n_cast + n_out:2 * n_cast + n_out]
    acc_ref, x1_ref = refs[2 * n_cast + n_out:]

    _cast_chunks(cast_in, cast_out)
    mix = jnp.dot(o_ref[...], wo_ref[...], preferred_element_type=F32)
    x1 = x_ref[...] + m_ref[2] * mix
    x1_ref[...] = x1
    h = _modulate(x1, g_ref[...], m_ref[3], m_ref[4]).astype(BF16)
    for c in range(D_FF // FF_CHUNK):
        cs = slice(c * FF_CHUNK, (c + 1) * FF_CHUNK)
        us = slice(D_FF + c * FF_CHUNK, D_FF + (c + 1) * FF_CHUNK)
        gate = jnp.dot(h, wgu_ref[:, cs], preferred_element_type=F32)
        up = jnp.dot(h, wgu_ref[:, us], preferred_element_type=F32)
        a = (gate * _sigmoid(gate) * up).astype(BF16)
        part = jnp.dot(a, wd_ref[cs, :], preferred_element_type=F32)
        if c == 0:
            acc_ref[...] = part
        else:
            acc_ref[...] += part
    def finish(out_ref):
        x2 = x1_ref[...] + m_ref[5] * acc_ref[...]
        if final_norm:
            ms = jnp.mean(x2 * x2, axis=-1, keepdims=True)
            x2 = x2 * lax.rsqrt(ms + RMS_EPS) * fg_ref[...]
        out_ref[...] = x2

    if not final_norm:
        finish(out_refs[0])
        return
    is_lat = pl.program_id(0) < N_LAT_TILES
    pl.when(is_lat)(lambda: finish(out_refs[0]))
    pl.when(jnp.logical_not(is_lat))(lambda: finish(out_refs[1]))


def _post_call(x, o, mod, g, w_o, w_gu, w_down, final_g, final_norm, cast_weights=()):
    cast_in, cast_out, cast_shapes = (zip(*[_cast_specs(w, l) for w, l in cast_weights])
                                      if cast_weights else ((), (), ()))
    if final_norm:
        out_specs = [_LAT_SPEC, _CTX_SPEC]
        out_shape = [jax.ShapeDtypeStruct((N_LAT_ROWS, D), F32), jax.ShapeDtypeStruct((N_CTX_ROWS, D), F32)]
    else:
        out_specs = [_ROW_SPEC]
        out_shape = [jax.ShapeDtypeStruct((N_ROWS, D), F32)]
    return pl.pallas_call(
        functools.partial(_post_kernel, final_norm, len(cast_weights)),
        grid=(N_TILES,),
        in_specs=[
            _ROW_SPEC, _ROW_SPEC, _MOD_SPEC,
            _const_spec((1, D)),
            _const_spec((D, D)),
            _const_spec((D, 2 * D_FF)),
            _const_spec((D_FF, D)),
            _const_spec((1, D)),
            *cast_in,
        ],
        out_specs=[*out_specs, *cast_out],
        out_shape=[*out_shape, *cast_shapes],
        scratch_shapes=[pltpu.VMEM((TM, D), F32), pltpu.VMEM((TM, D), F32)],
        compiler_params=_params(("arbitrary",)),
        name="mix_out_ffn",
    )(x, o, mod, g, w_o, w_gu, w_down, final_g, *[w for w, _ in cast_weights])


def _lru_in_kernel(x_ref, m_ref, g_ref, w_ref, gate_ref, xr_ref):
    xm = _modulate(x_ref[...], g_ref[...], m_ref[0], m_ref[1])
    y = jnp.dot(xm.astype(BF16), w_ref[...], preferred_element_type=F32)
    gate_ref[...] = jax.nn.gelu(y[:, :D]).astype(BF16)
    xr_ref[...] = y[:, D:]


def _lru_in_call(x, mod, g, w_in):
    return pl.pallas_call(
        _lru_in_kernel,
        grid=(N_TILES,),
        in_specs=[_ROW_SPEC, _MOD_SPEC, _const_spec((1, D)), _const_spec((D, 2 * D))],
        out_specs=[_ROW_SPEC, _ROW_SPEC],
        out_shape=[jax.ShapeDtypeStruct((N_ROWS, D), BF16), jax.ShapeDtypeStruct((N_ROWS, D), F32)],
        compiler_params=_params(("arbitrary",)),
        name="lru_in_proj",
    )(x, mod, g, w_in)


def _lru_scan_kernel(nb, nt, aliased, xr_ref, gg_ref, cw_ref, cb_ref, wa_ref, wi_ref, ba_ref, bi_ref, lam_ref,
                     h0_ref, *refs):
    y_ref, ht_ref, xt_ref, hf_ref, ab_ref, bb_ref, af0, bf0, af1, bf1 = refs[1:] if aliased else refs
    fwd_slots = ((af0, bf0), (af1, bf1))
    left = CONV_W // 2
    pad_rows = (CONV_W - 1) * nb
    xt_ref[pl.ds(0, left * nb), :] = jnp.zeros((left * nb, LRU_BW), F32)
    xt_ref[pl.ds((left + nt) * nb, pad_rows - left * nb), :] = jnp.zeros((pad_rows - left * nb, LRU_BW), F32)
    for s in range(nb):
        xt_ref[pl.ds(left * nb + s, nt, stride=nb), :] = xr_ref[pl.ds(s * nt, nt), :]

    cw = cw_ref[...]
    cb = cb_ref[...]
    w4 = (jnp.concatenate([wa_ref[0], wi_ref[0], wa_ref[1], wi_ref[1]], axis=1) * 0.5).astype(BF16)
    half_ba = 0.5 * ba_ref[...]
    half_bi = 0.5 * bi_ref[...]
    neg_lam = -lam_ref[...]
    sp = jnp.maximum(neg_lam, 0.0) + jnp.log1p(jnp.exp(-jnp.abs(neg_lam)))
    c1 = (-0.5 * LRU_C * LOG2E) * sp
    crows = SCAN_CT * nb
    n_chunks = nt // SCAN_CT
    assert n_chunks % 2 == 0

    def chunk_rows(c):
        return pl.ds(pl.multiple_of(c * crows, 8), crows)

    def coeffs(c, slot):
        t0 = c * SCAN_CT
        xc = cb
        for j in range(CONV_W):
            xc = xc + xt_ref[pl.ds(pl.multiple_of((t0 + j) * nb, 8), crows), :] * cw[j:j + 1]
        half_xc = 0.5 * xc
        pre = jnp.dot(xc.astype(BF16), w4, preferred_element_type=F32)

        def direction(d):
            o = 2 * d * LRU_BW
            t_r = jnp.tanh(pre[:, o:o + LRU_BW] + half_ba[d:d + 1])
            a = jnp.exp2(c1[d:d + 1] * t_r + c1[d:d + 1])
            t_i = jnp.tanh(pre[:, o + LRU_BW:o + 2 * LRU_BW] + half_bi[d:d + 1])
            y = 1.0 - a * a
            root = jnp.where(y > 0.0, y * lax.rsqrt(y), 0.0)
            return a, root * (half_xc * t_i + half_xc)

        a_ref, b_ref = fwd_slots[slot]
        a_ref[...], b_ref[...] = direction(0)
        ab_ref[chunk_rows(c), :], bb_ref[chunk_rows(c), :] = direction(1)

    def scan_fwd(c, slot, h):
        a_ref, b_ref = fwd_slots[slot]
        for s in range(SCAN_CT):
            rows = slice(s * nb, (s + 1) * nb)
            h = a_ref[rows, :] * h + b_ref[rows, :]
            hf_ref[pl.ds(pl.multiple_of(c * crows + s * nb, 8), nb), :] = h
        return h

    def scan_bwd(c, h):
        for s in range(SCAN_CT - 1, 0, -2):
            rows1 = pl.ds(pl.multiple_of(c * crows + s * nb, 8), nb)
            rows0 = pl.ds(pl.multiple_of(c * crows + (s - 1) * nb, 8), nb)
            a1, b1 = ab_ref[rows1, :], bb_ref[rows1, :]
            a0, b0 = ab_ref[rows0, :], bb_ref[rows0, :]
            bb_ref[rows1, :] = a1 * h + b1
            h = (a0 * a1) * h + (a0 * b1 + b0)
            bb_ref[rows0, :] = h
        return h

    def emit(c):
        for s in range(nb):
            src = pl.ds(c * crows + s, SCAN_CT, stride=nb)
            dst = pl.ds(pl.multiple_of(s * nt + c * SCAN_CT, SCAN_CT), SCAN_CT)
            hsum = hf_ref[src, :] + bb_ref[src, :]
            y_ref[dst, :] = (hsum * gg_ref[dst, :].astype(F32)).astype(BF16)

    coeffs(0, 0)

    def pass1(k, h):
        c = 2 * k
        h = scan_fwd(c, 0, h)
        coeffs(c + 1, 1)
        h = scan_fwd(c + 1, 1, h)
        coeffs(jnp.minimum(c + 2, n_chunks - 1), 0)
        return h

    ht_ref[0] = lax.fori_loop(0, n_chunks // 2, pass1, h0_ref[0])

    hb = scan_bwd(n_chunks - 1, h0_ref[1])

    def pass2(k, h):
        c = n_chunks - 2 - k
        emit(c + 1)
        return scan_bwd(c, h)

    ht_ref[1] = lax.fori_loop(0, n_chunks - 1, pass2, hb)
    emit(0)


def _lru_scan_call(nb, nt, row_block0, gg, xr, h0, conv_w, conv_b, w_a, w_i, b_a, b_i, lam, y_prev=None):
    rows = nb * nt
    col = pl.BlockSpec((rows, LRU_BW), lambda n: (row_block0, n))
    vec2 = pl.BlockSpec((2, LRU_BW), lambda n: (0, n))
    wblk = pl.BlockSpec((2, None, LRU_BW, LRU_BW), lambda n: (0, n, 0, 0))
    state = pl.BlockSpec((2, nb, LRU_BW), lambda n: (0, 0, n))
    tm_rows = (nt + CONV_W - 1) * nb
    crows = SCAN_CT * nb
    aliased = y_prev is not None
    return pl.pallas_call(
        functools.partial(_lru_scan_kernel, nb, nt, aliased),
        grid=(LRU_BLOCKS,),
        in_specs=[
            col, col,
            pl.BlockSpec((CONV_W, LRU_BW), lambda n: (0, n)),
            pl.BlockSpec((1, LRU_BW), lambda n: (0, n)),
            wblk, wblk, vec2, vec2, vec2, state,
        ] + ([pl.BlockSpec(memory_space=pl.ANY)] if aliased else []),
        out_specs=[col, state],
        out_shape=[jax.ShapeDtypeStruct((N_ROWS, D), BF16), jax.ShapeDtypeStruct((2, nb, D), F32)],
        input_output_aliases={10: 0} if aliased else {},
        scratch_shapes=[pltpu.VMEM((tm_rows, LRU_BW), F32)] + [pltpu.VMEM((rows, LRU_BW), F32)] * 3
        + [pltpu.VMEM((crows, LRU_BW), F32)] * 4,
        compiler_params=_params(("arbitrary",)),
        name=f"lru_scan_{nb}x{nt}",
    )(xr, gg, conv_w, conv_b, w_a, w_i, b_a, b_i, lam, h0, *([y_prev] if aliased else []))


def kernel(x_prompt, x_sample, c, cache_k, cache_v, state_h, c_ctx, norm_g, w_mod, b_mod, attn_w_qkv, attn_w_o,
           attn_rpb, lru_w_in, lru_conv_w, lru_conv_b, lru_w_a, lru_b_a, lru_w_i, lru_b_i, lru_lam, lru_w_out,
           ffn_w_gu, ffn_w_down, final_g):
    x_lat = x_sample.reshape(N_LAT_ROWS, D)
    x_ctx = x_prompt.reshape(N_CTX_ROWS, D)
    cond = jnp.concatenate([c, c_ctx[None, :], jnp.zeros((N_COND - B_LAT - 1, D), F32)], axis=0)
    mod = _mod_call(cond, w_mod, b_mod).reshape(2, N_COND, N_MOD, 1, D)
    final_g2 = final_g.reshape(1, D)

    q, k, v, k_ctx, v_ctx, x, w_o, w_gu0, w_down0 = _qkv_call(
        x_lat, x_ctx, mod[0], norm_g[0, 0].reshape(1, D), attn_w_qkv[0].astype(BF16),
        ((attn_w_o, 0), (ffn_w_gu, 0), (ffn_w_down, 0)))
    o = _ctx_attn_call(q, k, v)
    o = _nbr_attn_call(q, k, v, cache_k[:, 0].reshape(B_LAT, PAST, D), cache_v[:, 0].reshape(B_LAT, PAST, D),
                       _pair_table_call(attn_rpb[0]), o)
    x, w_in, w_out, w_gu1, w_down1 = _post_call(
        x, o, mod[0], norm_g[0, 1].reshape(1, D), w_o, w_gu0, w_down0, final_g2, False,
        ((lru_w_in, 0), (lru_w_out, 0), (ffn_w_gu, 1), (ffn_w_down, 1)))

    gg, xr = _lru_in_call(x, mod[1], norm_g[1, 0].reshape(1, D), w_in)
    lru_p = (lru_conv_w[0], lru_conv_b[0].reshape(1, D), lru_w_a[0], lru_w_i[0], lru_b_a[0], lru_b_i[0], lru_lam[0])
    y, _ = _lru_scan_call(B_LAT, T_LAT, 0, gg, xr, jnp.swapaxes(state_h[:, 0], 0, 1), *lru_p)
    y, h_ctx = _lru_scan_call(B_CTX, T_CTX, N_LAT_ROWS // N_CTX_ROWS, gg, xr,
                              jnp.zeros((2, B_CTX, D), F32), *lru_p, y_prev=y)
    out_lat, out_ctx = _post_call(x, y, mod[1], norm_g[1, 1].reshape(1, D), w_out, w_gu1, w_down1, final_g2, True)

    y_sample = out_lat.reshape(B_LAT, T_LAT, D)
    y_prompt = out_ctx.reshape(B_CTX, T_CTX, D)
    new_k = k_ctx.reshape(B_CTX, 1, T_CTX, N_HEADS, HEAD_DIM)
    new_v = v_ctx.reshape(B_CTX, 1, T_CTX, N_HEADS, HEAD_DIM)
    new_h = jnp.swapaxes(h_ctx, 0, 1)[:, None]
    return y_prompt, y_sample, new_k, new_v, new_h
```

```python
import functools
import math

import jax
import jax.numpy as jnp
from jax import lax
from jax.experimental import pallas as pl
from jax.experimental.pallas import tpu as pltpu

F32 = jnp.float32
BF16 = jnp.bfloat16

D = 1024
B_CTX, T_CTX = 16, 256
B_LAT, T_LAT = 8, 1024
PAST = 512
GRID_W = 64
GRID_ROWS = T_LAT // GRID_W
N_HEADS = 16
HEAD_DIM = D // N_HEADS
ATTN_SCALE = HEAD_DIM ** -0.5
LOG2E = math.log2(math.e)
WIN_ROWS, WIN_COLS = 8, 16
N_DR = 2 * WIN_ROWS - 1
N_DC = 2 * WIN_COLS - 1
LRU_BLOCKS = 8
LRU_BW = D // LRU_BLOCKS
CONV_W = 4
LRU_C = 8.0
D_FF = 2816
N_MOD = 6
RMS_EPS = 1e-6
NEG_INF = -1e30
LANES = 128

N_LAT_ROWS = B_LAT * T_LAT
N_CTX_ROWS = B_CTX * T_CTX
N_ROWS = N_LAT_ROWS + N_CTX_ROWS
N_COND = 16

TM = 512
N_LAT_TILES = N_LAT_ROWS // TM
N_CTX_TILES = N_CTX_ROWS // TM
N_TILES = N_LAT_TILES + N_CTX_TILES
TILES_PER_LAT_SAMPLE = T_LAT // TM
MOD_TN = 1536
FF_CHUNK = 256
Q_BLK_ROWS = 4
N_PAIR_TYPES = 16
PAIR_TABLE_HEADS = 4
CAST_CHUNKS = 16
SCAN_CT = 32
VMEM_LIMIT = 56 * 1024 * 1024


def _sigmoid(x):
    return 0.5 * jnp.tanh(0.5 * x) + 0.5


def _modulate(x, g, shift, scale):
    ms = jnp.mean(x * x, axis=-1, keepdims=True)
    return (x * lax.rsqrt(ms + RMS_EPS)) * g * (1.0 + scale) + shift


def _cond_row_of_tile(i):
    return jnp.where(i < N_LAT_TILES, i // TILES_PER_LAT_SAMPLE, B_LAT)


def _const_spec(shape):
    nd = len(shape)
    return pl.BlockSpec(shape, lambda *_: (0,) * nd, pipeline_mode=pl.Buffered(1))


def _params(sem):
    return pltpu.CompilerParams(dimension_semantics=sem, vmem_limit_bytes=VMEM_LIMIT)


_ROW_SPEC = pl.BlockSpec((TM, D), lambda i: (i, 0))
_LAT_SPEC = pl.BlockSpec((TM, D), lambda i: (jnp.minimum(i, N_LAT_TILES - 1), 0))
_CTX_SPEC = pl.BlockSpec((TM, D), lambda i: (jnp.maximum(i - N_LAT_TILES, 0), 0))
_MOD_SPEC = pl.BlockSpec((None, N_MOD, 1, D), lambda i: (_cond_row_of_tile(i), 0, 0, 0))


def _cast_specs(w, layer):
    _, rows, cols = w.shape
    chunk = lambda i: jnp.minimum(i, CAST_CHUNKS - 1)
    return (pl.BlockSpec((None, rows // CAST_CHUNKS, cols), lambda i: (layer, chunk(i), 0)),
            pl.BlockSpec((rows // CAST_CHUNKS, cols), lambda i: (chunk(i), 0)),
            jax.ShapeDtypeStruct((rows, cols), BF16))


def _cast_chunks(in_refs, out_refs):
    @pl.when(pl.program_id(0) < CAST_CHUNKS)
    def _():
        for src, dst in zip(in_refs, out_refs):
            dst[...] = src[...].astype(BF16)


def _mod_kernel(cond_ref, w_ref, b_ref, o_ref):
    c = cond_ref[...]
    s = (c * _sigmoid(c)).astype(BF16)
    o_ref[...] = jnp.dot(s, w_ref[...].astype(BF16), preferred_element_type=F32) + b_ref[...]


def _mod_call(cond, w_mod, b_mod):
    depth = w_mod.shape[0]
    n = N_MOD * D
    return pl.pallas_call(
        _mod_kernel,
        grid=(depth, n // MOD_TN),
        in_specs=[
            pl.BlockSpec((N_COND, D), lambda l, j: (0, 0)),
            pl.BlockSpec((None, D, MOD_TN), lambda l, j: (l, 0, j)),
            pl.BlockSpec((None, 1, MOD_TN), lambda l, j: (l, 0, j)),
        ],
        out_specs=pl.BlockSpec((None, N_COND, MOD_TN), lambda l, j: (l, 0, j)),
        out_shape=jax.ShapeDtypeStruct((depth, N_COND, n), F32),
        compiler_params=_params(("arbitrary", "arbitrary")),
        name="adaln_mod",
    )(cond, w_mod, b_mod.reshape(depth, 1, n))


def _pair_table_kernel(r_ref, o_ref):
    shape = (GRID_W, 2 * GRID_W)
    ck = lax.broadcasted_iota(jnp.int32, shape, 0)
    lane = lax.broadcasted_iota(jnp.int32, shape, 1)
    cq = jnp.bitwise_and(lane, GRID_W - 1)
    cs = jnp.clip(cq - WIN_COLS // 2, 0, GRID_W - WIN_COLS)
    in_win = jnp.logical_and(ck >= cs, ck < cs + WIN_COLS)

    def toeplitz(h, d, lane0):
        base = jnp.broadcast_to(r_ref[h, d:d + 1, :], shape)
        return pltpu.roll(base, (lane0 - (WIN_COLS - 1)) % LANES, 1, stride=1, stride_axis=0)

    neg = jnp.full(shape, NEG_INF, F32)
    for h in range(PAIR_TABLE_HEADS):
        for t in range(N_PAIR_TYPES):
            d_left, d_right = (t + 1, t) if t < 14 else ((3, None) if t == 14 else (None, 10))
            left = neg if d_left is None else toeplitz(h, d_left, 0)
            right = neg if d_right is None else toeplitz(h, d_right, GRID_W)
            val = jnp.where(lane < GRID_W, left, right)
            o_ref[h, t] = jnp.where(in_win, val * LOG2E, NEG_INF)


def _pair_table_call(rpb):
    assert 2 * GRID_W == LANES
    r = jnp.pad(rpb[:, :, ::-1], ((0, 0), (0, 0), (0, LANES - N_DC)))
    return pl.pallas_call(
        _pair_table_kernel,
        grid=(N_HEADS // PAIR_TABLE_HEADS,),
        in_specs=[pl.BlockSpec((PAIR_TABLE_HEADS, N_DR, LANES), lambda h: (h, 0, 0))],
        out_specs=pl.BlockSpec((PAIR_TABLE_HEADS, N_PAIR_TYPES, GRID_W, LANES), lambda h: (h, 0, 0, 0)),
        out_shape=jax.ShapeDtypeStruct((N_HEADS, N_PAIR_TYPES, GRID_W, LANES), F32),
        compiler_params=_params(("arbitrary",)),
        name="rpb_table",
    )(r)


def _row_window_start(r):
    return min(max(r - WIN_ROWS // 2, 0), GRID_ROWS - WIN_ROWS)


def _pair_type(kr, r0):
    in0 = _row_window_start(r0) <= kr < _row_window_start(r0) + WIN_ROWS
    in1 = _row_window_start(r0 + 1) <= kr < _row_window_start(r0 + 1) + WIN_ROWS
    if in0 and in1:
        dl = kr - r0
        assert -6 <= dl <= 7
        return dl + 6
    if in0:
        assert kr - r0 == -(WIN_ROWS // 2)
        return 14
    if in1:
        assert kr - (r0 + 1) == WIN_ROWS // 2 - 1
        return 15
    return None


def _qkv_kernel(n_cast, xl_ref, xc_ref, m_ref, g_ref, w_ref, *refs):
    cast_in = refs[:n_cast]
    q_ref, k_ref, v_ref, kf_ref, vf_ref, x_ref = refs[n_cast:n_cast + 6]
    cast_out = refs[n_cast + 6:]
    _cast_chunks(cast_in, cast_out)
    x = jnp.where(pl.program_id(0) < N_LAT_TILES, xl_ref[...], xc_ref[...])
    x_ref[...] = x
    xm = _modulate(x, g_ref[...], m_ref[0], m_ref[1])
    qkv = jnp.dot(xm.astype(BF16), w_ref[...], preferred_element_type=F32)
    q_ref[...] = (qkv[:, :D] * (ATTN_SCALE * LOG2E)).astype(BF16)
    k = qkv[:, D:2 * D]
    v = qkv[:, 2 * D:]
    k_ref[...] = k.astype(BF16)
    v_ref[...] = v.astype(BF16)

    @pl.when(pl.program_id(0) >= N_LAT_TILES)
    def _():
        kf_ref[...] = k.reshape(TM, N_HEADS, HEAD_DIM)
        vf_ref[...] = v.reshape(TM, N_HEADS, HEAD_DIM)


def _qkv_call(x_lat, x_ctx, mod, g, w_qkv, cast_weights):
    cast_in, cast_out, cast_shapes = zip(*[_cast_specs(w, l) for w, l in cast_weights])
    new_cache = pl.BlockSpec((TM, N_HEADS, HEAD_DIM), lambda i: (jnp.maximum(i - N_LAT_TILES, 0), 0, 0))
    return pl.pallas_call(
        functools.partial(_qkv_kernel, len(cast_weights)),
        grid=(N_TILES,),
        in_specs=[_LAT_SPEC, _CTX_SPEC, _MOD_SPEC, _const_spec((1, D)), _const_spec((D, 3 * D)), *cast_in],
        out_specs=[_ROW_SPEC, _ROW_SPEC, _ROW_SPEC, new_cache, new_cache, _ROW_SPEC, *cast_out],
        out_shape=[jax.ShapeDtypeStruct((N_ROWS, D), BF16)] * 3
        + [jax.ShapeDtypeStruct((N_CTX_ROWS, N_HEADS, HEAD_DIM), F32)] * 2
        + [jax.ShapeDtypeStruct((N_ROWS, D), F32)] + list(cast_shapes),
        compiler_params=_params(("arbitrary",)),
        name="qkv_proj",
    )(x_lat, x_ctx, mod, g, w_qkv, *[w for w, _ in cast_weights])


ATTN_SLOTS = 3


def _kq(k, q):
    return lax.dot_general(k, q, (((1,), (1,)), ((), ())), preferred_element_type=F32)


def _attn_pipeline(n_units, scores_fn, probs_fn, out_fn):
    assert ATTN_SLOTS == 3
    col_max = {n: scores_fn(n) for n in range(min(2, n_units))}
    col_sum = {}
    outs = []
    for n in range(n_units):
        if n + 2 < n_units:
            col_max[n + 2] = scores_fn(n + 2)
        col_sum[n] = probs_fn(n, col_max[n])
        if n >= 1:
            outs.append(out_fn(n - 1, col_sum[n - 1]))
    outs.append(out_fn(n_units - 1, col_sum[n_units - 1]))
    return outs


def _store_scores(s_ref, slot, row0, s):
    s_ref[slot, row0:row0 + s.shape[0], :] = s
    return s.max(axis=0, keepdims=True)


def _store_probs(s_ref, p_ref, slot, row0, rows, m):
    p = jnp.exp2(s_ref[slot, row0:row0 + rows, :] - m)
    p_ref[slot, row0:row0 + rows, :] = p.astype(BF16)
    return p.sum(axis=0, keepdims=True)


def _head_masks(n_q):
    lane = lax.broadcasted_iota(jnp.int32, (n_q, 2 * HEAD_DIM), 1)
    return lane < HEAD_DIM, lane >= HEAD_DIM


def _ctx_attn_kernel(q_ref, k_ref, v_ref, o_ref, s_ref, p_ref):
    v_t = v_ref[...].T
    masks = _head_masks(T_CTX)

    def scores(h):
        cols = slice((h // 2) * 2 * HEAD_DIM, (h // 2 + 1) * 2 * HEAD_DIM)
        q = q_ref[:, cols]
        qm = jnp.where(masks[h % 2], q, jnp.zeros_like(q))
        return _store_scores(s_ref, h % ATTN_SLOTS, 0, _kq(k_ref[:, cols], qm))

    def probs(h, m):
        return _store_probs(s_ref, p_ref, h % ATTN_SLOTS, 0, T_CTX, m)

    def out(h, l):
        o = jnp.dot(v_t[h * HEAD_DIM:(h + 1) * HEAD_DIM, :], p_ref[h % ATTN_SLOTS], preferred_element_type=F32)
        return o / l

    outs = _attn_pipeline(N_HEADS, scores, probs, out)
    o_ref[...] = jnp.concatenate(outs, axis=0).T.astype(BF16)


def _ctx_attn_call(q, k, v):
    blk = pl.BlockSpec((T_CTX, D), lambda b: (N_LAT_ROWS // T_CTX + b, 0))
    return pl.pallas_call(
        _ctx_attn_kernel,
        grid=(B_CTX,),
        in_specs=[blk, blk, blk],
        out_specs=blk,
        out_shape=jax.ShapeDtypeStruct((N_ROWS, D), BF16),
        scratch_shapes=[pltpu.VMEM((ATTN_SLOTS, T_CTX, T_CTX), F32), pltpu.VMEM((ATTN_SLOTS, T_CTX, T_CTX), BF16)],
        compiler_params=_params(("arbitrary",)),
        name="ctx_attn",
    )(q, k, v)


def _key_rows_of_block(i):
    r_first = Q_BLK_ROWS * i
    return _row_window_start(r_first), _row_window_start(r_first + Q_BLK_ROWS - 1) + WIN_ROWS


N_Q_BLOCKS = GRID_ROWS // Q_BLK_ROWS
MAX_LAT_KEYS = max(kr1 - kr0 for kr0, kr1 in map(_key_rows_of_block, range(N_Q_BLOCKS))) * GRID_W


def _nbr_attn_kernel(q_ref, k_ref, v_ref, ck_ref, cv_ref, pt_ref, o_in_ref, o_ref, s_ref, p_ref, ck_scr, cvt_scr):
    del o_in_ref
    hp = pl.program_id(1)

    @pl.when(hp == 0)
    def _():
        ck_scr[...] = ck_ref[...].reshape(PAST, D).astype(BF16)
        cvt_scr[...] = cv_ref[...].reshape(PAST, D).astype(BF16).T

    pair_cols = pl.ds(pl.multiple_of(hp * 2 * HEAD_DIM, 2 * HEAD_DIM), 2 * HEAD_DIM)
    v_t = v_ref[...].T
    ck = ck_scr[:, pair_cols]
    cv_t = cvt_scr[pair_cols, :]
    n_q = Q_BLK_ROWS * GRID_W
    masks = _head_masks(n_q)
    neg_blk = jnp.full((GRID_W, 2 * GRID_W), NEG_INF, F32)

    def unit(n):
        return (n // 2, n % 2, n % ATTN_SLOTS) + _key_rows_of_block(n // 2)

    def scores(n):
        i, hh, slot, kr0, kr1 = unit(n)
        r_first = Q_BLK_ROWS * i
        q = q_ref[r_first * GRID_W:(r_first + Q_BLK_ROWS) * GRID_W, :]
        qm = jnp.where(masks[hh], q, jnp.zeros_like(q))
        bias_rows = []
        for kr in range(kr0, kr1):
            blks = []
            for r0 in range(r_first, r_first + Q_BLK_ROWS, 2):
                typ = _pair_type(kr, r0)
                blks.append(neg_blk if typ is None else pt_ref[hh, typ])
            bias_rows.append(jnp.concatenate(blks, axis=1))
        s_lat = _kq(k_ref[kr0 * GRID_W:kr1 * GRID_W, :], qm) + jnp.concatenate(bias_rows, axis=0)
        m_lat = _store_scores(s_ref, slot, 0, s_lat)
        m_ctx = _store_scores(s_ref, slot, MAX_LAT_KEYS, _kq(ck, qm))
        return jnp.maximum(m_lat, m_ctx)

    def probs(n, m):
        _, _, slot, kr0, kr1 = unit(n)
        return (_store_probs(s_ref, p_ref, slot, 0, (kr1 - kr0) * GRID_W, m)
                + _store_probs(s_ref, p_ref, slot, MAX_LAT_KEYS, PAST, m))

    def out(n, l):
        _, hh, slot, kr0, kr1 = unit(n)
        hs = slice(hh * HEAD_DIM, (hh + 1) * HEAD_DIM)
        lk = (kr1 - kr0) * GRID_W
        o = (jnp.dot(v_t[hs, kr0 * GRID_W:kr1 * GRID_W], p_ref[slot, 0:lk, :], preferred_element_type=F32)
             + jnp.dot(cv_t[hs, :], p_ref[slot, MAX_LAT_KEYS:MAX_LAT_KEYS + PAST, :], preferred_element_type=F32))
        return o / l

    outs = _attn_pipeline(2 * N_Q_BLOCKS, scores, probs, out)
    blocks = [jnp.concatenate(outs[2 * i:2 * i + 2], axis=0) for i in range(N_Q_BLOCKS)]
    o_ref[...] = jnp.concatenate(blocks, axis=1).T.astype(BF16)


def _nbr_attn_call(q, k, v, cache_k, cache_v, pair_table, o_ctx):
    hw = 2 * HEAD_DIM
    blk = pl.BlockSpec((T_LAT, hw), lambda b, hp: (b, hp))
    cache = pl.BlockSpec((None, None, PAST, N_HEADS, HEAD_DIM), lambda b, hp: (b, 0, 0, 0, 0))
    return pl.pallas_call(
        _nbr_attn_kernel,
        grid=(B_LAT, N_HEADS // 2),
        in_specs=[blk, blk, blk, cache, cache,
                  pl.BlockSpec((2, N_PAIR_TYPES, GRID_W, 2 * GRID_W), lambda b, hp: (hp, 0, 0, 0)),
                  pl.BlockSpec(memory_space=pl.ANY)],
        out_specs=blk,
        out_shape=jax.ShapeDtypeStruct((N_ROWS, D), BF16),
        input_output_aliases={6: 0},
        scratch_shapes=[pltpu.VMEM((ATTN_SLOTS, MAX_LAT_KEYS + PAST, Q_BLK_ROWS * GRID_W), F32),
                        pltpu.VMEM((ATTN_SLOTS, MAX_LAT_KEYS + PAST, Q_BLK_ROWS * GRID_W), BF16),
                        pltpu.VMEM((PAST, D), BF16), pltpu.VMEM((D, PAST), BF16)],
        compiler_params=_params(("arbitrary", "arbitrary")),
        name="nbr_attn",
    )(q, k, v, cache_k, cache_v, pair_table, o_ctx)


def _post_kernel(final_norm, n_cast, x_ref, o_ref, m_ref, g_ref, wo_ref, wgu_ref, wd_ref, fg_ref, *refs):
    n_out = 2 if final_norm else 1
    cast_in = refs[:n_cast]
    out_refs = refs[n_cast:n_cast + n_out]
    cast_out = refs[n_cast + n_out:2 * n_cast + n_out]
    acc_ref, x1_ref = refs[2 * n_cast + n_out:]

    _cast_chunks(cast_in, cast_out)
    mix = jnp.dot(o_ref[...], wo_ref[...], preferred_element_type=F32)
    x1 = x_ref[...] + m_ref[2] * mix
    x1_ref[...] = x1
    h = _modulate(x1, g_ref[...], m_ref[3], m_ref[4]).astype(BF16)
    for c in range(D_FF // FF_CHUNK):
        cs = slice(c * FF_CHUNK, (c + 1) * FF_CHUNK)
        us = slice(D_FF + c * FF_CHUNK, D_FF + (c + 1) * FF_CHUNK)
        gate = jnp.dot(h, wgu_ref[:, cs], preferred_element_type=F32)
        up = jnp.dot(h, wgu_ref[:, us], preferred_element_type=F32)
        a = (gate * _sigmoid(gate) * up).astype(BF16)
        part = jnp.dot(a, wd_ref[cs, :], preferred_element_type=F32)
        if c == 0:
            acc_ref[...] = part
        else:
            acc_ref[...] += part
    def finish(out_ref):
        x2 = x1_ref[...] + m_ref[5] * acc_ref[...]
        if final_norm:
            ms = jnp.mean(x2 * x2, axis=-1, keepdims=True)
            x2 = x2 * lax.rsqrt(ms + RMS_EPS) * fg_ref[...]
        out_ref[...] = x2

    if not final_norm:
        finish(out_refs[0])
        return
    is_lat = pl.program_id(0) < N_LAT_TILES
    pl.when(is_lat)(lambda: finish(out_refs[0]))
    pl.when(jnp.logical_not(is_lat))(lambda: finish(out_refs[1]))


def _post_call(x, o, mod, g, w_o, w_gu, w_down, final_g, final_norm, cast_weights=()):
    cast_in, cast_out, cast_shapes = (zip(*[_cast_specs(w, l) for w, l in cast_weights])
                                      if cast_weights else ((), (), ()))
    if final_norm:
        out_specs = [_LAT_SPEC, _CTX_SPEC]
        out_shape = [jax.ShapeDtypeStruct((N_LAT_ROWS, D), F32), jax.ShapeDtypeStruct((N_CTX_ROWS, D), F32)]
    else:
        out_specs = [_ROW_SPEC]
        out_shape = [jax.ShapeDtypeStruct((N_ROWS, D), F32)]
    return pl.pallas_call(
        functools.partial(_post_kernel, final_norm, len(cast_weights)),
        grid=(N_TILES,),
        in_specs=[
            _ROW_SPEC, _ROW_SPEC, _MOD_SPEC,
            _const_spec((1, D)),
            _const_spec((D, D)),
            _const_spec((D, 2 * D_FF)),
            _const_spec((D_FF, D)),
            _const_spec((1, D)),
            *cast_in,
        ],
        out_specs=[*out_specs, *cast_out],
        out_shape=[*out_shape, *cast_shapes],
        scratch_shapes=[pltpu.VMEM((TM, D), F32), pltpu.VMEM((TM, D), F32)],
        compiler_params=_params(("arbitrary",)),
        name="mix_out_ffn",
    )(x, o, mod, g, w_o, w_gu, w_down, final_g, *[w for w, _ in cast_weights])


def _lru_in_kernel(x_ref, m_ref, g_ref, w_ref, gate_ref, xr_ref):
    xm = _modulate(x_ref[...], g_ref[...], m_ref[0], m_ref[1])
    y = jnp.dot(xm.astype(BF16), w_ref[...], preferred_element_type=F32)
    gate_ref[...] = jax.nn.gelu(y[:, :D]).astype(BF16)
    xr_ref[...] = y[:, D:]


def _lru_in_call(x, mod, g, w_in):
    return pl.pallas_call(
        _lru_in_kernel,
        grid=(N_TILES,),
        in_specs=[_ROW_SPEC, _MOD_SPEC, _const_spec((1, D)), _const_spec((D, 2 * D))],
        out_specs=[_ROW_SPEC, _ROW_SPEC],
        out_shape=[jax.ShapeDtypeStruct((N_ROWS, D), BF16), jax.ShapeDtypeStruct((N_ROWS, D), F32)],
        compiler_params=_params(("arbitrary",)),
        name="lru_in_proj",
    )(x, mod, g, w_in)


def _lru_scan_kernel(nb, nt, aliased, xr_ref, gg_ref, cw_ref, cb_ref, wa_ref, wi_ref, ba_ref, bi_ref, lam_ref,
                     h0_ref, *refs):
    y_ref, ht_ref, xt_ref, hf_ref, ab_ref, bb_ref, af0, bf0, af1, bf1 = refs[1:] if aliased else refs
    fwd_slots = ((af0, bf0), (af1, bf1))
    left = CONV_W // 2
    pad_rows = (CONV_W - 1) * nb
    xt_ref[pl.ds(0, left * nb), :] = jnp.zeros((left * nb, LRU_BW), F32)
    xt_ref[pl.ds((left + nt) * nb, pad_rows - left * nb), :] = jnp.zeros((pad_rows - left * nb, LRU_BW), F32)
    for s in range(nb):
        xt_ref[pl.ds(left * nb + s, nt, stride=nb), :] = xr_ref[pl.ds(s * nt, nt), :]

    cw = cw_ref[...]
    cb = cb_ref[...]
    w4 = (jnp.concatenate([wa_ref[0], wi_ref[0], wa_ref[1], wi_ref[1]], axis=1) * 0.5).astype(BF16)
    half_ba = 0.5 * ba_ref[...]
    half_bi = 0.5 * bi_ref[...]
    neg_lam = -lam_ref[...]
    sp = jnp.maximum(neg_lam, 0.0) + jnp.log1p(jnp.exp(-jnp.abs(neg_lam)))
    c1 = (-0.5 * LRU_C * LOG2E) * sp
    crows = SCAN_CT * nb
    n_chunks = nt // SCAN_CT
    assert n_chunks % 2 == 0

    def chunk_rows(c):
        return pl.ds(pl.multiple_of(c * crows, 8), crows)

    def coeffs(c, slot):
        t0 = c * SCAN_CT
        xc = cb
        for j in range(CONV_W):
            xc = xc + xt_ref[pl.ds(pl.multiple_of((t0 + j) * nb, 8), crows), :] * cw[j:j + 1]
        half_xc = 0.5 * xc
        pre = jnp.dot(xc.astype(BF16), w4, preferred_element_type=F32)

        def direction(d):
            o = 2 * d * LRU_BW
            t_r = jnp.tanh(pre[:, o:o + LRU_BW] + half_ba[d:d + 1])
            a = jnp.exp2(c1[d:d + 1] * t_r + c1[d:d + 1])
            t_i = jnp.tanh(pre[:, o + LRU_BW:o + 2 * LRU_BW] + half_bi[d:d + 1])
            y = 1.0 - a * a
            root = jnp.where(y > 0.0, y * lax.rsqrt(y), 0.0)
            return a, root * (half_xc * t_i + half_xc)

        a_ref, b_ref = fwd_slots[slot]
        a_ref[...], b_ref[...] = direction(0)
        ab_ref[chunk_rows(c), :], bb_ref[chunk_rows(c), :] = direction(1)

    def scan_fwd(c, slot, h):
        a_ref, b_ref = fwd_slots[slot]
        for s in range(SCAN_CT):
            rows = slice(s * nb, (s + 1) * nb)
            h = a_ref[rows, :] * h + b_ref[rows, :]
            hf_ref[pl.ds(pl.multiple_of(c * crows + s * nb, 8), nb), :] = h
        return h

    def scan_bwd(c, h):
        for s in range(SCAN_CT - 1, 0, -2):
            rows1 = pl.ds(pl.multiple_of(c * crows + s * nb, 8), nb)
            rows0 = pl.ds(pl.multiple_of(c * crows + (s - 1) * nb, 8), nb)
            a1, b1 = ab_ref[rows1, :], bb_ref[rows1, :]
            a0, b0 = ab_ref[rows0, :], bb_ref[rows0, :]
            bb_ref[rows1, :] = a1 * h + b1
            h = (a0 * a1) * h + (a0 * b1 + b0)
            bb_ref[rows0, :] = h
        return h

    def emit(c):
        for s in range(nb):
            src = pl.ds(c * crows + s, SCAN_CT, stride=nb)
            dst = pl.ds(pl.multiple_of(s * nt + c * SCAN_CT, SCAN_CT), SCAN_CT)
            hsum = hf_ref[src, :] + bb_ref[src, :]
            y_ref[dst, :] = (hsum * gg_ref[dst, :].astype(F32)).astype(BF16)

    coeffs(0, 0)

    def pass1(k, h):
        c = 2 * k
        h = scan_fwd(c, 0, h)
        coeffs(c + 1, 1)
        h = scan_fwd(c + 1, 1, h)
        coeffs(jnp.minimum(c + 2, n_chunks - 1), 0)
        return h

    ht_ref[0] = lax.fori_loop(0, n_chunks // 2, pass1, h0_ref[0])

    hb = scan_bwd(n_chunks - 1, h0_ref[1])

    def pass2(k, h):
        c = n_chunks - 2 - k
        emit(c + 1)
        return scan_bwd(c, h)

    ht_ref[1] = lax.fori_loop(0, n_chunks - 1, pass2, hb)
    emit(0)


def _lru_scan_call(nb, nt, row_block0, gg, xr, h0, conv_w, conv_b, w_a, w_i, b_a, b_i, lam, y_prev=None):
    rows = nb * nt
    col = pl.BlockSpec((rows, LRU_BW), lambda n: (row_block0, n))
    vec2 = pl.BlockSpec((2, LRU_BW), lambda n: (0, n))
    wblk = pl.BlockSpec((2, None, LRU_BW, LRU_BW), lambda n: (0, n, 0, 0))
    state = pl.BlockSpec((2, nb, LRU_BW), lambda n: (0, 0, n))
    tm_rows = (nt + CONV_W - 1) * nb
    crows = SCAN_CT * nb
    aliased = y_prev is not None
    return pl.pallas_call(
        functools.partial(_lru_scan_kernel, nb, nt, aliased),
        grid=(LRU_BLOCKS,),
        in_specs=[
            col, col,
            pl.BlockSpec((CONV_W, LRU_BW), lambda n: (0, n)),
            pl.BlockSpec((1, LRU_BW), lambda n: (0, n)),
            wblk, wblk, vec2, vec2, vec2, state,
        ] + ([pl.BlockSpec(memory_space=pl.ANY)] if aliased else []),
        out_specs=[col, state],
        out_shape=[jax.ShapeDtypeStruct((N_ROWS, D), BF16), jax.ShapeDtypeStruct((2, nb, D), F32)],
        input_output_aliases={10: 0} if aliased else {},
        scratch_shapes=[pltpu.VMEM((tm_rows, LRU_BW), F32)] + [pltpu.VMEM((rows, LRU_BW), F32)] * 3
        + [pltpu.VMEM((crows, LRU_BW), F32)] * 4,
        compiler_params=_params(("arbitrary",)),
        name=f"lru_scan_{nb}x{nt}",
    )(xr, gg, conv_w, conv_b, w_a, w_i, b_a, b_i, lam, h0, *([y_prev] if aliased else []))


def kernel(x_prompt, x_sample, c, cache_k, cache_v, state_h, c_ctx, norm_g, w_mod, b_mod, attn_w_qkv, attn_w_o,
           attn_rpb, lru_w_in, lru_conv_w, lru_conv_b, lru_w_a, lru_b_a, lru_w_i, lru_b_i, lru_lam, lru_w_out,
           ffn_w_gu, ffn_w_down, final_g):
    x_lat = x_sample.reshape(N_LAT_ROWS, D)
    x_ctx = x_prompt.reshape(N_CTX_ROWS, D)
    cond = jnp.concatenate([c, c_ctx[None, :], jnp.zeros((N_COND - B_LAT - 1, D), F32)], axis=0)
    mod = _mod_call(cond, w_mod, b_mod).reshape(2, N_COND, N_MOD, 1, D)
    final_g2 = final_g.reshape(1, D)

    q, k, v, k_ctx, v_ctx, x, w_o, w_gu0, w_down0 = _qkv_call(
        x_lat, x_ctx, mod[0], norm_g[0, 0].reshape(1, D), attn_w_qkv[0].astype(BF16),
        ((attn_w_o, 0), (ffn_w_gu, 0), (ffn_w_down, 0)))
    o = _ctx_attn_call(q, k, v)
    o = _nbr_attn_call(q, k, v, cache_k, cache_v, _pair_table_call(attn_rpb[0]), o)
    x, w_in, w_out, w_gu1, w_down1 = _post_call(
        x, o, mod[0], norm_g[0, 1].reshape(1, D), w_o, w_gu0, w_down0, final_g2, False,
        ((lru_w_in, 0), (lru_w_out, 0), (ffn_w_gu, 1), (ffn_w_down, 1)))

    gg, xr = _lru_in_call(x, mod[1], norm_g[1, 0].reshape(1, D), w_in)
    lru_p = (lru_conv_w[0], lru_conv_b[0].reshape(1, D), lru_w_a[0], lru_w_i[0], lru_b_a[0], lru_b_i[0], lru_lam[0])
    y, _ = _lru_scan_call(B_LAT, T_LAT, 0, gg, xr, jnp.swapaxes(state_h[:, 0], 0, 1), *lru_p)
    y, h_ctx = _lru_scan_call(B_CTX, T_CTX, N_LAT_ROWS // N_CTX_ROWS, gg, xr,
                              jnp.zeros((2, B_CTX, D), F32), *lru_p, y_prev=y)
    out_lat, out_ctx = _post_call(x, y, mod[1], norm_g[1, 1].reshape(1, D), w_out, w_gu1, w_down1, final_g2, True)

    y_sample = out_lat.reshape(B_LAT, T_LAT, D)
    y_prompt = out_ctx.reshape(B_CTX, T_CTX, D)
    new_k = k_ctx.reshape(B_CTX, 1, T_CTX, N_HEADS, HEAD_DIM)
    new_v = v_ctx.reshape(B_CTX, 1, T_CTX, N_HEADS, HEAD_DIM)
    new_h = jnp.swapaxes(h_ctx, 0, 1)[:, None]
    return y_prompt, y_sample, new_k, new_v, new_h
```

```python
import functools
import math

import jax
import jax.numpy as jnp
from jax import lax
from jax.experimental import pallas as pl
from jax.experimental.pallas import tpu as pltpu

F32 = jnp.float32
BF16 = jnp.bfloat16

D = 1024
B_CTX, T_CTX = 16, 256
B_LAT, T_LAT = 8, 1024
PAST = 512
GRID_W = 64
GRID_ROWS = T_LAT // GRID_W
N_HEADS = 16
HEAD_DIM = D // N_HEADS
ATTN_SCALE = HEAD_DIM ** -0.5
LOG2E = math.log2(math.e)
WIN_ROWS, WIN_COLS = 8, 16
N_DR = 2 * WIN_ROWS - 1
N_DC = 2 * WIN_COLS - 1
LRU_BLOCKS = 8
LRU_BW = D // LRU_BLOCKS
CONV_W = 4
LRU_C = 8.0
D_FF = 2816
N_MOD = 6
RMS_EPS = 1e-6
NEG_INF = -1e30
LANES = 128

N_LAT_ROWS = B_LAT * T_LAT
N_CTX_ROWS = B_CTX * T_CTX
N_ROWS = N_LAT_ROWS + N_CTX_ROWS
N_COND = 16

TM = 512
N_LAT_TILES = N_LAT_ROWS // TM
N_CTX_TILES = N_CTX_ROWS // TM
N_TILES = N_LAT_TILES + N_CTX_TILES
TILES_PER_LAT_SAMPLE = T_LAT // TM
MOD_TN = 1536
FF_CHUNK = 256
Q_BLK_ROWS = 4
N_PAIR_TYPES = 16
PAIR_TABLE_HEADS = 4
CAST_CHUNKS = 16
SCAN_CT = 32
VMEM_LIMIT = 56 * 1024 * 1024


def _sigmoid(x):
    return 0.5 * jnp.tanh(0.5 * x) + 0.5


def _modulate(x, g, shift, scale):
    ms = jnp.mean(x * x, axis=-1, keepdims=True)
    return (x * lax.rsqrt(ms + RMS_EPS)) * g * (1.0 + scale) + shift


def _cond_row_of_tile(i):
    return jnp.where(i < N_LAT_TILES, i // TILES_PER_LAT_SAMPLE, B_LAT)


def _const_spec(shape):
    nd = len(shape)
    return pl.BlockSpec(shape, lambda *_: (0,) * nd, pipeline_mode=pl.Buffered(1))


def _params(sem):
    return pltpu.CompilerParams(dimension_semantics=sem, vmem_limit_bytes=VMEM_LIMIT)


_ROW_SPEC = pl.BlockSpec((TM, D), lambda i: (i, 0))
_LAT_SPEC = pl.BlockSpec((TM, D), lambda i: (jnp.minimum(i, N_LAT_TILES - 1), 0))
_CTX_SPEC = pl.BlockSpec((TM, D), lambda i: (jnp.maximum(i - N_LAT_TILES, 0), 0))
_MOD_SPEC = pl.BlockSpec((None, N_MOD, 1, D), lambda i: (_cond_row_of_tile(i), 0, 0, 0))


def _cast_specs(w, layer):
    _, rows, cols = w.shape
    chunk = lambda i: jnp.minimum(i, CAST_CHUNKS - 1)
    return (pl.BlockSpec((None, rows // CAST_CHUNKS, cols), lambda i: (layer, chunk(i), 0)),
            pl.BlockSpec((rows // CAST_CHUNKS, cols), lambda i: (chunk(i), 0)),
            jax.ShapeDtypeStruct((rows, cols), BF16))


def _cast_chunks(in_refs, out_refs):
    @pl.when(pl.program_id(0) < CAST_CHUNKS)
    def _():
        for src, dst in zip(in_refs, out_refs):
            dst[...] = src[...].astype(BF16)


def _mod_kernel(cond_ref, w_ref, b_ref, o_ref):
    c = cond_ref[...]
    s = (c * _sigmoid(c)).astype(BF16)
    o_ref[...] = jnp.dot(s, w_ref[...].astype(BF16), preferred_element_type=F32) + b_ref[...]


def _mod_call(cond, w_mod, b_mod):
    depth = w_mod.shape[0]
    n = N_MOD * D
    return pl.pallas_call(
        _mod_kernel,
        grid=(depth, n // MOD_TN),
        in_specs=[
            pl.BlockSpec((N_COND, D), lambda l, j: (0, 0)),
            pl.BlockSpec((None, D, MOD_TN), lambda l, j: (l, 0, j)),
            pl.BlockSpec((None, 1, MOD_TN), lambda l, j: (l, 0, j)),
        ],
        out_specs=pl.BlockSpec((None, N_COND, MOD_TN), lambda l, j: (l, 0, j)),
        out_shape=jax.ShapeDtypeStruct((depth, N_COND, n), F32),
        compiler_params=_params(("arbitrary", "arbitrary")),
        name="adaln_mod",
    )(cond, w_mod, b_mod.reshape(depth, 1, n))


def _pair_table_kernel(r_ref, o_ref):
    shape = (GRID_W, 2 * GRID_W)
    ck = lax.broadcasted_iota(jnp.int32, shape, 0)
    lane = lax.broadcasted_iota(jnp.int32, shape, 1)
    cq = jnp.bitwise_and(lane, GRID_W - 1)
    cs = jnp.clip(cq - WIN_COLS // 2, 0, GRID_W - WIN_COLS)
    in_win = jnp.logical_and(ck >= cs, ck < cs + WIN_COLS)

    def toeplitz(h, d, lane0):
        base = jnp.broadcast_to(r_ref[h, d:d + 1, :], shape)
        return pltpu.roll(base, (lane0 - (WIN_COLS - 1)) % LANES, 1, stride=1, stride_axis=0)

    neg = jnp.full(shape, NEG_INF, F32)
    for h in range(PAIR_TABLE_HEADS):
        for t in range(N_PAIR_TYPES):
            d_left, d_right = (t + 1, t) if t < 14 else ((3, None) if t == 14 else (None, 10))
            left = neg if d_left is None else toeplitz(h, d_left, 0)
            right = neg if d_right is None else toeplitz(h, d_right, GRID_W)
            val = jnp.where(lane < GRID_W, left, right)
            o_ref[h, t] = jnp.where(in_win, val * LOG2E, NEG_INF)


def _pair_table_call(rpb):
    assert 2 * GRID_W == LANES
    r = jnp.pad(rpb[:, :, ::-1], ((0, 0), (0, 0), (0, LANES - N_DC)))
    return pl.pallas_call(
        _pair_table_kernel,
        grid=(N_HEADS // PAIR_TABLE_HEADS,),
        in_specs=[pl.BlockSpec((PAIR_TABLE_HEADS, N_DR, LANES), lambda h: (h, 0, 0))],
        out_specs=pl.BlockSpec((PAIR_TABLE_HEADS, N_PAIR_TYPES, GRID_W, LANES), lambda h: (h, 0, 0, 0)),
        out_shape=jax.ShapeDtypeStruct((N_HEADS, N_PAIR_TYPES, GRID_W, LANES), F32),
        compiler_params=_params(("arbitrary",)),
        name="rpb_table",
    )(r)


def _row_window_start(r):
    return min(max(r - WIN_ROWS // 2, 0), GRID_ROWS - WIN_ROWS)


def _pair_type(kr, r0):
    in0 = _row_window_start(r0) <= kr < _row_window_start(r0) + WIN_ROWS
    in1 = _row_window_start(r0 + 1) <= kr < _row_window_start(r0 + 1) + WIN_ROWS
    if in0 and in1:
        dl = kr - r0
        assert -6 <= dl <= 7
        return dl + 6
    if in0:
        assert kr - r0 == -(WIN_ROWS // 2)
        return 14
    if in1:
        assert kr - (r0 + 1) == WIN_ROWS // 2 - 1
        return 15
    return None


def _qkv_kernel(n_cast, xl_ref, xc_ref, m_ref, g_ref, w_ref, *refs):
    cast_in = refs[:n_cast]
    q_ref, k_ref, v_ref, kf_ref, vf_ref, x_ref = refs[n_cast:n_cast + 6]
    cast_out = refs[n_cast + 6:]
    _cast_chunks(cast_in, cast_out)
    x = jnp.where(pl.program_id(0) < N_LAT_TILES, xl_ref[...], xc_ref[...])
    x_ref[...] = x
    xm = _modulate(x, g_ref[...], m_ref[0], m_ref[1])
    qkv = jnp.dot(xm.astype(BF16), w_ref[...], preferred_element_type=F32)
    q_ref[...] = (qkv[:, :D] * (ATTN_SCALE * LOG2E)).astype(BF16)
    k = qkv[:, D:2 * D]
    v = qkv[:, 2 * D:]
    k_ref[...] = k.astype(BF16)
    v_ref[...] = v.astype(BF16)

    @pl.when(pl.program_id(0) >= N_LAT_TILES)
    def _():
        for src, dst in ((k, kf_ref), (v, vf_ref)):
            t = src.T.reshape(N_HEADS, HEAD_DIM, TM)
            for s in range(TM // T_CTX):
                dst[s] = t[:, :, s * T_CTX:(s + 1) * T_CTX]


def _qkv_call(x_lat, x_ctx, mod, g, w_qkv, cast_weights):
    cast_in, cast_out, cast_shapes = zip(*[_cast_specs(w, l) for w, l in cast_weights])
    new_cache = pl.BlockSpec((TM // T_CTX, N_HEADS, HEAD_DIM, T_CTX),
                             lambda i: (jnp.maximum(i - N_LAT_TILES, 0), 0, 0, 0))
    return pl.pallas_call(
        functools.partial(_qkv_kernel, len(cast_weights)),
        grid=(N_TILES,),
        in_specs=[_LAT_SPEC, _CTX_SPEC, _MOD_SPEC, _const_spec((1, D)), _const_spec((D, 3 * D)), *cast_in],
        out_specs=[_ROW_SPEC, _ROW_SPEC, _ROW_SPEC, new_cache, new_cache, _ROW_SPEC, *cast_out],
        out_shape=[jax.ShapeDtypeStruct((N_ROWS, D), BF16)] * 3
        + [jax.ShapeDtypeStruct((B_CTX, N_HEADS, HEAD_DIM, T_CTX), F32)] * 2
        + [jax.ShapeDtypeStruct((N_ROWS, D), F32)] + list(cast_shapes),
        compiler_params=_params(("arbitrary",)),
        name="qkv_proj",
    )(x_lat, x_ctx, mod, g, w_qkv, *[w for w, _ in cast_weights])


ATTN_SLOTS = 3


def _kq(k, q):
    return lax.dot_general(k, q, (((1,), (1,)), ((), ())), preferred_element_type=F32)


def _attn_pipeline(n_units, scores_fn, probs_fn, out_fn):
    assert ATTN_SLOTS == 3
    col_max = {n: scores_fn(n) for n in range(min(2, n_units))}
    col_sum = {}
    outs = []
    for n in range(n_units):
        if n + 2 < n_units:
            col_max[n + 2] = scores_fn(n + 2)
        col_sum[n] = probs_fn(n, col_max[n])
        if n >= 1:
            outs.append(out_fn(n - 1, col_sum[n - 1]))
    outs.append(out_fn(n_units - 1, col_sum[n_units - 1]))
    return outs


def _store_scores(s_ref, slot, row0, s):
    s_ref[slot, row0:row0 + s.shape[0], :] = s
    return s.max(axis=0, keepdims=True)


def _store_probs(s_ref, p_ref, slot, row0, rows, m):
    p = jnp.exp2(s_ref[slot, row0:row0 + rows, :] - m)
    p_ref[slot, row0:row0 + rows, :] = p.astype(BF16)
    return p.sum(axis=0, keepdims=True)


def _head_masks(n_q):
    lane = lax.broadcasted_iota(jnp.int32, (n_q, 2 * HEAD_DIM), 1)
    return lane < HEAD_DIM, lane >= HEAD_DIM


def _ctx_attn_kernel(q_ref, k_ref, v_ref, o_ref, s_ref, p_ref):
    v_t = v_ref[...].T
    masks = _head_masks(T_CTX)

    def scores(h):
        cols = slice((h // 2) * 2 * HEAD_DIM, (h // 2 + 1) * 2 * HEAD_DIM)
        q = q_ref[:, cols]
        qm = jnp.where(masks[h % 2], q, jnp.zeros_like(q))
        return _store_scores(s_ref, h % ATTN_SLOTS, 0, _kq(k_ref[:, cols], qm))

    def probs(h, m):
        return _store_probs(s_ref, p_ref, h % ATTN_SLOTS, 0, T_CTX, m)

    def out(h, l):
        o = jnp.dot(v_t[h * HEAD_DIM:(h + 1) * HEAD_DIM, :], p_ref[h % ATTN_SLOTS], preferred_element_type=F32)
        return o / l

    outs = _attn_pipeline(N_HEADS, scores, probs, out)
    o_ref[...] = jnp.concatenate(outs, axis=0).T.astype(BF16)


def _ctx_attn_call(q, k, v):
    blk = pl.BlockSpec((T_CTX, D), lambda b: (N_LAT_ROWS // T_CTX + b, 0))
    return pl.pallas_call(
        _ctx_attn_kernel,
        grid=(B_CTX,),
        in_specs=[blk, blk, blk],
        out_specs=blk,
        out_shape=jax.ShapeDtypeStruct((N_ROWS, D), BF16),
        scratch_shapes=[pltpu.VMEM((ATTN_SLOTS, T_CTX, T_CTX), F32), pltpu.VMEM((ATTN_SLOTS, T_CTX, T_CTX), BF16)],
        compiler_params=_params(("arbitrary",)),
        name="ctx_attn",
    )(q, k, v)


def _key_rows_of_block(i):
    r_first = Q_BLK_ROWS * i
    return _row_window_start(r_first), _row_window_start(r_first + Q_BLK_ROWS - 1) + WIN_ROWS


N_Q_BLOCKS = GRID_ROWS // Q_BLK_ROWS
MAX_LAT_KEYS = max(kr1 - kr0 for kr0, kr1 in map(_key_rows_of_block, range(N_Q_BLOCKS))) * GRID_W


def _nbr_attn_kernel(q_ref, k_ref, v_ref, ckt_ref, cvt_ref, pt_ref, o_in_ref, o_ref, s_ref, p_ref):
    del o_in_ref
    v_t = v_ref[...].T
    ck = ckt_ref[...].reshape(2 * HEAD_DIM, PAST).astype(BF16).T
    cv_t = cvt_ref[...].reshape(2 * HEAD_DIM, PAST).astype(BF16)
    n_q = Q_BLK_ROWS * GRID_W
    masks = _head_masks(n_q)
    neg_blk = jnp.full((GRID_W, 2 * GRID_W), NEG_INF, F32)

    def unit(n):
        return (n // 2, n % 2, n % ATTN_SLOTS) + _key_rows_of_block(n // 2)

    def scores(n):
        i, hh, slot, kr0, kr1 = unit(n)
        r_first = Q_BLK_ROWS * i
        q = q_ref[r_first * GRID_W:(r_first + Q_BLK_ROWS) * GRID_W, :]
        qm = jnp.where(masks[hh], q, jnp.zeros_like(q))
        bias_rows = []
        for kr in range(kr0, kr1):
            blks = []
            for r0 in range(r_first, r_first + Q_BLK_ROWS, 2):
                typ = _pair_type(kr, r0)
                blks.append(neg_blk if typ is None else pt_ref[hh, typ])
            bias_rows.append(jnp.concatenate(blks, axis=1))
        s_lat = _kq(k_ref[kr0 * GRID_W:kr1 * GRID_W, :], qm) + jnp.concatenate(bias_rows, axis=0)
        m_lat = _store_scores(s_ref, slot, 0, s_lat)
        m_ctx = _store_scores(s_ref, slot, MAX_LAT_KEYS, _kq(ck, qm))
        return jnp.maximum(m_lat, m_ctx)

    def probs(n, m):
        _, _, slot, kr0, kr1 = unit(n)
        return (_store_probs(s_ref, p_ref, slot, 0, (kr1 - kr0) * GRID_W, m)
                + _store_probs(s_ref, p_ref, slot, MAX_LAT_KEYS, PAST, m))

    def out(n, l):
        _, hh, slot, kr0, kr1 = unit(n)
        hs = slice(hh * HEAD_DIM, (hh + 1) * HEAD_DIM)
        lk = (kr1 - kr0) * GRID_W
        o = (jnp.dot(v_t[hs, kr0 * GRID_W:kr1 * GRID_W], p_ref[slot, 0:lk, :], preferred_element_type=F32)
             + jnp.dot(cv_t[hs, :], p_ref[slot, MAX_LAT_KEYS:MAX_LAT_KEYS + PAST, :], preferred_element_type=F32))
        return o / l

    outs = _attn_pipeline(2 * N_Q_BLOCKS, scores, probs, out)
    blocks = [jnp.concatenate(outs[2 * i:2 * i + 2], axis=0) for i in range(N_Q_BLOCKS)]
    o_ref[...] = jnp.concatenate(blocks, axis=1).T.astype(BF16)


def _nbr_attn_call(q, k, v, cache_k, cache_v, pair_table, o_ctx):
    hw = 2 * HEAD_DIM
    blk = pl.BlockSpec((T_LAT, hw), lambda b, hp: (b, hp))
    cache = pl.BlockSpec((None, 2, HEAD_DIM, PAST), lambda b, hp: (b, hp, 0, 0))
    return pl.pallas_call(
        _nbr_attn_kernel,
        grid=(B_LAT, N_HEADS // 2),
        in_specs=[blk, blk, blk, cache, cache,
                  pl.BlockSpec((2, N_PAIR_TYPES, GRID_W, 2 * GRID_W), lambda b, hp: (hp, 0, 0, 0)),
                  pl.BlockSpec(memory_space=pl.ANY)],
        out_specs=blk,
        out_shape=jax.ShapeDtypeStruct((N_ROWS, D), BF16),
        input_output_aliases={6: 0},
        scratch_shapes=[pltpu.VMEM((ATTN_SLOTS, MAX_LAT_KEYS + PAST, Q_BLK_ROWS * GRID_W), F32),
                        pltpu.VMEM((ATTN_SLOTS, MAX_LAT_KEYS + PAST, Q_BLK_ROWS * GRID_W), BF16)],
        compiler_params=_params(("arbitrary", "arbitrary")),
        name="nbr_attn",
    )(q, k, v, cache_k, cache_v, pair_table, o_ctx)


def _post_kernel(final_norm, n_cast, x_ref, o_ref, m_ref, g_ref, wo_ref, wgu_ref, wd_ref, fg_ref, *refs):
    n_out = 2 if final_norm else 1
    cast_in = refs[:n_cast]
    out_refs = refs[n_cast:n_cast + n_out]
    cast_out = refs[n_cast + n_out:2 * n_cast + n_out]
    acc_ref, x1_ref = refs[2 * n_cast + n_out:]

    _cast_chunks(cast_in, cast_out)
    mix = jnp.dot(o_ref[...], wo_ref[...], preferred_element_type=F32)
    x1 = x_ref[...] + m_ref[2] * mix
    x1_ref[...] = x1
    h = _modulate(x1, g_ref[...], m_ref[3], m_ref[4]).astype(BF16)
    for c in range(D_FF // FF_CHUNK):
        cs = slice(c * FF_CHUNK, (c + 1) * FF_CHUNK)
        us = slice(D_FF + c * FF_CHUNK, D_FF + (c + 1) * FF_CHUNK)
        gate = jnp.dot(h, wgu_ref[:, cs], preferred_element_type=F32)
        up = jnp.dot(h, wgu_ref[:, us], preferred_element_type=F32)
        a = (gate * _sigmoid(gate) * up).astype(BF16)
        part = jnp.dot(a, wd_ref[cs, :], preferred_element_type=F32)
        if c == 0:
            acc_ref[...] = part
        else:
            acc_ref[...] += part
    def finish(out_ref):
        x2 = x1_ref[...] + m_ref[5] * acc_ref[...]
        if final_norm:
            ms = jnp.mean(x2 * x2, axis=-1, keepdims=True)
            x2 = x2 * lax.rsqrt(ms + RMS_EPS) * fg_ref[...]
        out_ref[...] = x2

    if not final_norm:
        finish(out_refs[0])
        return
    is_lat = pl.program_id(0) < N_LAT_TILES
    pl.when(is_lat)(lambda: finish(out_refs[0]))
    pl.when(jnp.logical_not(is_lat))(lambda: finish(out_refs[1]))


def _post_call(x, o, mod, g, w_o, w_gu, w_down, final_g, final_norm, cast_weights=()):
    cast_in, cast_out, cast_shapes = (zip(*[_cast_specs(w, l) for w, l in cast_weights])
                                      if cast_weights else ((), (), ()))
    if final_norm:
        out_specs = [_LAT_SPEC, _CTX_SPEC]
        out_shape = [jax.ShapeDtypeStruct((N_LAT_ROWS, D), F32), jax.ShapeDtypeStruct((N_CTX_ROWS, D), F32)]
    else:
        out_specs = [_ROW_SPEC]
        out_shape = [jax.ShapeDtypeStruct((N_ROWS, D), F32)]
    return pl.pallas_call(
        functools.partial(_post_kernel, final_norm, len(cast_weights)),
        grid=(N_TILES,),
        in_specs=[
            _ROW_SPEC, _ROW_SPEC, _MOD_SPEC,
            _const_spec((1, D)),
            _const_spec((D, D)),
            _const_spec((D, 2 * D_FF)),
            _const_spec((D_FF, D)),
            _const_spec((1, D)),
            *cast_in,
        ],
        out_specs=[*out_specs, *cast_out],
        out_shape=[*out_shape, *cast_shapes],
        scratch_shapes=[pltpu.VMEM((TM, D), F32), pltpu.VMEM((TM, D), F32)],
        compiler_params=_params(("arbitrary",)),
        name="mix_out_ffn",
    )(x, o, mod, g, w_o, w_gu, w_down, final_g, *[w for w, _ in cast_weights])


def _lru_in_kernel(x_ref, m_ref, g_ref, w_ref, gate_ref, xr_ref):
    xm = _modulate(x_ref[...], g_ref[...], m_ref[0], m_ref[1])
    y = jnp.dot(xm.astype(BF16), w_ref[...], preferred_element_type=F32)
    gate_ref[...] = jax.nn.gelu(y[:, :D]).astype(BF16)
    xr_ref[...] = y[:, D:]


def _lru_in_call(x, mod, g, w_in):
    return pl.pallas_call(
        _lru_in_kernel,
        grid=(N_TILES,),
        in_specs=[_ROW_SPEC, _MOD_SPEC, _const_spec((1, D)), _const_spec((D, 2 * D))],
        out_specs=[_ROW_SPEC, _ROW_SPEC],
        out_shape=[jax.ShapeDtypeStruct((N_ROWS, D), BF16), jax.ShapeDtypeStruct((N_ROWS, D), F32)],
        compiler_params=_params(("arbitrary",)),
        name="lru_in_proj",
    )(x, mod, g, w_in)


def _lru_scan_kernel(nb, nt, aliased, xr_ref, gg_ref, cw_ref, cb_ref, wa_ref, wi_ref, ba_ref, bi_ref, lam_ref,
                     h0_ref, *refs):
    y_ref, ht_ref, xt_ref, hf_ref, ab_ref, bb_ref, af0, bf0, af1, bf1 = refs[1:] if aliased else refs
    fwd_slots = ((af0, bf0), (af1, bf1))
    left = CONV_W // 2
    pad_rows = (CONV_W - 1) * nb
    xt_ref[pl.ds(0, left * nb), :] = jnp.zeros((left * nb, LRU_BW), F32)
    xt_ref[pl.ds((left + nt) * nb, pad_rows - left * nb), :] = jnp.zeros((pad_rows - left * nb, LRU_BW), F32)
    for s in range(nb):
        xt_ref[pl.ds(left * nb + s, nt, stride=nb), :] = xr_ref[pl.ds(s * nt, nt), :]

    cw = cw_ref[...]
    cb = cb_ref[...]
    w4 = (jnp.concatenate([wa_ref[0], wi_ref[0], wa_ref[1], wi_ref[1]], axis=1) * 0.5).astype(BF16)
    half_ba = 0.5 * ba_ref[...]
    half_bi = 0.5 * bi_ref[...]
    neg_lam = -lam_ref[...]
    sp = jnp.maximum(neg_lam, 0.0) + jnp.log1p(jnp.exp(-jnp.abs(neg_lam)))
    c1 = (-0.5 * LRU_C * LOG2E) * sp
    crows = SCAN_CT * nb
    n_chunks = nt // SCAN_CT
    assert n_chunks % 2 == 0

    def chunk_rows(c):
        return pl.ds(pl.multiple_of(c * crows, 8), crows)

    def coeffs(c, slot):
        t0 = c * SCAN_CT
        xc = cb
        for j in range(CONV_W):
            xc = xc + xt_ref[pl.ds(pl.multiple_of((t0 + j) * nb, 8), crows), :] * cw[j:j + 1]
        half_xc = 0.5 * xc
        pre = jnp.dot(xc.astype(BF16), w4, preferred_element_type=F32)

        def direction(d):
            o = 2 * d * LRU_BW
            t_r = jnp.tanh(pre[:, o:o + LRU_BW] + half_ba[d:d + 1])
            a = jnp.exp2(c1[d:d + 1] * t_r + c1[d:d + 1])
            t_i = jnp.tanh(pre[:, o + LRU_BW:o + 2 * LRU_BW] + half_bi[d:d + 1])
            y = 1.0 - a * a
            root = jnp.where(y > 0.0, y * lax.rsqrt(y), 0.0)
            return a, root * (half_xc * t_i + half_xc)

        a_ref, b_ref = fwd_slots[slot]
        a_ref[...], b_ref[...] = direction(0)
        ab_ref[chunk_rows(c), :], bb_ref[chunk_rows(c), :] = direction(1)

    def scan_fwd(c, slot, h):
        a_ref, b_ref = fwd_slots[slot]
        for s in range(SCAN_CT):
            rows = slice(s * nb, (s + 1) * nb)
            h = a_ref[rows, :] * h + b_ref[rows, :]
            hf_ref[pl.ds(pl.multiple_of(c * crows + s * nb, 8), nb), :] = h
        return h

    def scan_bwd(c, h):
        for s in range(SCAN_CT - 1, 0, -2):
            rows1 = pl.ds(pl.multiple_of(c * crows + s * nb, 8), nb)
            rows0 = pl.ds(pl.multiple_of(c * crows + (s - 1) * nb, 8), nb)
            a1, b1 = ab_ref[rows1, :], bb_ref[rows1, :]
            a0, b0 = ab_ref[rows0, :], bb_ref[rows0, :]
            bb_ref[rows1, :] = a1 * h + b1
            h = (a0 * a1) * h + (a0 * b1 + b0)
            bb_ref[rows0, :] = h
        return h

    def emit(c):
        for s in range(nb):
            src = pl.ds(c * crows + s, SCAN_CT, stride=nb)
            dst = pl.ds(pl.multiple_of(s * nt + c * SCAN_CT, SCAN_CT), SCAN_CT)
            hsum = hf_ref[src, :] + bb_ref[src, :]
            y_ref[dst, :] = (hsum * gg_ref[dst, :].astype(F32)).astype(BF16)

    coeffs(0, 0)

    def pass1(k, h):
        c = 2 * k
        h = scan_fwd(c, 0, h)
        coeffs(c + 1, 1)
        h = scan_fwd(c + 1, 1, h)
        coeffs(jnp.minimum(c + 2, n_chunks - 1), 0)
        return h

    ht_ref[0] = lax.fori_loop(0, n_chunks // 2, pass1, h0_ref[0])

    hb = scan_bwd(n_chunks - 1, h0_ref[1])

    def pass2(k, h):
        c = n_chunks - 2 - k
        emit(c + 1)
        return scan_bwd(c, h)

    ht_ref[1] = lax.fori_loop(0, n_chunks - 1, pass2, hb)
    emit(0)


def _lru_scan_call(nb, nt, row_block0, gg, xr, h0, conv_w, conv_b, w_a, w_i, b_a, b_i, lam, y_prev=None):
    rows = nb * nt
    col = pl.BlockSpec((rows, LRU_BW), lambda n: (row_block0, n))
    vec2 = pl.BlockSpec((2, LRU_BW), lambda n: (0, n))
    wblk = pl.BlockSpec((2, None, LRU_BW, LRU_BW), lambda n: (0, n, 0, 0))
    state = pl.BlockSpec((2, nb, LRU_BW), lambda n: (0, 0, n))
    tm_rows = (nt + CONV_W - 1) * nb
    crows = SCAN_CT * nb
    aliased = y_prev is not None
    return pl.pallas_call(
        functools.partial(_lru_scan_kernel, nb, nt, aliased),
        grid=(LRU_BLOCKS,),
        in_specs=[
            col, col,
            pl.BlockSpec((CONV_W, LRU_BW), lambda n: (0, n)),
            pl.BlockSpec((1, LRU_BW), lambda n: (0, n)),
            wblk, wblk, vec2, vec2, vec2, state,
        ] + ([pl.BlockSpec(memory_space=pl.ANY)] if aliased else []),
        out_specs=[col, state],
        out_shape=[jax.ShapeDtypeStruct((N_ROWS, D), BF16), jax.ShapeDtypeStruct((2, nb, D), F32)],
        input_output_aliases={10: 0} if aliased else {},
        scratch_shapes=[pltpu.VMEM((tm_rows, LRU_BW), F32)] + [pltpu.VMEM((rows, LRU_BW), F32)] * 3
        + [pltpu.VMEM((crows, LRU_BW), F32)] * 4,
        compiler_params=_params(("arbitrary",)),
        name=f"lru_scan_{nb}x{nt}",
    )(xr, gg, conv_w, conv_b, w_a, w_i, b_a, b_i, lam, h0, *([y_prev] if aliased else []))


def kernel(x_prompt, x_sample, c, cache_k, cache_v, state_h, c_ctx, norm_g, w_mod, b_mod, attn_w_qkv, attn_w_o,
           attn_rpb, lru_w_in, lru_conv_w, lru_conv_b, lru_w_a, lru_b_a, lru_w_i, lru_b_i, lru_lam, lru_w_out,
           ffn_w_gu, ffn_w_down, final_g):
    x_lat = x_sample.reshape(N_LAT_ROWS, D)
    x_ctx = x_prompt.reshape(N_CTX_ROWS, D)
    cond = jnp.concatenate([c, c_ctx[None, :], jnp.zeros((N_COND - B_LAT - 1, D), F32)], axis=0)
    mod = _mod_call(cond, w_mod, b_mod).reshape(2, N_COND, N_MOD, 1, D)
    final_g2 = final_g.reshape(1, D)

    q, k, v, k_ctx, v_ctx, x, w_o, w_gu0, w_down0 = _qkv_call(
        x_lat, x_ctx, mod[0], norm_g[0, 0].reshape(1, D), attn_w_qkv[0].astype(BF16),
        ((attn_w_o, 0), (ffn_w_gu, 0), (ffn_w_down, 0)))
    o = _ctx_attn_call(q, k, v)
    cache_kt = jnp.transpose(cache_k[:, 0], (0, 2, 3, 1))
    cache_vt = jnp.transpose(cache_v[:, 0], (0, 2, 3, 1))
    o = _nbr_attn_call(q, k, v, cache_kt, cache_vt, _pair_table_call(attn_rpb[0]), o)
    x, w_in, w_out, w_gu1, w_down1 = _post_call(
        x, o, mod[0], norm_g[0, 1].reshape(1, D), w_o, w_gu0, w_down0, final_g2, False,
        ((lru_w_in, 0), (lru_w_out, 0), (ffn_w_gu, 1), (ffn_w_down, 1)))

    gg, xr = _lru_in_call(x, mod[1], norm_g[1, 0].reshape(1, D), w_in)
    lru_p = (lru_conv_w[0], lru_conv_b[0].reshape(1, D), lru_w_a[0], lru_w_i[0], lru_b_a[0], lru_b_i[0], lru_lam[0])
    y, _ = _lru_scan_call(B_LAT, T_LAT, 0, gg, xr, jnp.swapaxes(state_h[:, 0], 0, 1), *lru_p)
    y, h_ctx = _lru_scan_call(B_CTX, T_CTX, N_LAT_ROWS // N_CTX_ROWS, gg, xr,
                              jnp.zeros((2, B_CTX, D), F32), *lru_p, y_prev=y)
    out_lat, out_ctx = _post_call(x, y, mod[1], norm_g[1, 1].reshape(1, D), w_out, w_gu1, w_down1, final_g2, True)

    y_sample = out_lat.reshape(B_LAT, T_LAT, D)
    y_prompt = out_ctx.reshape(B_CTX, T_CTX, D)
    new_k = jnp.transpose(k_ctx, (0, 3, 1, 2))[:, None]
    new_v = jnp.transpose(v_ctx, (0, 3, 1, 2))[:, None]
    new_h = jnp.swapaxes(h_ctx, 0, 1)[:, None]
    return y_prompt, y_sample, new_k, new_v, new_h
```

```python
import functools
import math

import jax
import jax.numpy as jnp
from jax import lax
from jax.experimental import pallas as pl
from jax.experimental.pallas import tpu as pltpu

F32 = jnp.float32
BF16 = jnp.bfloat16

D = 1024
B_CTX, T_CTX = 16, 256
B_LAT, T_LAT = 8, 1024
PAST = 512
GRID_W = 64
GRID_ROWS = T_LAT // GRID_W
N_HEADS = 16
HEAD_DIM = D // N_HEADS
ATTN_SCALE = HEAD_DIM ** -0.5
LOG2E = math.log2(math.e)
WIN_ROWS, WIN_COLS = 8, 16
N_DR = 2 * WIN_ROWS - 1
N_DC = 2 * WIN_COLS - 1
LRU_BLOCKS = 8
LRU_BW = D // LRU_BLOCKS
CONV_W = 4
LRU_C = 8.0
D_FF = 2816
N_MOD = 6
RMS_EPS = 1e-6
NEG_INF = -1e30
LANES = 128

N_LAT_ROWS = B_LAT * T_LAT
N_CTX_ROWS = B_CTX * T_CTX
N_ROWS = N_LAT_ROWS + N_CTX_ROWS
N_COND = 16

TM = 512
N_LAT_TILES = N_LAT_ROWS // TM
N_CTX_TILES = N_CTX_ROWS // TM
N_TILES = N_LAT_TILES + N_CTX_TILES
TILES_PER_LAT_SAMPLE = T_LAT // TM
MOD_TN = 1536
FF_CHUNK = 256
Q_BLK_ROWS = 4
N_PAIR_TYPES = 16
PAIR_TABLE_HEADS = 4
CAST_CHUNKS = 16
SCAN_CT = 32
VMEM_LIMIT = 56 * 1024 * 1024


def _sigmoid(x):
    return 0.5 * jnp.tanh(0.5 * x) + 0.5


def _modulate(x, g, shift, scale):
    ms = jnp.mean(x * x, axis=-1, keepdims=True)
    return (x * lax.rsqrt(ms + RMS_EPS)) * g * (1.0 + scale) + shift


def _cond_row_of_tile(i):
    return jnp.where(i < N_LAT_TILES, i // TILES_PER_LAT_SAMPLE, B_LAT)


def _const_spec(shape):
    nd = len(shape)
    return pl.BlockSpec(shape, lambda *_: (0,) * nd, pipeline_mode=pl.Buffered(1))


def _params(sem):
    return pltpu.CompilerParams(dimension_semantics=sem, vmem_limit_bytes=VMEM_LIMIT)


_ROW_SPEC = pl.BlockSpec((TM, D), lambda i: (i, 0))
_LAT_SPEC = pl.BlockSpec((TM, D), lambda i: (jnp.minimum(i, N_LAT_TILES - 1), 0))
_CTX_SPEC = pl.BlockSpec((TM, D), lambda i: (jnp.maximum(i - N_LAT_TILES, 0), 0))
_MOD_SPEC = pl.BlockSpec((None, N_MOD, 1, D), lambda i: (_cond_row_of_tile(i), 0, 0, 0))


def _cast_specs(w, layer):
    _, rows, cols = w.shape
    chunk = lambda i: jnp.minimum(i, CAST_CHUNKS - 1)
    return (pl.BlockSpec((None, rows // CAST_CHUNKS, cols), lambda i: (layer, chunk(i), 0)),
            pl.BlockSpec((rows // CAST_CHUNKS, cols), lambda i: (chunk(i), 0)),
            jax.ShapeDtypeStruct((rows, cols), BF16))


def _cast_chunks(in_refs, out_refs):
    @pl.when(pl.program_id(0) < CAST_CHUNKS)
    def _():
        for src, dst in zip(in_refs, out_refs):
            dst[...] = src[...].astype(BF16)


def _mod_kernel(cond_ref, w_ref, b_ref, o_ref):
    c = cond_ref[...]
    s = (c * _sigmoid(c)).astype(BF16)
    o_ref[...] = jnp.dot(s, w_ref[...].astype(BF16), preferred_element_type=F32) + b_ref[...]


def _mod_call(cond, w_mod, b_mod):
    depth = w_mod.shape[0]
    n = N_MOD * D
    return pl.pallas_call(
        _mod_kernel,
        grid=(depth, n // MOD_TN),
        in_specs=[
            pl.BlockSpec((N_COND, D), lambda l, j: (0, 0)),
            pl.BlockSpec((None, D, MOD_TN), lambda l, j: (l, 0, j)),
            pl.BlockSpec((None, 1, MOD_TN), lambda l, j: (l, 0, j)),
        ],
        out_specs=pl.BlockSpec((None, N_COND, MOD_TN), lambda l, j: (l, 0, j)),
        out_shape=jax.ShapeDtypeStruct((depth, N_COND, n), F32),
        compiler_params=_params(("arbitrary", "arbitrary")),
        name="adaln_mod",
    )(cond, w_mod, b_mod.reshape(depth, 1, n))


def _pair_table_kernel(r_ref, o_ref):
    shape = (GRID_W, 2 * GRID_W)
    ck = lax.broadcasted_iota(jnp.int32, shape, 0)
    lane = lax.broadcasted_iota(jnp.int32, shape, 1)
    cq = jnp.bitwise_and(lane, GRID_W - 1)
    cs = jnp.clip(cq - WIN_COLS // 2, 0, GRID_W - WIN_COLS)
    in_win = jnp.logical_and(ck >= cs, ck < cs + WIN_COLS)

    def toeplitz(h, d, lane0):
        base = jnp.broadcast_to(r_ref[h, d:d + 1, :], shape)
        return pltpu.roll(base, (lane0 - (WIN_COLS - 1)) % LANES, 1, stride=1, stride_axis=0)

    neg = jnp.full(shape, NEG_INF, F32)
    for h in range(PAIR_TABLE_HEADS):
        for t in range(N_PAIR_TYPES):
            d_left, d_right = (t + 1, t) if t < 14 else ((3, None) if t == 14 else (None, 10))
            left = neg if d_left is None else toeplitz(h, d_left, 0)
            right = neg if d_right is None else toeplitz(h, d_right, GRID_W)
            val = jnp.where(lane < GRID_W, left, right)
            o_ref[h, t] = jnp.where(in_win, val * LOG2E, NEG_INF)


def _pair_table_call(rpb):
    assert 2 * GRID_W == LANES
    r = jnp.pad(rpb[:, :, ::-1], ((0, 0), (0, 0), (0, LANES - N_DC)))
    return pl.pallas_call(
        _pair_table_kernel,
        grid=(N_HEADS // PAIR_TABLE_HEADS,),
        in_specs=[pl.BlockSpec((PAIR_TABLE_HEADS, N_DR, LANES), lambda h: (h, 0, 0))],
        out_specs=pl.BlockSpec((PAIR_TABLE_HEADS, N_PAIR_TYPES, GRID_W, LANES), lambda h: (h, 0, 0, 0)),
        out_shape=jax.ShapeDtypeStruct((N_HEADS, N_PAIR_TYPES, GRID_W, LANES), F32),
        compiler_params=_params(("arbitrary",)),
        name="rpb_table",
    )(r)


def _row_window_start(r):
    return min(max(r - WIN_ROWS // 2, 0), GRID_ROWS - WIN_ROWS)


def _pair_type(kr, r0):
    in0 = _row_window_start(r0) <= kr < _row_window_start(r0) + WIN_ROWS
    in1 = _row_window_start(r0 + 1) <= kr < _row_window_start(r0 + 1) + WIN_ROWS
    if in0 and in1:
        dl = kr - r0
        assert -6 <= dl <= 7
        return dl + 6
    if in0:
        assert kr - r0 == -(WIN_ROWS // 2)
        return 14
    if in1:
        assert kr - (r0 + 1) == WIN_ROWS // 2 - 1
        return 15
    return None


def _qkv_kernel(n_cast, xl_ref, xc_ref, m_ref, g_ref, w_ref, *refs):
    cast_in = refs[:n_cast]
    q_ref, k_ref, v_ref, kf_ref, vf_ref, x_ref = refs[n_cast:n_cast + 6]
    cast_out = refs[n_cast + 6:]
    _cast_chunks(cast_in, cast_out)
    x = jnp.where(pl.program_id(0) < N_LAT_TILES, xl_ref[...], xc_ref[...])
    x_ref[...] = x
    xm = _modulate(x, g_ref[...], m_ref[0], m_ref[1])
    qkv = jnp.dot(xm.astype(BF16), w_ref[...], preferred_element_type=F32)
    q_ref[...] = (qkv[:, :D] * (ATTN_SCALE * LOG2E)).astype(BF16)
    k = qkv[:, D:2 * D]
    v = qkv[:, 2 * D:]
    k_ref[...] = k.astype(BF16)
    v_ref[...] = v.astype(BF16)

    @pl.when(pl.program_id(0) >= N_LAT_TILES)
    def _():
        for src, dst in ((k, kf_ref), (v, vf_ref)):
            t = src.T.reshape(N_HEADS, HEAD_DIM, TM)
            for s in range(TM // T_CTX):
                dst[s] = t[:, :, s * T_CTX:(s + 1) * T_CTX]


def _qkv_call(x_lat, x_ctx, mod, g, w_qkv, cast_weights):
    cast_in, cast_out, cast_shapes = zip(*[_cast_specs(w, l) for w, l in cast_weights])
    new_cache = pl.BlockSpec((TM // T_CTX, N_HEADS, HEAD_DIM, T_CTX),
                             lambda i: (jnp.maximum(i - N_LAT_TILES, 0), 0, 0, 0))
    return pl.pallas_call(
        functools.partial(_qkv_kernel, len(cast_weights)),
        grid=(N_TILES,),
        in_specs=[_LAT_SPEC, _CTX_SPEC, _MOD_SPEC, _const_spec((1, D)), _const_spec((D, 3 * D)), *cast_in],
        out_specs=[_ROW_SPEC, _ROW_SPEC, _ROW_SPEC, new_cache, new_cache, _ROW_SPEC, *cast_out],
        out_shape=[jax.ShapeDtypeStruct((N_ROWS, D), BF16)] * 3
        + [jax.ShapeDtypeStruct((B_CTX, N_HEADS, HEAD_DIM, T_CTX), F32)] * 2
        + [jax.ShapeDtypeStruct((N_ROWS, D), F32)] + list(cast_shapes),
        compiler_params=_params(("arbitrary",)),
        name="qkv_proj",
    )(x_lat, x_ctx, mod, g, w_qkv, *[w for w, _ in cast_weights])


ATTN_SLOTS = 3


def _kq(k, q):
    return lax.dot_general(k, q, (((1,), (1,)), ((), ())), preferred_element_type=F32)


def _attn_pipeline(n_units, scores_fn, probs_fn, out_fn):
    assert ATTN_SLOTS == 3
    col_max = {n: scores_fn(n) for n in range(min(2, n_units))}
    col_sum = {}
    outs = []
    for n in range(n_units):
        if n + 2 < n_units:
            col_max[n + 2] = scores_fn(n + 2)
        col_sum[n] = probs_fn(n, col_max[n])
        if n >= 1:
            outs.append(out_fn(n - 1, col_sum[n - 1]))
    outs.append(out_fn(n_units - 1, col_sum[n_units - 1]))
    return outs


def _store_scores(s_ref, slot, row0, s):
    s_ref[slot, row0:row0 + s.shape[0], :] = s
    return s.max(axis=0, keepdims=True)


def _store_probs(s_ref, p_ref, slot, row0, rows, m):
    p = jnp.exp2(s_ref[slot, row0:row0 + rows, :] - m)
    p_ref[slot, row0:row0 + rows, :] = p.astype(BF16)
    return p.sum(axis=0, keepdims=True)


def _head_masks(n_q):
    lane = lax.broadcasted_iota(jnp.int32, (n_q, 2 * HEAD_DIM), 1)
    return lane < HEAD_DIM, lane >= HEAD_DIM


def _ctx_attn_kernel(q_ref, k_ref, v_ref, o_ref, s_ref, p_ref):
    v_t = v_ref[...].T
    masks = _head_masks(T_CTX)

    def scores(h):
        cols = slice((h // 2) * 2 * HEAD_DIM, (h // 2 + 1) * 2 * HEAD_DIM)
        q = q_ref[:, cols]
        qm = jnp.where(masks[h % 2], q, jnp.zeros_like(q))
        return _store_scores(s_ref, h % ATTN_SLOTS, 0, _kq(k_ref[:, cols], qm))

    def probs(h, m):
        return _store_probs(s_ref, p_ref, h % ATTN_SLOTS, 0, T_CTX, m)

    def out(h, l):
        o = jnp.dot(v_t[h * HEAD_DIM:(h + 1) * HEAD_DIM, :], p_ref[h % ATTN_SLOTS], preferred_element_type=F32)
        return o / l

    outs = _attn_pipeline(N_HEADS, scores, probs, out)
    o_ref[...] = jnp.concatenate(outs, axis=0).T.astype(BF16)


def _ctx_attn_call(q, k, v):
    blk = pl.BlockSpec((T_CTX, D), lambda b: (N_LAT_ROWS // T_CTX + b, 0))
    return pl.pallas_call(
        _ctx_attn_kernel,
        grid=(B_CTX,),
        in_specs=[blk, blk, blk],
        out_specs=blk,
        out_shape=jax.ShapeDtypeStruct((N_ROWS, D), BF16),
        scratch_shapes=[pltpu.VMEM((ATTN_SLOTS, T_CTX, T_CTX), F32), pltpu.VMEM((ATTN_SLOTS, T_CTX, T_CTX), BF16)],
        compiler_params=_params(("arbitrary",)),
        name="ctx_attn",
    )(q, k, v)


def _key_rows_of_block(i):
    r_first = Q_BLK_ROWS * i
    return _row_window_start(r_first), _row_window_start(r_first + Q_BLK_ROWS - 1) + WIN_ROWS


N_Q_BLOCKS = GRID_ROWS // Q_BLK_ROWS
MAX_LAT_KEYS = max(kr1 - kr0 for kr0, kr1 in map(_key_rows_of_block, range(N_Q_BLOCKS))) * GRID_W


NBR_HEADS = 4


def _nbr_attn_kernel(q_ref, k_ref, v_ref, ckt_ref, cvt_ref, pt_ref, o_in_ref, o_ref, s_ref, p_ref):
    del o_in_ref
    pw = 2 * HEAD_DIM
    v_t = v_ref[...].T
    ck = ckt_ref[...].reshape(NBR_HEADS * HEAD_DIM, PAST).astype(BF16).T
    cv_t = cvt_ref[...].reshape(NBR_HEADS * HEAD_DIM, PAST).astype(BF16)
    n_q = Q_BLK_ROWS * GRID_W
    masks = _head_masks(n_q)
    neg_blk = jnp.full((GRID_W, 2 * GRID_W), NEG_INF, F32)

    def unit(n):
        h, i = divmod(n, N_Q_BLOCKS)
        return (h, i, n % ATTN_SLOTS) + _key_rows_of_block(i)

    def scores(n):
        h, i, slot, kr0, kr1 = unit(n)
        cols = slice((h // 2) * pw, (h // 2 + 1) * pw)
        r_first = Q_BLK_ROWS * i
        q = q_ref[r_first * GRID_W:(r_first + Q_BLK_ROWS) * GRID_W, cols]
        qm = jnp.where(masks[h % 2], q, jnp.zeros_like(q))
        bias_rows = []
        for kr in range(kr0, kr1):
            blks = []
            for r0 in range(r_first, r_first + Q_BLK_ROWS, 2):
                typ = _pair_type(kr, r0)
                blks.append(neg_blk if typ is None else pt_ref[h, typ])
            bias_rows.append(jnp.concatenate(blks, axis=1))
        s_lat = _kq(k_ref[kr0 * GRID_W:kr1 * GRID_W, cols], qm) + jnp.concatenate(bias_rows, axis=0)
        m_lat = _store_scores(s_ref, slot, 0, s_lat)
        m_ctx = _store_scores(s_ref, slot, MAX_LAT_KEYS, _kq(ck[:, cols], qm))
        return jnp.maximum(m_lat, m_ctx)

    def probs(n, m):
        _, _, slot, kr0, kr1 = unit(n)
        return (_store_probs(s_ref, p_ref, slot, 0, (kr1 - kr0) * GRID_W, m)
                + _store_probs(s_ref, p_ref, slot, MAX_LAT_KEYS, PAST, m))

    def out(n, l):
        h, _, slot, kr0, kr1 = unit(n)
        hs = slice(h * HEAD_DIM, (h + 1) * HEAD_DIM)
        lk = (kr1 - kr0) * GRID_W
        o = (jnp.dot(v_t[hs, kr0 * GRID_W:kr1 * GRID_W], p_ref[slot, 0:lk, :], preferred_element_type=F32)
             + jnp.dot(cv_t[hs, :], p_ref[slot, MAX_LAT_KEYS:MAX_LAT_KEYS + PAST, :], preferred_element_type=F32))
        return o / l

    outs = _attn_pipeline(NBR_HEADS * N_Q_BLOCKS, scores, probs, out)
    o_t = jnp.concatenate([jnp.concatenate(outs[h * N_Q_BLOCKS:(h + 1) * N_Q_BLOCKS], axis=1)
                           for h in range(NBR_HEADS)], axis=0)
    o_ref[...] = o_t.T.astype(BF16)


def _nbr_attn_call(q, k, v, cache_k, cache_v, pair_table, o_ctx):
    assert NBR_HEADS % 2 == 0
    blk = pl.BlockSpec((T_LAT, NBR_HEADS * HEAD_DIM), lambda b, g: (b, g))
    cache = pl.BlockSpec((None, NBR_HEADS, HEAD_DIM, PAST), lambda b, g: (b, g, 0, 0))
    return pl.pallas_call(
        _nbr_attn_kernel,
        grid=(B_LAT, N_HEADS // NBR_HEADS),
        in_specs=[blk, blk, blk, cache, cache,
                  pl.BlockSpec((NBR_HEADS, N_PAIR_TYPES, GRID_W, 2 * GRID_W), lambda b, g: (g, 0, 0, 0)),
                  pl.BlockSpec(memory_space=pl.ANY)],
        out_specs=blk,
        out_shape=jax.ShapeDtypeStruct((N_ROWS, D), BF16),
        input_output_aliases={6: 0},
        scratch_shapes=[pltpu.VMEM((ATTN_SLOTS, MAX_LAT_KEYS + PAST, Q_BLK_ROWS * GRID_W), F32),
                        pltpu.VMEM((ATTN_SLOTS, MAX_LAT_KEYS + PAST, Q_BLK_ROWS * GRID_W), BF16)],
        compiler_params=_params(("arbitrary", "arbitrary")),
        name="nbr_attn",
    )(q, k, v, cache_k, cache_v, pair_table, o_ctx)


def _post_kernel(final_norm, n_cast, x_ref, o_ref, m_ref, g_ref, wo_ref, wgu_ref, wd_ref, fg_ref, *refs):
    n_out = 2 if final_norm else 1
    cast_in = refs[:n_cast]
    out_refs = refs[n_cast:n_cast + n_out]
    cast_out = refs[n_cast + n_out:2 * n_cast + n_out]
    acc_ref, x1_ref = refs[2 * n_cast + n_out:]

    _cast_chunks(cast_in, cast_out)
    mix = jnp.dot(o_ref[...], wo_ref[...], preferred_element_type=F32)
    x1 = x_ref[...] + m_ref[2] * mix
    x1_ref[...] = x1
    h = _modulate(x1, g_ref[...], m_ref[3], m_ref[4]).astype(BF16)
    for c in range(D_FF // FF_CHUNK):
        cs = slice(c * FF_CHUNK, (c + 1) * FF_CHUNK)
        us = slice(D_FF + c * FF_CHUNK, D_FF + (c + 1) * FF_CHUNK)
        gate = jnp.dot(h, wgu_ref[:, cs], preferred_element_type=F32)
        up = jnp.dot(h, wgu_ref[:, us], preferred_element_type=F32)
        a = (gate * _sigmoid(gate) * up).astype(BF16)
        part = jnp.dot(a, wd_ref[cs, :], preferred_element_type=F32)
        if c == 0:
            acc_ref[...] = part
        else:
            acc_ref[...] += part
    def finish(out_ref):
        x2 = x1_ref[...] + m_ref[5] * acc_ref[...]
        if final_norm:
            ms = jnp.mean(x2 * x2, axis=-1, keepdims=True)
            x2 = x2 * lax.rsqrt(ms + RMS_EPS) * fg_ref[...]
        out_ref[...] = x2

    if not final_norm:
        finish(out_refs[0])
        return
    finish(x1_ref)
    is_lat = pl.program_id(0) < N_LAT_TILES

    @pl.when(is_lat)
    def _():
        out_refs[0][...] = x1_ref[...]

    @pl.when(jnp.logical_not(is_lat))
    def _():
        out_refs[1][...] = x1_ref[...]


def _post_call(x, o, mod, g, w_o, w_gu, w_down, final_g, final_norm, cast_weights=()):
    cast_in, cast_out, cast_shapes = (zip(*[_cast_specs(w, l) for w, l in cast_weights])
                                      if cast_weights else ((), (), ()))
    if final_norm:
        out_specs = [_LAT_SPEC, _CTX_SPEC]
        out_shape = [jax.ShapeDtypeStruct((N_LAT_ROWS, D), F32), jax.ShapeDtypeStruct((N_CTX_ROWS, D), F32)]
    else:
        out_specs = [_ROW_SPEC]
        out_shape = [jax.ShapeDtypeStruct((N_ROWS, D), F32)]
    return pl.pallas_call(
        functools.partial(_post_kernel, final_norm, len(cast_weights)),
        grid=(N_TILES,),
        in_specs=[
            _ROW_SPEC, _ROW_SPEC, _MOD_SPEC,
            _const_spec((1, D)),
            _const_spec((D, D)),
            _const_spec((D, 2 * D_FF)),
            _const_spec((D_FF, D)),
            _const_spec((1, D)),
            *cast_in,
        ],
        out_specs=[*out_specs, *cast_out],
        out_shape=[*out_shape, *cast_shapes],
        scratch_shapes=[pltpu.VMEM((TM, D), F32), pltpu.VMEM((TM, D), F32)],
        compiler_params=_params(("arbitrary",)),
        name="mix_out_ffn",
    )(x, o, mod, g, w_o, w_gu, w_down, final_g, *[w for w, _ in cast_weights])


def _lru_in_kernel(x_ref, m_ref, g_ref, w_ref, gate_ref, xr_ref):
    xm = _modulate(x_ref[...], g_ref[...], m_ref[0], m_ref[1])
    y = jnp.dot(xm.astype(BF16), w_ref[...], preferred_element_type=F32)
    gate_ref[...] = jax.nn.gelu(y[:, :D]).astype(BF16)
    xr_ref[...] = y[:, D:]


def _lru_in_call(x, mod, g, w_in):
    return pl.pallas_call(
        _lru_in_kernel,
        grid=(N_TILES,),
        in_specs=[_ROW_SPEC, _MOD_SPEC, _const_spec((1, D)), _const_spec((D, 2 * D))],
        out_specs=[_ROW_SPEC, _ROW_SPEC],
        out_shape=[jax.ShapeDtypeStruct((N_ROWS, D), BF16), jax.ShapeDtypeStruct((N_ROWS, D), F32)],
        compiler_params=_params(("arbitrary",)),
        name="lru_in_proj",
    )(x, mod, g, w_in)


def _lru_scan_kernel(nb, nt, aliased, xr_ref, gg_ref, cw_ref, cb_ref, wa_ref, wi_ref, ba_ref, bi_ref, lam_ref,
                     h0_ref, *refs):
    y_ref, ht_ref, xt_ref, hf_ref, ab_ref, bb_ref, af0, bf0, af1, bf1 = refs[1:] if aliased else refs
    fwd_slots = ((af0, bf0), (af1, bf1))
    left = CONV_W // 2
    pad_rows = (CONV_W - 1) * nb
    xt_ref[pl.ds(0, left * nb), :] = jnp.zeros((left * nb, LRU_BW), F32)
    xt_ref[pl.ds((left + nt) * nb, pad_rows - left * nb), :] = jnp.zeros((pad_rows - left * nb, LRU_BW), F32)
    for s in range(nb):
        xt_ref[pl.ds(left * nb + s, nt, stride=nb), :] = xr_ref[pl.ds(s * nt, nt), :]

    cw = cw_ref[...]
    cb = cb_ref[...]
    w4 = (jnp.concatenate([wa_ref[0], wi_ref[0], wa_ref[1], wi_ref[1]], axis=1) * 0.5).astype(BF16)
    half_ba = 0.5 * ba_ref[...]
    half_bi = 0.5 * bi_ref[...]
    neg_lam = -lam_ref[...]
    sp = jnp.maximum(neg_lam, 0.0) + jnp.log1p(jnp.exp(-jnp.abs(neg_lam)))
    c1 = (-0.5 * LRU_C * LOG2E) * sp
    crows = SCAN_CT * nb
    n_chunks = nt // SCAN_CT
    assert n_chunks % 2 == 0

    def chunk_rows(c):
        return pl.ds(pl.multiple_of(c * crows, 8), crows)

    def coeffs(c, slot):
        t0 = c * SCAN_CT
        xc = cb
        for j in range(CONV_W):
            xc = xc + xt_ref[pl.ds(pl.multiple_of((t0 + j) * nb, 8), crows), :] * cw[j:j + 1]
        half_xc = 0.5 * xc
        pre = jnp.dot(xc.astype(BF16), w4, preferred_element_type=F32)

        def direction(d):
            o = 2 * d * LRU_BW
            t_r = jnp.tanh(pre[:, o:o + LRU_BW] + half_ba[d:d + 1])
            a = jnp.exp2(c1[d:d + 1] * t_r + c1[d:d + 1])
            t_i = jnp.tanh(pre[:, o + LRU_BW:o + 2 * LRU_BW] + half_bi[d:d + 1])
            y = 1.0 - a * a
            root = jnp.where(y > 0.0, y * lax.rsqrt(y), 0.0)
            return a, root * (half_xc * t_i + half_xc)

        a_ref, b_ref = fwd_slots[slot]
        a_ref[...], b_ref[...] = direction(0)
        ab_ref[chunk_rows(c), :], bb_ref[chunk_rows(c), :] = direction(1)

    def scan_fwd(c, slot, h):
        a_ref, b_ref = fwd_slots[slot]
        for s in range(SCAN_CT):
            rows = slice(s * nb, (s + 1) * nb)
            h = a_ref[rows, :] * h + b_ref[rows, :]
            hf_ref[pl.ds(pl.multiple_of(c * crows + s * nb, 8), nb), :] = h
        return h

    def scan_bwd(c, h):
        for s in range(SCAN_CT - 1, 0, -2):
            rows1 = pl.ds(pl.multiple_of(c * crows + s * nb, 8), nb)
            rows0 = pl.ds(pl.multiple_of(c * crows + (s - 1) * nb, 8), nb)
            a1, b1 = ab_ref[rows1, :], bb_ref[rows1, :]
            a0, b0 = ab_ref[rows0, :], bb_ref[rows0, :]
            bb_ref[rows1, :] = a1 * h + b1
            h = (a0 * a1) * h + (a0 * b1 + b0)
            bb_ref[rows0, :] = h
        return h

    def emit(c):
        for s in range(nb):
            src = pl.ds(c * crows + s, SCAN_CT, stride=nb)
            dst = pl.ds(pl.multiple_of(s * nt + c * SCAN_CT, SCAN_CT), SCAN_CT)
            hsum = hf_ref[src, :] + bb_ref[src, :]
            y_ref[dst, :] = (hsum * gg_ref[dst, :].astype(F32)).astype(BF16)

    coeffs(0, 0)

    def pass1(k, h):
        c = 2 * k
        h = scan_fwd(c, 0, h)
        coeffs(c + 1, 1)
        h = scan_fwd(c + 1, 1, h)
        coeffs(jnp.minimum(c + 2, n_chunks - 1), 0)
        return h

    ht_ref[0] = lax.fori_loop(0, n_chunks // 2, pass1, h0_ref[0])

    hb = scan_bwd(n_chunks - 1, h0_ref[1])

    def pass2(k, h):
        c = n_chunks - 2 - k
        emit(c + 1)
        return scan_bwd(c, h)

    ht_ref[1] = lax.fori_loop(0, n_chunks - 1, pass2, hb)
    emit(0)


def _lru_scan_call(nb, nt, row_block0, gg, xr, h0, conv_w, conv_b, w_a, w_i, b_a, b_i, lam, y_prev=None):
    rows = nb * nt
    col = pl.BlockSpec((rows, LRU_BW), lambda n: (row_block0, n))
    vec2 = pl.BlockSpec((2, LRU_BW), lambda n: (0, n))
    wblk = pl.BlockSpec((2, None, LRU_BW, LRU_BW), lambda n: (0, n, 0, 0))
    state = pl.BlockSpec((2, nb, LRU_BW), lambda n: (0, 0, n))
    tm_rows = (nt + CONV_W - 1) * nb
    crows = SCAN_CT * nb
    aliased = y_prev is not None
    return pl.pallas_call(
        functools.partial(_lru_scan_kernel, nb, nt, aliased),
        grid=(LRU_BLOCKS,),
        in_specs=[
            col, col,
            pl.BlockSpec((CONV_W, LRU_BW), lambda n: (0, n)),
            pl.BlockSpec((1, LRU_BW), lambda n: (0, n)),
            wblk, wblk, vec2, vec2, vec2, state,
        ] + ([pl.BlockSpec(memory_space=pl.ANY)] if aliased else []),
        out_specs=[col, state],
        out_shape=[jax.ShapeDtypeStruct((N_ROWS, D), BF16), jax.ShapeDtypeStruct((2, nb, D), F32)],
        input_output_aliases={10: 0} if aliased else {},
        scratch_shapes=[pltpu.VMEM((tm_rows, LRU_BW), F32)] + [pltpu.VMEM((rows, LRU_BW), F32)] * 3
        + [pltpu.VMEM((crows, LRU_BW), F32)] * 4,
        compiler_params=_params(("arbitrary",)),
        name=f"lru_scan_{nb}x{nt}",
    )(xr, gg, conv_w, conv_b, w_a, w_i, b_a, b_i, lam, h0, *([y_prev] if aliased else []))


def kernel(x_prompt, x_sample, c, cache_k, cache_v, state_h, c_ctx, norm_g, w_mod, b_mod, attn_w_qkv, attn_w_o,
           attn_rpb, lru_w_in, lru_conv_w, lru_conv_b, lru_w_a, lru_b_a, lru_w_i, lru_b_i, lru_lam, lru_w_out,
           ffn_w_gu, ffn_w_down, final_g):
    x_lat = x_sample.reshape(N_LAT_ROWS, D)
    x_ctx = x_prompt.reshape(N_CTX_ROWS, D)
    cond = jnp.concatenate([c, c_ctx[None, :], jnp.zeros((N_COND - B_LAT - 1, D), F32)], axis=0)
    mod = _mod_call(cond, w_mod, b_mod).reshape(2, N_COND, N_MOD, 1, D)
    final_g2 = final_g.reshape(1, D)

    q, k, v, k_ctx, v_ctx, x, w_o, w_gu0, w_down0 = _qkv_call(
        x_lat, x_ctx, mod[0], norm_g[0, 0].reshape(1, D), attn_w_qkv[0].astype(BF16),
        ((attn_w_o, 0), (ffn_w_gu, 0), (ffn_w_down, 0)))
    o = _ctx_attn_call(q, k, v)
    cache_kt = jnp.transpose(cache_k[:, 0], (0, 2, 3, 1))
    cache_vt = jnp.transpose(cache_v[:, 0], (0, 2, 3, 1))
    o = _nbr_attn_call(q, k, v, cache_kt, cache_vt, _pair_table_call(attn_rpb[0]), o)
    x, w_in, w_out, w_gu1, w_down1 = _post_call(
        x, o, mod[0], norm_g[0, 1].reshape(1, D), w_o, w_gu0, w_down0, final_g2, False,
        ((lru_w_in, 0), (lru_w_out, 0), (ffn_w_gu, 1), (ffn_w_down, 1)))

    gg, xr = _lru_in_call(x, mod[1], norm_g[1, 0].reshape(1, D), w_in)
    lru_p = (lru_conv_w[0], lru_conv_b[0].reshape(1, D), lru_w_a[0], lru_w_i[0], lru_b_a[0], lru_b_i[0], lru_lam[0])
    y, _ = _lru_scan_call(B_LAT, T_LAT, 0, gg, xr, jnp.swapaxes(state_h[:, 0], 0, 1), *lru_p)
    y, h_ctx = _lru_scan_call(B_CTX, T_CTX, N_LAT_ROWS // N_CTX_ROWS, gg, xr,
                              jnp.zeros((2, B_CTX, D), F32), *lru_p, y_prev=y)
    out_lat, out_ctx = _post_call(x, y, mod[1], norm_g[1, 1].reshape(1, D), w_out, w_gu1, w_down1, final_g2, True)

    y_sample = out_lat.reshape(B_LAT, T_LAT, D)
    y_prompt = out_ctx.reshape(B_CTX, T_CTX, D)
    new_k = jnp.transpose(k_ctx, (0, 3, 1, 2))[:, None]
    new_v = jnp.transpose(v_ctx, (0, 3, 1, 2))[:, None]
    new_h = jnp.swapaxes(h_ctx, 0, 1)[:, None]
    return y_prompt, y_sample, new_k, new_v, new_h
```

```python
import functools
import math

import jax
import jax.numpy as jnp
from jax import lax
from jax.experimental import pallas as pl
from jax.experimental.pallas import tpu as pltpu

F32 = jnp.float32
BF16 = jnp.bfloat16

D = 1024
B_CTX, T_CTX = 16, 256
B_LAT, T_LAT = 8, 1024
PAST = 512
GRID_W = 64
GRID_ROWS = T_LAT // GRID_W
N_HEADS = 16
HEAD_DIM = D // N_HEADS
ATTN_SCALE = HEAD_DIM ** -0.5
LOG2E = math.log2(math.e)
WIN_ROWS, WIN_COLS = 8, 16
N_DR = 2 * WIN_ROWS - 1
N_DC = 2 * WIN_COLS - 1
LRU_BLOCKS = 8
LRU_BW = D // LRU_BLOCKS
CONV_W = 4
LRU_C = 8.0
D_FF = 2816
N_MOD = 6
RMS_EPS = 1e-6
NEG_INF = -1e30
LANES = 128

N_LAT_ROWS = B_LAT * T_LAT
N_CTX_ROWS = B_CTX * T_CTX
N_ROWS = N_LAT_ROWS + N_CTX_ROWS
N_COND = 16

TM = 512
N_LAT_TILES = N_LAT_ROWS // TM
N_CTX_TILES = N_CTX_ROWS // TM
N_TILES = N_LAT_TILES + N_CTX_TILES
TILES_PER_LAT_SAMPLE = T_LAT // TM
MOD_TN = 1536
FF_CHUNK = 256
Q_BLK_ROWS = 4
N_PAIR_TYPES = 16
PAIR_TABLE_HEADS = 4
CAST_CHUNKS = 16
SCAN_CT = 32
VMEM_LIMIT = 56 * 1024 * 1024


def _sigmoid(x):
    return 0.5 * jnp.tanh(0.5 * x) + 0.5


def _modulate(x, g, shift, scale):
    ms = jnp.mean(x * x, axis=-1, keepdims=True)
    return (x * lax.rsqrt(ms + RMS_EPS)) * g * (1.0 + scale) + shift


def _cond_row_of_tile(i):
    return jnp.where(i < N_LAT_TILES, i // TILES_PER_LAT_SAMPLE, B_LAT)


def _const_spec(shape):
    nd = len(shape)
    return pl.BlockSpec(shape, lambda *_: (0,) * nd, pipeline_mode=pl.Buffered(1))


def _params(sem):
    return pltpu.CompilerParams(dimension_semantics=sem, vmem_limit_bytes=VMEM_LIMIT)


_ROW_SPEC = pl.BlockSpec((TM, D), lambda i: (i, 0))
_LAT_SPEC = pl.BlockSpec((TM, D), lambda i: (jnp.minimum(i, N_LAT_TILES - 1), 0))
_CTX_SPEC = pl.BlockSpec((TM, D), lambda i: (jnp.maximum(i - N_LAT_TILES, 0), 0))
_MOD_SPEC = pl.BlockSpec((None, N_MOD, 1, D), lambda i: (_cond_row_of_tile(i), 0, 0, 0))
_BLOCKED_ROW_SPEC = pl.BlockSpec((LRU_BLOCKS, TM, LRU_BW), lambda i: (0, i, 0))


def _cast_specs(w, layer):
    _, rows, cols = w.shape
    chunk = lambda i: jnp.minimum(i, CAST_CHUNKS - 1)
    return (pl.BlockSpec((None, rows // CAST_CHUNKS, cols), lambda i: (layer, chunk(i), 0)),
            pl.BlockSpec((rows // CAST_CHUNKS, cols), lambda i: (chunk(i), 0)),
            jax.ShapeDtypeStruct((rows, cols), BF16))


def _cast_chunks(in_refs, out_refs):
    @pl.when(pl.program_id(0) < CAST_CHUNKS)
    def _():
        for src, dst in zip(in_refs, out_refs):
            dst[...] = src[...].astype(BF16)


def _mod_kernel(cond_ref, w_ref, b_ref, o_ref):
    c = cond_ref[...]
    s = (c * _sigmoid(c)).astype(BF16)
    o_ref[...] = jnp.dot(s, w_ref[...].astype(BF16), preferred_element_type=F32) + b_ref[...]


def _mod_call(cond, w_mod, b_mod):
    depth = w_mod.shape[0]
    n = N_MOD * D
    return pl.pallas_call(
        _mod_kernel,
        grid=(depth, n // MOD_TN),
        in_specs=[
            pl.BlockSpec((N_COND, D), lambda l, j: (0, 0)),
            pl.BlockSpec((None, D, MOD_TN), lambda l, j: (l, 0, j)),
            pl.BlockSpec((None, 1, MOD_TN), lambda l, j: (l, 0, j)),
        ],
        out_specs=pl.BlockSpec((None, N_COND, MOD_TN), lambda l, j: (l, 0, j)),
        out_shape=jax.ShapeDtypeStruct((depth, N_COND, n), F32),
        compiler_params=_params(("arbitrary", "arbitrary")),
        name="adaln_mod",
    )(cond, w_mod, b_mod.reshape(depth, 1, n))


def _pair_table_kernel(r_ref, o_ref):
    shape = (GRID_W, 2 * GRID_W)
    ck = lax.broadcasted_iota(jnp.int32, shape, 0)
    lane = lax.broadcasted_iota(jnp.int32, shape, 1)
    cq = jnp.bitwise_and(lane, GRID_W - 1)
    cs = jnp.clip(cq - WIN_COLS // 2, 0, GRID_W - WIN_COLS)
    in_win = jnp.logical_and(ck >= cs, ck < cs + WIN_COLS)

    def toeplitz(h, d, lane0):
        base = jnp.broadcast_to(r_ref[h, d:d + 1, :], shape)
        return pltpu.roll(base, (lane0 - (WIN_COLS - 1)) % LANES, 1, stride=1, stride_axis=0)

    neg = jnp.full(shape, NEG_INF, F32)
    for h in range(PAIR_TABLE_HEADS):
        for t in range(N_PAIR_TYPES):
            d_left, d_right = (t + 1, t) if t < 14 else ((3, None) if t == 14 else (None, 10))
            left = neg if d_left is None else toeplitz(h, d_left, 0)
            right = neg if d_right is None else toeplitz(h, d_right, GRID_W)
            val = jnp.where(lane < GRID_W, left, right)
            o_ref[h, t] = jnp.where(in_win, val * LOG2E, NEG_INF)


def _pair_table_call(rpb):
    assert 2 * GRID_W == LANES
    r = jnp.pad(rpb[:, :, ::-1], ((0, 0), (0, 0), (0, LANES - N_DC)))
    return pl.pallas_call(
        _pair_table_kernel,
        grid=(N_HEADS // PAIR_TABLE_HEADS,),
        in_specs=[pl.BlockSpec((PAIR_TABLE_HEADS, N_DR, LANES), lambda h: (h, 0, 0))],
        out_specs=pl.BlockSpec((PAIR_TABLE_HEADS, N_PAIR_TYPES, GRID_W, LANES), lambda h: (h, 0, 0, 0)),
        out_shape=jax.ShapeDtypeStruct((N_HEADS, N_PAIR_TYPES, GRID_W, LANES), F32),
        compiler_params=_params(("arbitrary",)),
        name="rpb_table",
    )(r)


def _row_window_start(r):
    return min(max(r - WIN_ROWS // 2, 0), GRID_ROWS - WIN_ROWS)


def _pair_type(kr, r0):
    in0 = _row_window_start(r0) <= kr < _row_window_start(r0) + WIN_ROWS
    in1 = _row_window_start(r0 + 1) <= kr < _row_window_start(r0 + 1) + WIN_ROWS
    if in0 and in1:
        dl = kr - r0
        assert -6 <= dl <= 7
        return dl + 6
    if in0:
        assert kr - r0 == -(WIN_ROWS // 2)
        return 14
    if in1:
        assert kr - (r0 + 1) == WIN_ROWS // 2 - 1
        return 15
    return None


def _qkv_kernel(n_cast, xl_ref, xc_ref, m_ref, g_ref, w_ref, *refs):
    cast_in = refs[:n_cast]
    q_ref, k_ref, v_ref, kf_ref, vf_ref, x_ref = refs[n_cast:n_cast + 6]
    cast_out = refs[n_cast + 6:]
    _cast_chunks(cast_in, cast_out)
    x = jnp.where(pl.program_id(0) < N_LAT_TILES, xl_ref[...], xc_ref[...])
    x_ref[...] = x
    xm = _modulate(x, g_ref[...], m_ref[0], m_ref[1])
    qkv = jnp.dot(xm.astype(BF16), w_ref[...], preferred_element_type=F32)
    q_ref[...] = (qkv[:, :D] * (ATTN_SCALE * LOG2E)).astype(BF16)
    k = qkv[:, D:2 * D]
    v = qkv[:, 2 * D:]
    k_ref[...] = k.astype(BF16)
    v_ref[...] = v.astype(BF16)

    @pl.when(pl.program_id(0) >= N_LAT_TILES)
    def _():
        for src, dst in ((k, kf_ref), (v, vf_ref)):
            t = src.T.reshape(N_HEADS, HEAD_DIM, TM)
            for s in range(TM // T_CTX):
                dst[s] = t[:, :, s * T_CTX:(s + 1) * T_CTX]


def _qkv_call(x_lat, x_ctx, mod, g, w_qkv, cast_weights):
    cast_in, cast_out, cast_shapes = zip(*[_cast_specs(w, l) for w, l in cast_weights])
    new_cache = pl.BlockSpec((TM // T_CTX, N_HEADS, HEAD_DIM, T_CTX),
                             lambda i: (jnp.maximum(i - N_LAT_TILES, 0), 0, 0, 0))
    return pl.pallas_call(
        functools.partial(_qkv_kernel, len(cast_weights)),
        grid=(N_TILES,),
        in_specs=[_LAT_SPEC, _CTX_SPEC, _MOD_SPEC, _const_spec((1, D)), _const_spec((D, 3 * D)), *cast_in],
        out_specs=[_ROW_SPEC, _ROW_SPEC, _ROW_SPEC, new_cache, new_cache, _ROW_SPEC, *cast_out],
        out_shape=[jax.ShapeDtypeStruct((N_ROWS, D), BF16)] * 3
        + [jax.ShapeDtypeStruct((B_CTX, N_HEADS, HEAD_DIM, T_CTX), F32)] * 2
        + [jax.ShapeDtypeStruct((N_ROWS, D), F32)] + list(cast_shapes),
        compiler_params=_params(("arbitrary",)),
        name="qkv_proj",
    )(x_lat, x_ctx, mod, g, w_qkv, *[w for w, _ in cast_weights])


ATTN_SLOTS = 3


def _kq(k, q):
    return lax.dot_general(k, q, (((1,), (1,)), ((), ())), preferred_element_type=F32)


def _attn_pipeline(n_units, scores_fn, probs_fn, out_fn):
    assert ATTN_SLOTS == 3
    col_max = {n: scores_fn(n) for n in range(min(2, n_units))}
    col_sum = {}
    outs = []
    for n in range(n_units):
        if n + 2 < n_units:
            col_max[n + 2] = scores_fn(n + 2)
        col_sum[n] = probs_fn(n, col_max[n])
        if n >= 1:
            outs.append(out_fn(n - 1, col_sum[n - 1]))
    outs.append(out_fn(n_units - 1, col_sum[n_units - 1]))
    return outs


def _store_scores(s_ref, slot, row0, s):
    s_ref[slot, row0:row0 + s.shape[0], :] = s
    return s.max(axis=0, keepdims=True)


def _store_probs(s_ref, p_ref, slot, row0, rows, m):
    p = jnp.exp2(s_ref[slot, row0:row0 + rows, :] - m)
    p_ref[slot, row0:row0 + rows, :] = p.astype(BF16)
    return p.sum(axis=0, keepdims=True)


def _head_masks(n_q):
    lane = lax.broadcasted_iota(jnp.int32, (n_q, 2 * HEAD_DIM), 1)
    return lane < HEAD_DIM, lane >= HEAD_DIM


def _ctx_attn_kernel(q_ref, k_ref, v_ref, o_ref, s_ref, p_ref):
    v_t = v_ref[...].T
    masks = _head_masks(T_CTX)

    def scores(h):
        cols = slice((h // 2) * 2 * HEAD_DIM, (h // 2 + 1) * 2 * HEAD_DIM)
        q = q_ref[:, cols]
        qm = jnp.where(masks[h % 2], q, jnp.zeros_like(q))
        return _store_scores(s_ref, h % ATTN_SLOTS, 0, _kq(k_ref[:, cols], qm))

    def probs(h, m):
        return _store_probs(s_ref, p_ref, h % ATTN_SLOTS, 0, T_CTX, m)

    def out(h, l):
        o = jnp.dot(v_t[h * HEAD_DIM:(h + 1) * HEAD_DIM, :], p_ref[h % ATTN_SLOTS], preferred_element_type=F32)
        return o / l

    outs = _attn_pipeline(N_HEADS, scores, probs, out)
    o_ref[...] = jnp.concatenate(outs, axis=0).T.astype(BF16)


def _ctx_attn_call(q, k, v):
    blk = pl.BlockSpec((T_CTX, D), lambda b: (N_LAT_ROWS // T_CTX + b, 0))
    return pl.pallas_call(
        _ctx_attn_kernel,
        grid=(B_CTX,),
        in_specs=[blk, blk, blk],
        out_specs=blk,
        out_shape=jax.ShapeDtypeStruct((N_ROWS, D), BF16),
        scratch_shapes=[pltpu.VMEM((ATTN_SLOTS, T_CTX, T_CTX), F32), pltpu.VMEM((ATTN_SLOTS, T_CTX, T_CTX), BF16)],
        compiler_params=_params(("arbitrary",)),
        name="ctx_attn",
    )(q, k, v)


def _key_rows_of_block(i):
    r_first = Q_BLK_ROWS * i
    return _row_window_start(r_first), _row_window_start(r_first + Q_BLK_ROWS - 1) + WIN_ROWS


N_Q_BLOCKS = GRID_ROWS // Q_BLK_ROWS
MAX_LAT_KEYS = max(kr1 - kr0 for kr0, kr1 in map(_key_rows_of_block, range(N_Q_BLOCKS))) * GRID_W


NBR_HEADS = 8


def _nbr_attn_kernel(q_ref, k_ref, v_ref, ckt_ref, cvt_ref, pt_ref, o_in_ref, o_ref, s_ref, p_ref):
    del o_in_ref
    pw = 2 * HEAD_DIM
    v_t = v_ref[...].T
    ck = ckt_ref[...].reshape(NBR_HEADS * HEAD_DIM, PAST).astype(BF16).T
    cv_t = cvt_ref[...].reshape(NBR_HEADS * HEAD_DIM, PAST).astype(BF16)
    n_q = Q_BLK_ROWS * GRID_W
    masks = _head_masks(n_q)
    neg_blk = jnp.full((GRID_W, 2 * GRID_W), NEG_INF, F32)

    def unit(n):
        h, i = divmod(n, N_Q_BLOCKS)
        return (h, i, n % ATTN_SLOTS) + _key_rows_of_block(i)

    def scores(n):
        h, i, slot, kr0, kr1 = unit(n)
        cols = slice((h // 2) * pw, (h // 2 + 1) * pw)
        r_first = Q_BLK_ROWS * i
        q = q_ref[r_first * GRID_W:(r_first + Q_BLK_ROWS) * GRID_W, cols]
        qm = jnp.where(masks[h % 2], q, jnp.zeros_like(q))
        bias_rows = []
        for kr in range(kr0, kr1):
            blks = []
            for r0 in range(r_first, r_first + Q_BLK_ROWS, 2):
                typ = _pair_type(kr, r0)
                blks.append(neg_blk if typ is None else pt_ref[h, typ])
            bias_rows.append(jnp.concatenate(blks, axis=1))
        s_lat = _kq(k_ref[kr0 * GRID_W:kr1 * GRID_W, cols], qm) + jnp.concatenate(bias_rows, axis=0)
        m_lat = _store_scores(s_ref, slot, 0, s_lat)
        m_ctx = _store_scores(s_ref, slot, MAX_LAT_KEYS, _kq(ck[:, cols], qm))
        return jnp.maximum(m_lat, m_ctx)

    def probs(n, m):
        _, _, slot, kr0, kr1 = unit(n)
        return (_store_probs(s_ref, p_ref, slot, 0, (kr1 - kr0) * GRID_W, m)
                + _store_probs(s_ref, p_ref, slot, MAX_LAT_KEYS, PAST, m))

    def out(n, l):
        h, _, slot, kr0, kr1 = unit(n)
        hs = slice(h * HEAD_DIM, (h + 1) * HEAD_DIM)
        lk = (kr1 - kr0) * GRID_W
        o = (jnp.dot(v_t[hs, kr0 * GRID_W:kr1 * GRID_W], p_ref[slot, 0:lk, :], preferred_element_type=F32)
             + jnp.dot(cv_t[hs, :], p_ref[slot, MAX_LAT_KEYS:MAX_LAT_KEYS + PAST, :], preferred_element_type=F32))
        return o / l

    outs = _attn_pipeline(NBR_HEADS * N_Q_BLOCKS, scores, probs, out)
    o_t = jnp.concatenate([jnp.concatenate(outs[h * N_Q_BLOCKS:(h + 1) * N_Q_BLOCKS], axis=1)
                           for h in range(NBR_HEADS)], axis=0)
    o_ref[...] = o_t.T.astype(BF16)


def _nbr_attn_call(q, k, v, cache_k, cache_v, pair_table, o_ctx):
    assert NBR_HEADS % 2 == 0
    blk = pl.BlockSpec((T_LAT, NBR_HEADS * HEAD_DIM), lambda b, g: (b, g))
    cache = pl.BlockSpec((None, NBR_HEADS, HEAD_DIM, PAST), lambda b, g: (b, g, 0, 0))
    return pl.pallas_call(
        _nbr_attn_kernel,
        grid=(B_LAT, N_HEADS // NBR_HEADS),
        in_specs=[blk, blk, blk, cache, cache,
                  pl.BlockSpec((NBR_HEADS, N_PAIR_TYPES, GRID_W, 2 * GRID_W), lambda b, g: (g, 0, 0, 0)),
                  pl.BlockSpec(memory_space=pl.ANY)],
        out_specs=blk,
        out_shape=jax.ShapeDtypeStruct((N_ROWS, D), BF16),
        input_output_aliases={6: 0},
        scratch_shapes=[pltpu.VMEM((ATTN_SLOTS, MAX_LAT_KEYS + PAST, Q_BLK_ROWS * GRID_W), F32),
                        pltpu.VMEM((ATTN_SLOTS, MAX_LAT_KEYS + PAST, Q_BLK_ROWS * GRID_W), BF16)],
        compiler_params=_params(("arbitrary", "arbitrary")),
        name="nbr_attn",
    )(q, k, v, cache_k, cache_v, pair_table, o_ctx)


def _post_kernel(final_norm, o_blocked, n_cast, x_ref, o_ref, m_ref, g_ref, wo_ref, wgu_ref, wd_ref, fg_ref, *refs):
    n_out = 2 if final_norm else 1
    cast_in = refs[:n_cast]
    out_refs = refs[n_cast:n_cast + n_out]
    cast_out = refs[n_cast + n_out:2 * n_cast + n_out]
    acc_ref, x1_ref = refs[2 * n_cast + n_out:]

    _cast_chunks(cast_in, cast_out)
    o = jnp.concatenate([o_ref[n] for n in range(LRU_BLOCKS)], axis=1) if o_blocked else o_ref[...]
    mix = jnp.dot(o, wo_ref[...], preferred_element_type=F32)
    x1 = x_ref[...] + m_ref[2] * mix
    x1_ref[...] = x1
    h = _modulate(x1, g_ref[...], m_ref[3], m_ref[4]).astype(BF16)
    for c in range(D_FF // FF_CHUNK):
        cs = slice(c * FF_CHUNK, (c + 1) * FF_CHUNK)
        us = slice(D_FF + c * FF_CHUNK, D_FF + (c + 1) * FF_CHUNK)
        gate = jnp.dot(h, wgu_ref[:, cs], preferred_element_type=F32)
        up = jnp.dot(h, wgu_ref[:, us], preferred_element_type=F32)
        a = (gate * _sigmoid(gate) * up).astype(BF16)
        part = jnp.dot(a, wd_ref[cs, :], preferred_element_type=F32)
        if c == 0:
            acc_ref[...] = part
        else:
            acc_ref[...] += part
    def finish(out_ref):
        x2 = x1_ref[...] + m_ref[5] * acc_ref[...]
        if final_norm:
            ms = jnp.mean(x2 * x2, axis=-1, keepdims=True)
            x2 = x2 * lax.rsqrt(ms + RMS_EPS) * fg_ref[...]
        out_ref[...] = x2

    if not final_norm:
        finish(out_refs[0])
        return
    finish(x1_ref)
    is_lat = pl.program_id(0) < N_LAT_TILES

    @pl.when(is_lat)
    def _():
        out_refs[0][...] = x1_ref[...]

    @pl.when(jnp.logical_not(is_lat))
    def _():
        out_refs[1][...] = x1_ref[...]


def _post_call(x, o, mod, g, w_o, w_gu, w_down, final_g, final_norm, cast_weights=()):
    o_blocked = o.ndim == 3
    cast_in, cast_out, cast_shapes = (zip(*[_cast_specs(w, l) for w, l in cast_weights])
                                      if cast_weights else ((), (), ()))
    if final_norm:
        out_specs = [_LAT_SPEC, _CTX_SPEC]
        out_shape = [jax.ShapeDtypeStruct((N_LAT_ROWS, D), F32), jax.ShapeDtypeStruct((N_CTX_ROWS, D), F32)]
    else:
        out_specs = [_ROW_SPEC]
        out_shape = [jax.ShapeDtypeStruct((N_ROWS, D), F32)]
    return pl.pallas_call(
        functools.partial(_post_kernel, final_norm, o_blocked, len(cast_weights)),
        grid=(N_TILES,),
        in_specs=[
            _ROW_SPEC, _BLOCKED_ROW_SPEC if o_blocked else _ROW_SPEC, _MOD_SPEC,
            _const_spec((1, D)),
            _const_spec((D, D)),
            _const_spec((D, 2 * D_FF)),
            _const_spec((D_FF, D)),
            _const_spec((1, D)),
            *cast_in,
        ],
        out_specs=[*out_specs, *cast_out],
        out_shape=[*out_shape, *cast_shapes],
        scratch_shapes=[pltpu.VMEM((TM, D), F32), pltpu.VMEM((TM, D), F32)],
        compiler_params=_params(("arbitrary",)),
        name="mix_out_ffn",
    )(x, o, mod, g, w_o, w_gu, w_down, final_g, *[w for w, _ in cast_weights])


def _lru_in_kernel(x_ref, m_ref, g_ref, w_ref, gate_ref, xr_ref):
    xm = _modulate(x_ref[...], g_ref[...], m_ref[0], m_ref[1])
    y = jnp.dot(xm.astype(BF16), w_ref[...], preferred_element_type=F32)
    for n in range(LRU_BLOCKS):
        cols = slice(n * LRU_BW, (n + 1) * LRU_BW)
        gate_ref[n] = jax.nn.gelu(y[:, cols]).astype(BF16)
        xr_ref[n] = y[:, D + n * LRU_BW:D + (n + 1) * LRU_BW]


def _lru_in_call(x, mod, g, w_in):
    return pl.pallas_call(
        _lru_in_kernel,
        grid=(N_TILES,),
        in_specs=[_ROW_SPEC, _MOD_SPEC, _const_spec((1, D)), _const_spec((D, 2 * D))],
        out_specs=[_BLOCKED_ROW_SPEC, _BLOCKED_ROW_SPEC],
        out_shape=[jax.ShapeDtypeStruct((LRU_BLOCKS, N_ROWS, LRU_BW), BF16),
                   jax.ShapeDtypeStruct((LRU_BLOCKS, N_ROWS, LRU_BW), F32)],
        compiler_params=_params(("arbitrary",)),
        name="lru_in_proj",
    )(x, mod, g, w_in)


def _lru_scan_kernel(nb, nt, aliased, xr_ref, gg_ref, cw_ref, cb_ref, wa_ref, wi_ref, ba_ref, bi_ref, lam_ref,
                     h0_ref, *refs):
    y_ref, ht_ref, xt_ref, hf_ref, ab_ref, bb_ref, af0, bf0, af1, bf1 = refs[1:] if aliased else refs
    fwd_slots = ((af0, bf0), (af1, bf1))
    left = CONV_W // 2
    pad_rows = (CONV_W - 1) * nb
    xt_ref[pl.ds(0, left * nb), :] = jnp.zeros((left * nb, LRU_BW), F32)
    xt_ref[pl.ds((left + nt) * nb, pad_rows - left * nb), :] = jnp.zeros((pad_rows - left * nb, LRU_BW), F32)
    for s in range(nb):
        xt_ref[pl.ds(left * nb + s, nt, stride=nb), :] = xr_ref[pl.ds(s * nt, nt), :]

    cw = cw_ref[...]
    cb = cb_ref[...]
    w4 = (jnp.concatenate([wa_ref[0], wi_ref[0], wa_ref[1], wi_ref[1]], axis=1) * 0.5).astype(BF16)
    half_ba = 0.5 * ba_ref[...]
    half_bi = 0.5 * bi_ref[...]
    neg_lam = -lam_ref[...]
    sp = jnp.maximum(neg_lam, 0.0) + jnp.log1p(jnp.exp(-jnp.abs(neg_lam)))
    c1 = (-0.5 * LRU_C * LOG2E) * sp
    crows = SCAN_CT * nb
    n_chunks = nt // SCAN_CT
    assert n_chunks % 2 == 0

    def chunk_rows(c):
        return pl.ds(pl.multiple_of(c * crows, 8), crows)

    def coeffs(c, slot):
        t0 = c * SCAN_CT
        xc = cb
        for j in range(CONV_W):
            xc = xc + xt_ref[pl.ds(pl.multiple_of((t0 + j) * nb, 8), crows), :] * cw[j:j + 1]
        half_xc = 0.5 * xc
        pre = jnp.dot(xc.astype(BF16), w4, preferred_element_type=F32)

        def direction(d):
            o = 2 * d * LRU_BW
            t_r = jnp.tanh(pre[:, o:o + LRU_BW] + half_ba[d:d + 1])
            a = jnp.exp2(c1[d:d + 1] * t_r + c1[d:d + 1])
            t_i = jnp.tanh(pre[:, o + LRU_BW:o + 2 * LRU_BW] + half_bi[d:d + 1])
            y = 1.0 - a * a
            root = jnp.where(y > 0.0, y * lax.rsqrt(y), 0.0)
            return a, root * (half_xc * t_i + half_xc)

        a_ref, b_ref = fwd_slots[slot]
        a_ref[...], b_ref[...] = direction(0)
        ab_ref[chunk_rows(c), :], bb_ref[chunk_rows(c), :] = direction(1)

    def scan_fwd(c, slot, h):
        a_ref, b_ref = fwd_slots[slot]
        for s in range(SCAN_CT):
            rows = slice(s * nb, (s + 1) * nb)
            h = a_ref[rows, :] * h + b_ref[rows, :]
            hf_ref[pl.ds(pl.multiple_of(c * crows + s * nb, 8), nb), :] = h
        return h

    def scan_bwd(c, h):
        for s in range(SCAN_CT - 1, 0, -2):
            rows1 = pl.ds(pl.multiple_of(c * crows + s * nb, 8), nb)
            rows0 = pl.ds(pl.multiple_of(c * crows + (s - 1) * nb, 8), nb)
            a1, b1 = ab_ref[rows1, :], bb_ref[rows1, :]
            a0, b0 = ab_ref[rows0, :], bb_ref[rows0, :]
            bb_ref[rows1, :] = a1 * h + b1
            h = (a0 * a1) * h + (a0 * b1 + b0)
            bb_ref[rows0, :] = h
        return h

    def emit(c):
        for s in range(nb):
            src = pl.ds(c * crows + s, SCAN_CT, stride=nb)
            dst = pl.ds(pl.multiple_of(s * nt + c * SCAN_CT, SCAN_CT), SCAN_CT)
            hsum = hf_ref[src, :] + bb_ref[src, :]
            y_ref[dst, :] = (hsum * gg_ref[dst, :].astype(F32)).astype(BF16)

    coeffs(0, 0)

    def pass1(k, h):
        c = 2 * k
        h = scan_fwd(c, 0, h)
        coeffs(c + 1, 1)
        h = scan_fwd(c + 1, 1, h)
        coeffs(jnp.minimum(c + 2, n_chunks - 1), 0)
        return h

    ht_ref[0] = lax.fori_loop(0, n_chunks // 2, pass1, h0_ref[0])

    hb = scan_bwd(n_chunks - 1, h0_ref[1])

    def pass2(k, h):
        c = n_chunks - 2 - k
        emit(c + 1)
        return scan_bwd(c, h)

    ht_ref[1] = lax.fori_loop(0, n_chunks - 1, pass2, hb)
    emit(0)


def _lru_scan_call(nb, nt, row_block0, gg, xr, h0, conv_w, conv_b, w_a, w_i, b_a, b_i, lam, y_prev=None):
    rows = nb * nt
    col = pl.BlockSpec((None, rows, LRU_BW), lambda n: (n, row_block0, 0))
    vec2 = pl.BlockSpec((2, LRU_BW), lambda n: (0, n))
    wblk = pl.BlockSpec((2, None, LRU_BW, LRU_BW), lambda n: (0, n, 0, 0))
    state = pl.BlockSpec((2, nb, LRU_BW), lambda n: (0, 0, n))
    tm_rows = (nt + CONV_W - 1) * nb
    crows = SCAN_CT * nb
    aliased = y_prev is not None
    return pl.pallas_call(
        functools.partial(_lru_scan_kernel, nb, nt, aliased),
        grid=(LRU_BLOCKS,),
        in_specs=[
            col, col,
            pl.BlockSpec((CONV_W, LRU_BW), lambda n: (0, n)),
            pl.BlockSpec((1, LRU_BW), lambda n: (0, n)),
            wblk, wblk, vec2, vec2, vec2, state,
        ] + ([pl.BlockSpec(memory_space=pl.ANY)] if aliased else []),
        out_specs=[col, state],
        out_shape=[jax.ShapeDtypeStruct((LRU_BLOCKS, N_ROWS, LRU_BW), BF16),
                   jax.ShapeDtypeStruct((2, nb, D), F32)],
        input_output_aliases={10: 0} if aliased else {},
        scratch_shapes=[pltpu.VMEM((tm_rows, LRU_BW), F32)] + [pltpu.VMEM((rows, LRU_BW), F32)] * 3
        + [pltpu.VMEM((crows, LRU_BW), F32)] * 4,
        compiler_params=_params(("arbitrary",)),
        name=f"lru_scan_{nb}x{nt}",
    )(xr, gg, conv_w, conv_b, w_a, w_i, b_a, b_i, lam, h0, *([y_prev] if aliased else []))


def kernel(x_prompt, x_sample, c, cache_k, cache_v, state_h, c_ctx, norm_g, w_mod, b_mod, attn_w_qkv, attn_w_o,
           attn_rpb, lru_w_in, lru_conv_w, lru_conv_b, lru_w_a, lru_b_a, lru_w_i, lru_b_i, lru_lam, lru_w_out,
           ffn_w_gu, ffn_w_down, final_g):
    x_lat = x_sample.reshape(N_LAT_ROWS, D)
    x_ctx = x_prompt.reshape(N_CTX_ROWS, D)
    cond = jnp.concatenate([c, c_ctx[None, :], jnp.zeros((N_COND - B_LAT - 1, D), F32)], axis=0)
    mod = _mod_call(cond, w_mod, b_mod).reshape(2, N_COND, N_MOD, 1, D)
    final_g2 = final_g.reshape(1, D)

    q, k, v, k_ctx, v_ctx, x, w_o, w_gu0, w_down0 = _qkv_call(
        x_lat, x_ctx, mod[0], norm_g[0, 0].reshape(1, D), attn_w_qkv[0].astype(BF16),
        ((attn_w_o, 0), (ffn_w_gu, 0), (ffn_w_down, 0)))
    o = _ctx_attn_call(q, k, v)
    cache_kt = jnp.transpose(cache_k[:, 0], (0, 2, 3, 1))
    cache_vt = jnp.transpose(cache_v[:, 0], (0, 2, 3, 1))
    o = _nbr_attn_call(q, k, v, cache_kt, cache_vt, _pair_table_call(attn_rpb[0]), o)
    x, w_in, w_out, w_gu1, w_down1 = _post_call(
        x, o, mod[0], norm_g[0, 1].reshape(1, D), w_o, w_gu0, w_down0, final_g2, False,
        ((lru_w_in, 0), (lru_w_out, 0), (ffn_w_gu, 1), (ffn_w_down, 1)))

    gg, xr = _lru_in_call(x, mod[1], norm_g[1, 0].reshape(1, D), w_in)
    lru_p = (lru_conv_w[0], lru_conv_b[0].reshape(1, D), lru_w_a[0], lru_w_i[0], lru_b_a[0], lru_b_i[0], lru_lam[0])
    y, _ = _lru_scan_call(B_LAT, T_LAT, 0, gg, xr, jnp.swapaxes(state_h[:, 0], 0, 1), *lru_p)
    y, h_ctx = _lru_scan_call(B_CTX, T_CTX, N_LAT_ROWS // N_CTX_ROWS, gg, xr,
                              jnp.zeros((2, B_CTX, D), F32), *lru_p, y_prev=y)
    out_lat, out_ctx = _post_call(x, y, mod[1], norm_g[1, 1].reshape(1, D), w_out, w_gu1, w_down1, final_g2, True)

    y_sample = out_lat.reshape(B_LAT, T_LAT, D)
    y_prompt = out_ctx.reshape(B_CTX, T_CTX, D)
    new_k = jnp.transpose(k_ctx, (0, 3, 1, 2))[:, None]
    new_v = jnp.transpose(v_ctx, (0, 3, 1, 2))[:, None]
    new_h = jnp.swapaxes(h_ctx, 0, 1)[:, None]
    return y_prompt, y_sample, new_k, new_v, new_h
```

```python
import functools
import math

import jax
import jax.numpy as jnp
from jax import lax
from jax.experimental import pallas as pl
from jax.experimental.pallas import tpu as pltpu

F32 = jnp.float32
BF16 = jnp.bfloat16

D = 1024
B_CTX, T_CTX = 16, 256
B_LAT, T_LAT = 8, 1024
PAST = 512
GRID_W = 64
GRID_ROWS = T_LAT // GRID_W
N_HEADS = 16
HEAD_DIM = D // N_HEADS
ATTN_SCALE = HEAD_DIM ** -0.5
LOG2E = math.log2(math.e)
WIN_ROWS, WIN_COLS = 8, 16
N_DR = 2 * WIN_ROWS - 1
N_DC = 2 * WIN_COLS - 1
LRU_BLOCKS = 8
LRU_BW = D // LRU_BLOCKS
CONV_W = 4
LRU_C = 8.0
D_FF = 2816
N_MOD = 6
RMS_EPS = 1e-6
NEG_INF = -1e30
LANES = 128

N_LAT_ROWS = B_LAT * T_LAT
N_CTX_ROWS = B_CTX * T_CTX
N_ROWS = N_LAT_ROWS + N_CTX_ROWS
N_COND = 16

TM = 512
N_LAT_TILES = N_LAT_ROWS // TM
N_CTX_TILES = N_CTX_ROWS // TM
N_TILES = N_LAT_TILES + N_CTX_TILES
TILES_PER_LAT_SAMPLE = T_LAT // TM
MOD_TN = 1536
FF_CHUNK = 256
Q_BLK_ROWS = 4
N_PAIR_TYPES = 16
CAST_CHUNKS = 16
SCAN_CT = 64
VMEM_LIMIT = 56 * 1024 * 1024


def _sigmoid(x):
    return 0.5 * jnp.tanh(0.5 * x) + 0.5


def _modulate(x, g, shift, scale):
    ms = jnp.mean(x * x, axis=-1, keepdims=True)
    return (x * lax.rsqrt(ms + RMS_EPS)) * g * (1.0 + scale) + shift


def _cond_row_of_tile(i):
    return jnp.where(i < N_LAT_TILES, i // TILES_PER_LAT_SAMPLE, B_LAT)


def _const_spec(shape):
    nd = len(shape)
    return pl.BlockSpec(shape, lambda *_: (0,) * nd, pipeline_mode=pl.Buffered(1))


def _params(sem):
    return pltpu.CompilerParams(dimension_semantics=sem, vmem_limit_bytes=VMEM_LIMIT)


_ROW_SPEC = pl.BlockSpec((TM, D), lambda i: (i, 0))
_LAT_SPEC = pl.BlockSpec((TM, D), lambda i: (jnp.minimum(i, N_LAT_TILES - 1), 0))
_CTX_SPEC = pl.BlockSpec((TM, D), lambda i: (jnp.maximum(i - N_LAT_TILES, 0), 0))
_MOD_SPEC = pl.BlockSpec((None, N_MOD, 1, D), lambda i: (_cond_row_of_tile(i), 0, 0, 0))


def _cast_specs(w, layer):
    _, rows, cols = w.shape
    chunk = lambda i: jnp.minimum(i, CAST_CHUNKS - 1)
    return (pl.BlockSpec((None, rows // CAST_CHUNKS, cols), lambda i: (layer, chunk(i), 0)),
            pl.BlockSpec((rows // CAST_CHUNKS, cols), lambda i: (chunk(i), 0)),
            jax.ShapeDtypeStruct((rows, cols), BF16))


def _cast_chunks(in_refs, out_refs):
    @pl.when(pl.program_id(0) < CAST_CHUNKS)
    def _():
        for src, dst in zip(in_refs, out_refs):
            dst[...] = src[...].astype(BF16)


def _pair_table_blocks(r_ref, o_ref):
    shape = (GRID_W, 2 * GRID_W)
    ck = lax.broadcasted_iota(jnp.int32, shape, 0)
    lane = lax.broadcasted_iota(jnp.int32, shape, 1)
    cq = jnp.bitwise_and(lane, GRID_W - 1)
    cs = jnp.clip(cq - WIN_COLS // 2, 0, GRID_W - WIN_COLS)
    in_win = jnp.logical_and(ck >= cs, ck < cs + WIN_COLS)

    def toeplitz(h, d, lane0):
        base = jnp.broadcast_to(r_ref[h, d:d + 1, :], shape)
        return pltpu.roll(base, (lane0 - (WIN_COLS - 1)) % LANES, 1, stride=1, stride_axis=0)

    neg = jnp.full(shape, NEG_INF, F32)
    for h in range(r_ref.shape[0]):
        for t in range(N_PAIR_TYPES):
            d_left, d_right = (t + 1, t) if t < 14 else ((3, None) if t == 14 else (None, 10))
            left = neg if d_left is None else toeplitz(h, d_left, 0)
            right = neg if d_right is None else toeplitz(h, d_right, GRID_W)
            val = jnp.where(lane < GRID_W, left, right)
            o_ref[h, t] = jnp.where(in_win, val * LOG2E, NEG_INF)


def _prologue_kernel(cond_ref, w_ref, b_ref, r_ref, wqkv_ref, mod_ref, pt_ref, wq_ref):
    c = cond_ref[...]
    s = (c * _sigmoid(c)).astype(BF16)
    mod_ref[...] = jnp.dot(s, w_ref[...].astype(BF16), preferred_element_type=F32) + b_ref[...]
    _pair_table_blocks(r_ref, pt_ref)
    wq_ref[...] = wqkv_ref[...].astype(BF16)


def _prologue_call(cond, w_mod, b_mod, rpb, w_qkv):
    assert 2 * GRID_W == LANES
    depth = w_mod.shape[0]
    n = N_MOD * D
    cols_steps = n // MOD_TN
    steps = depth * cols_steps
    heads = N_HEADS // steps
    w_rows = D // steps
    step = lambda l, j: l * cols_steps + j
    r = jnp.pad(rpb[:, :, ::-1], ((0, 0), (0, 0), (0, LANES - N_DC)))
    return pl.pallas_call(
        _prologue_kernel,
        grid=(depth, cols_steps),
        in_specs=[
            pl.BlockSpec((N_COND, D), lambda l, j: (0, 0)),
            pl.BlockSpec((None, D, MOD_TN), lambda l, j: (l, 0, j)),
            pl.BlockSpec((None, 1, MOD_TN), lambda l, j: (l, 0, j)),
            pl.BlockSpec((heads, N_DR, LANES), lambda l, j: (step(l, j), 0, 0)),
            pl.BlockSpec((None, w_rows, 3 * D), lambda l, j: (0, step(l, j), 0)),
        ],
        out_specs=[
            pl.BlockSpec((None, N_COND, MOD_TN), lambda l, j: (l, 0, j)),
            pl.BlockSpec((heads, N_PAIR_TYPES, GRID_W, LANES), lambda l, j: (step(l, j), 0, 0, 0)),
            pl.BlockSpec((w_rows, 3 * D), lambda l, j: (step(l, j), 0)),
        ],
        out_shape=[jax.ShapeDtypeStruct((depth, N_COND, n), F32),
                   jax.ShapeDtypeStruct((N_HEADS, N_PAIR_TYPES, GRID_W, LANES), F32),
                   jax.ShapeDtypeStruct((D, 3 * D), BF16)],
        compiler_params=_params(("arbitrary", "arbitrary")),
        name="prologue",
    )(cond, w_mod, b_mod.reshape(depth, 1, n), r, w_qkv)


def _row_window_start(r):
    return min(max(r - WIN_ROWS // 2, 0), GRID_ROWS - WIN_ROWS)


def _pair_type(kr, r0):
    in0 = _row_window_start(r0) <= kr < _row_window_start(r0) + WIN_ROWS
    in1 = _row_window_start(r0 + 1) <= kr < _row_window_start(r0 + 1) + WIN_ROWS
    if in0 and in1:
        dl = kr - r0
        assert -6 <= dl <= 7
        return dl + 6
    if in0:
        assert kr - r0 == -(WIN_ROWS // 2)
        return 14
    if in1:
        assert kr - (r0 + 1) == WIN_ROWS // 2 - 1
        return 15
    return None


def _qkv_kernel(n_cast, xl_ref, xc_ref, m_ref, g_ref, w_ref, *refs):
    cast_in = refs[:n_cast]
    q_ref, k_ref, v_ref, kf_ref, vf_ref, x_ref = refs[n_cast:n_cast + 6]
    cast_out = refs[n_cast + 6:]
    _cast_chunks(cast_in, cast_out)
    x = jnp.where(pl.program_id(0) < N_LAT_TILES, xl_ref[...], xc_ref[...])
    x_ref[...] = x
    xm = _modulate(x, g_ref[...], m_ref[0], m_ref[1])
    qkv = jnp.dot(xm.astype(BF16), w_ref[...], preferred_element_type=F32)
    q_ref[...] = (qkv[:, :D] * (ATTN_SCALE * LOG2E)).astype(BF16)
    k = qkv[:, D:2 * D]
    v = qkv[:, 2 * D:]
    k_ref[...] = k.astype(BF16)
    v_ref[...] = v.astype(BF16)

    @pl.when(pl.program_id(0) >= N_LAT_TILES)
    def _():
        for src, dst in ((k, kf_ref), (v, vf_ref)):
            t = src.T.reshape(N_HEADS, HEAD_DIM, TM)
            for s in range(TM // T_CTX):
                dst[s] = t[:, :, s * T_CTX:(s + 1) * T_CTX]


def _qkv_call(x_lat, x_ctx, mod, g, w_qkv, cast_weights):
    cast_in, cast_out, cast_shapes = zip(*[_cast_specs(w, l) for w, l in cast_weights])
    new_cache = pl.BlockSpec((TM // T_CTX, N_HEADS, HEAD_DIM, T_CTX),
                             lambda i: (jnp.maximum(i - N_LAT_TILES, 0), 0, 0, 0))
    return pl.pallas_call(
        functools.partial(_qkv_kernel, len(cast_weights)),
        grid=(N_TILES,),
        in_specs=[_LAT_SPEC, _CTX_SPEC, _MOD_SPEC, _const_spec((1, D)), _const_spec((D, 3 * D)), *cast_in],
        out_specs=[_ROW_SPEC, _ROW_SPEC, _ROW_SPEC, new_cache, new_cache, _ROW_SPEC, *cast_out],
        out_shape=[jax.ShapeDtypeStruct((N_ROWS, D), BF16)] * 3
        + [jax.ShapeDtypeStruct((B_CTX, N_HEADS, HEAD_DIM, T_CTX), F32)] * 2
        + [jax.ShapeDtypeStruct((N_ROWS, D), F32)] + list(cast_shapes),
        compiler_params=_params(("arbitrary",)),
        name="qkv_proj",
    )(x_lat, x_ctx, mod, g, w_qkv, *[w for w, _ in cast_weights])


ATTN_SLOTS = 3


def _kq(k, q):
    return lax.dot_general(k, q, (((1,), (1,)), ((), ())), preferred_element_type=F32)


def _attn_pipeline(n_units, scores_fn, probs_fn, out_fn):
    assert ATTN_SLOTS == 3
    col_max = {n: scores_fn(n) for n in range(min(2, n_units))}
    col_sum = {}
    outs = []
    for n in range(n_units):
        if n + 2 < n_units:
            col_max[n + 2] = scores_fn(n + 2)
        col_sum[n] = probs_fn(n, col_max[n])
        if n >= 1:
            outs.append(out_fn(n - 1, col_sum[n - 1]))
    outs.append(out_fn(n_units - 1, col_sum[n_units - 1]))
    return outs


def _store_scores(s_ref, slot, row0, s):
    s_ref[slot, row0:row0 + s.shape[0], :] = s
    return s.max(axis=0, keepdims=True)


def _store_probs(s_ref, p_ref, slot, row0, rows, m):
    p = jnp.exp2(s_ref[slot, row0:row0 + rows, :] - m)
    p_ref[slot, row0:row0 + rows, :] = p.astype(BF16)
    return p.sum(axis=0, keepdims=True)


def _head_masks(n_q):
    lane = lax.broadcasted_iota(jnp.int32, (n_q, 2 * HEAD_DIM), 1)
    return lane < HEAD_DIM, lane >= HEAD_DIM


def _ctx_attn_kernel(q_ref, k_ref, v_ref, o_ref, s_ref, p_ref):
    v_t = v_ref[...].T
    masks = _head_masks(T_CTX)

    def scores(h):
        cols = slice((h // 2) * 2 * HEAD_DIM, (h // 2 + 1) * 2 * HEAD_DIM)
        q = q_ref[:, cols]
        qm = jnp.where(masks[h % 2], q, jnp.zeros_like(q))
        return _store_scores(s_ref, h % ATTN_SLOTS, 0, _kq(k_ref[:, cols], qm))

    def probs(h, m):
        return _store_probs(s_ref, p_ref, h % ATTN_SLOTS, 0, T_CTX, m)

    def out(h, l):
        o = jnp.dot(v_t[h * HEAD_DIM:(h + 1) * HEAD_DIM, :], p_ref[h % ATTN_SLOTS], preferred_element_type=F32)
        return o / l

    outs = _attn_pipeline(N_HEADS, scores, probs, out)
    o_ref[...] = jnp.concatenate(outs, axis=0).T.astype(BF16)


def _ctx_attn_call(q, k, v):
    blk = pl.BlockSpec((T_CTX, D), lambda b: (N_LAT_ROWS // T_CTX + b, 0))
    return pl.pallas_call(
        _ctx_attn_kernel,
        grid=(B_CTX,),
        in_specs=[blk, blk, blk],
        out_specs=blk,
        out_shape=jax.ShapeDtypeStruct((N_ROWS, D), BF16),
        scratch_shapes=[pltpu.VMEM((ATTN_SLOTS, T_CTX, T_CTX), F32), pltpu.VMEM((ATTN_SLOTS, T_CTX, T_CTX), BF16)],
        compiler_params=_params(("arbitrary",)),
        name="ctx_attn",
    )(q, k, v)


def _key_rows_of_block(i):
    r_first = Q_BLK_ROWS * i
    return _row_window_start(r_first), _row_window_start(r_first + Q_BLK_ROWS - 1) + WIN_ROWS


N_Q_BLOCKS = GRID_ROWS // Q_BLK_ROWS
MAX_LAT_KEYS = max(kr1 - kr0 for kr0, kr1 in map(_key_rows_of_block, range(N_Q_BLOCKS))) * GRID_W


NBR_HEADS = 4


def _nbr_attn_kernel(q_ref, k_ref, v_ref, ckt_ref, cvt_ref, pt_ref, o_in_ref, o_ref, s_ref, p_ref):
    del o_in_ref
    pw = 2 * HEAD_DIM
    v_t = v_ref[...].T
    ck = ckt_ref[...].reshape(NBR_HEADS * HEAD_DIM, PAST).astype(BF16).T
    cv_t = cvt_ref[...].reshape(NBR_HEADS * HEAD_DIM, PAST).astype(BF16)
    n_q = Q_BLK_ROWS * GRID_W
    masks = _head_masks(n_q)
    neg_blk = jnp.full((GRID_W, 2 * GRID_W), NEG_INF, F32)

    def unit(n):
        h, i = divmod(n, N_Q_BLOCKS)
        return (h, i, n % ATTN_SLOTS) + _key_rows_of_block(i)

    def scores(n):
        h, i, slot, kr0, kr1 = unit(n)
        cols = slice((h // 2) * pw, (h // 2 + 1) * pw)
        r_first = Q_BLK_ROWS * i
        q = q_ref[r_first * GRID_W:(r_first + Q_BLK_ROWS) * GRID_W, cols]
        qm = jnp.where(masks[h % 2], q, jnp.zeros_like(q))
        bias_rows = []
        for kr in range(kr0, kr1):
            blks = []
            for r0 in range(r_first, r_first + Q_BLK_ROWS, 2):
                typ = _pair_type(kr, r0)
                blks.append(neg_blk if typ is None else pt_ref[h, typ])
            bias_rows.append(jnp.concatenate(blks, axis=1))
        s_lat = _kq(k_ref[kr0 * GRID_W:kr1 * GRID_W, cols], qm) + jnp.concatenate(bias_rows, axis=0)
        m_lat = _store_scores(s_ref, slot, 0, s_lat)
        m_ctx = _store_scores(s_ref, slot, MAX_LAT_KEYS, _kq(ck[:, cols], qm))
        return jnp.maximum(m_lat, m_ctx)

    def probs(n, m):
        _, _, slot, kr0, kr1 = unit(n)
        return (_store_probs(s_ref, p_ref, slot, 0, (kr1 - kr0) * GRID_W, m)
                + _store_probs(s_ref, p_ref, slot, MAX_LAT_KEYS, PAST, m))

    def out(n, l):
        h, _, slot, kr0, kr1 = unit(n)
        hs = slice(h * HEAD_DIM, (h + 1) * HEAD_DIM)
        lk = (kr1 - kr0) * GRID_W
        o = (jnp.dot(v_t[hs, kr0 * GRID_W:kr1 * GRID_W], p_ref[slot, 0:lk, :], preferred_element_type=F32)
             + jnp.dot(cv_t[hs, :], p_ref[slot, MAX_LAT_KEYS:MAX_LAT_KEYS + PAST, :], preferred_element_type=F32))
        return o / l

    outs = _attn_pipeline(NBR_HEADS * N_Q_BLOCKS, scores, probs, out)
    o_t = jnp.concatenate([jnp.concatenate(outs[h * N_Q_BLOCKS:(h + 1) * N_Q_BLOCKS], axis=1)
                           for h in range(NBR_HEADS)], axis=0)
    o_ref[...] = o_t.T.astype(BF16)


def _nbr_attn_call(q, k, v, cache_k, cache_v, pair_table, o_ctx):
    assert NBR_HEADS % 2 == 0
    blk = pl.BlockSpec((T_LAT, NBR_HEADS * HEAD_DIM), lambda b, g: (b, g))
    cache = pl.BlockSpec((None, NBR_HEADS, HEAD_DIM, PAST), lambda b, g: (b, g, 0, 0))
    return pl.pallas_call(
        _nbr_attn_kernel,
        grid=(B_LAT, N_HEADS // NBR_HEADS),
        in_specs=[blk, blk, blk, cache, cache,
                  pl.BlockSpec((NBR_HEADS, N_PAIR_TYPES, GRID_W, 2 * GRID_W), lambda b, g: (g, 0, 0, 0)),
                  pl.BlockSpec(memory_space=pl.ANY)],
        out_specs=blk,
        out_shape=jax.ShapeDtypeStruct((N_ROWS, D), BF16),
        input_output_aliases={6: 0},
        scratch_shapes=[pltpu.VMEM((ATTN_SLOTS, MAX_LAT_KEYS + PAST, Q_BLK_ROWS * GRID_W), F32),
                        pltpu.VMEM((ATTN_SLOTS, MAX_LAT_KEYS + PAST, Q_BLK_ROWS * GRID_W), BF16)],
        compiler_params=_params(("arbitrary", "arbitrary")),
        name="nbr_attn",
    )(q, k, v, cache_k, cache_v, pair_table, o_ctx)


def _post_kernel(final_norm, n_cast, x_ref, o_ref, m_ref, g_ref, wo_ref, wgu_ref, wd_ref, fg_ref, *refs):
    n_out = 2 if final_norm else 1
    cast_in = refs[:n_cast]
    out_refs = refs[n_cast:n_cast + n_out]
    cast_out = refs[n_cast + n_out:2 * n_cast + n_out]
    acc_ref, x1_ref = refs[2 * n_cast + n_out:]

    _cast_chunks(cast_in, cast_out)
    mix = jnp.dot(o_ref[...], wo_ref[...], preferred_element_type=F32)
    x1 = x_ref[...] + m_ref[2] * mix
    x1_ref[...] = x1
    h = _modulate(x1, g_ref[...], m_ref[3], m_ref[4]).astype(BF16)
    for c in range(D_FF // FF_CHUNK):
        cs = slice(c * FF_CHUNK, (c + 1) * FF_CHUNK)
        us = slice(D_FF + c * FF_CHUNK, D_FF + (c + 1) * FF_CHUNK)
        gate = jnp.dot(h, wgu_ref[:, cs], preferred_element_type=F32)
        up = jnp.dot(h, wgu_ref[:, us], preferred_element_type=F32)
        a = (gate * _sigmoid(gate) * up).astype(BF16)
        part = jnp.dot(a, wd_ref[cs, :], preferred_element_type=F32)
        if c == 0:
            acc_ref[...] = part
        else:
            acc_ref[...] += part
    def finish(out_ref):
        x2 = x1_ref[...] + m_ref[5] * acc_ref[...]
        if final_norm:
            ms = jnp.mean(x2 * x2, axis=-1, keepdims=True)
            x2 = x2 * lax.rsqrt(ms + RMS_EPS) * fg_ref[...]
        out_ref[...] = x2

    if not final_norm:
        finish(out_refs[0])
        return
    finish(x1_ref)
    is_lat = pl.program_id(0) < N_LAT_TILES

    @pl.when(is_lat)
    def _():
        out_refs[0][...] = x1_ref[...]

    @pl.when(jnp.logical_not(is_lat))
    def _():
        out_refs[1][...] = x1_ref[...]


def _post_call(x, o, mod, g, w_o, w_gu, w_down, final_g, final_norm, cast_weights=()):
    cast_in, cast_out, cast_shapes = (zip(*[_cast_specs(w, l) for w, l in cast_weights])
                                      if cast_weights else ((), (), ()))
    if final_norm:
        out_specs = [_LAT_SPEC, _CTX_SPEC]
        out_shape = [jax.ShapeDtypeStruct((N_LAT_ROWS, D), F32), jax.ShapeDtypeStruct((N_CTX_ROWS, D), F32)]
    else:
        out_specs = [_ROW_SPEC]
        out_shape = [jax.ShapeDtypeStruct((N_ROWS, D), F32)]
    return pl.pallas_call(
        functools.partial(_post_kernel, final_norm, len(cast_weights)),
        grid=(N_TILES,),
        in_specs=[
            _ROW_SPEC, _ROW_SPEC, _MOD_SPEC,
            _const_spec((1, D)),
            _const_spec((D, D)),
            _const_spec((D, 2 * D_FF)),
            _const_spec((D_FF, D)),
            _const_spec((1, D)),
            *cast_in,
        ],
        out_specs=[*out_specs, *cast_out],
        out_shape=[*out_shape, *cast_shapes],
        scratch_shapes=[pltpu.VMEM((TM, D), F32), pltpu.VMEM((TM, D), F32)],
        compiler_params=_params(("arbitrary",)),
        name="mix_out_ffn",
    )(x, o, mod, g, w_o, w_gu, w_down, final_g, *[w for w, _ in cast_weights])


def _lru_in_kernel(x_ref, m_ref, g_ref, w_ref, gate_ref, xr_ref):
    xm = _modulate(x_ref[...], g_ref[...], m_ref[0], m_ref[1])
    y = jnp.dot(xm.astype(BF16), w_ref[...], preferred_element_type=F32)
    gate_ref[...] = jax.nn.gelu(y[:, :D]).astype(BF16)
    xr_ref[...] = y[:, D:]


def _lru_in_call(x, mod, g, w_in):
    return pl.pallas_call(
        _lru_in_kernel,
        grid=(N_TILES,),
        in_specs=[_ROW_SPEC, _MOD_SPEC, _const_spec((1, D)), _const_spec((D, 2 * D))],
        out_specs=[_ROW_SPEC, _ROW_SPEC],
        out_shape=[jax.ShapeDtypeStruct((N_ROWS, D), BF16), jax.ShapeDtypeStruct((N_ROWS, D), F32)],
        compiler_params=_params(("arbitrary",)),
        name="lru_in_proj",
    )(x, mod, g, w_in)


def _lru_scan_kernel(nb, nt, aliased, xr_ref, gg_ref, cw_ref, cb_ref, wa_ref, wi_ref, ba_ref, bi_ref, lam_ref,
                     h0_ref, *refs):
    y_ref, ht_ref, xt_ref, hf_ref, ab_ref, bb_ref, af0, bf0, af1, bf1 = refs[1:] if aliased else refs
    fwd_slots = ((af0, bf0), (af1, bf1))
    left = CONV_W // 2
    pad_rows = (CONV_W - 1) * nb
    xt_ref[pl.ds(0, left * nb), :] = jnp.zeros((left * nb, LRU_BW), F32)
    xt_ref[pl.ds((left + nt) * nb, pad_rows - left * nb), :] = jnp.zeros((pad_rows - left * nb, LRU_BW), F32)
    for s in range(nb):
        xt_ref[pl.ds(left * nb + s, nt, stride=nb), :] = xr_ref[pl.ds(s * nt, nt), :]

    cw = cw_ref[...]
    cb = cb_ref[...]
    w4 = (jnp.concatenate([wa_ref[0], wi_ref[0], wa_ref[1], wi_ref[1]], axis=1) * 0.5).astype(BF16)
    half_ba = 0.5 * ba_ref[...]
    half_bi = 0.5 * bi_ref[...]
    neg_lam = -lam_ref[...]
    sp = jnp.maximum(neg_lam, 0.0) + jnp.log1p(jnp.exp(-jnp.abs(neg_lam)))
    c1 = (-0.5 * LRU_C * LOG2E) * sp
    crows = SCAN_CT * nb
    n_chunks = nt // SCAN_CT
    assert n_chunks % 2 == 0

    def chunk_rows(c):
        return pl.ds(pl.multiple_of(c * crows, 8), crows)

    def coeffs(c, slot):
        t0 = c * SCAN_CT
        xc = cb
        for j in range(CONV_W):
            xc = xc + xt_ref[pl.ds(pl.multiple_of((t0 + j) * nb, 8), crows), :] * cw[j:j + 1]
        half_xc = 0.5 * xc
        pre = jnp.dot(xc.astype(BF16), w4, preferred_element_type=F32)

        def direction(d):
            o = 2 * d * LRU_BW
            t_r = jnp.tanh(pre[:, o:o + LRU_BW] + half_ba[d:d + 1])
            a = jnp.exp2(c1[d:d + 1] * t_r + c1[d:d + 1])
            t_i = jnp.tanh(pre[:, o + LRU_BW:o + 2 * LRU_BW] + half_bi[d:d + 1])
            y = 1.0 - a * a
            root = jnp.where(y > 0.0, y * lax.rsqrt(y), 0.0)
            return a, root * (half_xc * t_i + half_xc)

        a_ref, b_ref = fwd_slots[slot]
        a_ref[...], b_ref[...] = direction(0)
        ab_ref[chunk_rows(c), :], bb_ref[chunk_rows(c), :] = direction(1)

    def scan_fwd(c, slot, h):
        a_ref, b_ref = fwd_slots[slot]
        for s in range(SCAN_CT):
            rows = slice(s * nb, (s + 1) * nb)
            h = a_ref[rows, :] * h + b_ref[rows, :]
            hf_ref[pl.ds(pl.multiple_of(c * crows + s * nb, 8), nb), :] = h
        return h

    def scan_bwd(c, h):
        for s in range(SCAN_CT - 1, 0, -2):
            rows1 = pl.ds(pl.multiple_of(c * crows + s * nb, 8), nb)
            rows0 = pl.ds(pl.multiple_of(c * crows + (s - 1) * nb, 8), nb)
            a1, b1 = ab_ref[rows1, :], bb_ref[rows1, :]
            a0, b0 = ab_ref[rows0, :], bb_ref[rows0, :]
            bb_ref[rows1, :] = a1 * h + b1
            h = (a0 * a1) * h + (a0 * b1 + b0)
            bb_ref[rows0, :] = h
        return h

    def emit(c):
        for s in range(nb):
            src = pl.ds(c * crows + s, SCAN_CT, stride=nb)
            dst = pl.ds(pl.multiple_of(s * nt + c * SCAN_CT, SCAN_CT), SCAN_CT)
            hsum = hf_ref[src, :] + bb_ref[src, :]
            y_ref[dst, :] = (hsum * gg_ref[dst, :].astype(F32)).astype(BF16)

    coeffs(0, 0)

    def pass1(k, h):
        c = 2 * k
        h = scan_fwd(c, 0, h)
        coeffs(c + 1, 1)
        h = scan_fwd(c + 1, 1, h)
        coeffs(jnp.minimum(c + 2, n_chunks - 1), 0)
        return h

    ht_ref[0] = lax.fori_loop(0, n_chunks // 2, pass1, h0_ref[0])

    hb = scan_bwd(n_chunks - 1, h0_ref[1])

    def pass2(k, h):
        c = n_chunks - 2 - k
        emit(c + 1)
        return scan_bwd(c, h)

    ht_ref[1] = lax.fori_loop(0, n_chunks - 1, pass2, hb)
    emit(0)


def _lru_scan_call(nb, nt, row_block0, gg, xr, h0, conv_w, conv_b, w_a, w_i, b_a, b_i, lam, y_prev=None):
    rows = nb * nt
    col = pl.BlockSpec((rows, LRU_BW), lambda n: (row_block0, n))
    vec2 = pl.BlockSpec((2, LRU_BW), lambda n: (0, n))
    wblk = pl.BlockSpec((2, None, LRU_BW, LRU_BW), lambda n: (0, n, 0, 0))
    state = pl.BlockSpec((2, nb, LRU_BW), lambda n: (0, 0, n))
    tm_rows = (nt + CONV_W - 1) * nb
    crows = SCAN_CT * nb
    aliased = y_prev is not None
    return pl.pallas_call(
        functools.partial(_lru_scan_kernel, nb, nt, aliased),
        grid=(LRU_BLOCKS,),
        in_specs=[
            col, col,
            pl.BlockSpec((CONV_W, LRU_BW), lambda n: (0, n)),
            pl.BlockSpec((1, LRU_BW), lambda n: (0, n)),
            wblk, wblk, vec2, vec2, vec2, state,
        ] + ([pl.BlockSpec(memory_space=pl.ANY)] if aliased else []),
        out_specs=[col, state],
        out_shape=[jax.ShapeDtypeStruct((N_ROWS, D), BF16), jax.ShapeDtypeStruct((2, nb, D), F32)],
        input_output_aliases={10: 0} if aliased else {},
        scratch_shapes=[pltpu.VMEM((tm_rows, LRU_BW), F32)] + [pltpu.VMEM((rows, LRU_BW), F32)] * 3
        + [pltpu.VMEM((crows, LRU_BW), F32)] * 4,
        compiler_params=_params(("arbitrary",)),
        name=f"lru_scan_{nb}x{nt}",
    )(xr, gg, conv_w, conv_b, w_a, w_i, b_a, b_i, lam, h0, *([y_prev] if aliased else []))


def kernel(x_prompt, x_sample, c, cache_k, cache_v, state_h, c_ctx, norm_g, w_mod, b_mod, attn_w_qkv, attn_w_o,
           attn_rpb, lru_w_in, lru_conv_w, lru_conv_b, lru_w_a, lru_b_a, lru_w_i, lru_b_i, lru_lam, lru_w_out,
           ffn_w_gu, ffn_w_down, final_g):
    x_lat = x_sample.reshape(N_LAT_ROWS, D)
    x_ctx = x_prompt.reshape(N_CTX_ROWS, D)
    cond = jnp.concatenate([c, c_ctx[None, :], jnp.zeros((N_COND - B_LAT - 1, D), F32)], axis=0)
    mod, pair_table, w_qkv = _prologue_call(cond, w_mod, b_mod, attn_rpb[0], attn_w_qkv)
    mod = mod.reshape(2, N_COND, N_MOD, 1, D)
    final_g2 = final_g.reshape(1, D)

    q, k, v, k_ctx, v_ctx, x, w_o, w_gu0, w_down0 = _qkv_call(
        x_lat, x_ctx, mod[0], norm_g[0, 0].reshape(1, D), w_qkv,
        ((attn_w_o, 0), (ffn_w_gu, 0), (ffn_w_down, 0)))
    o = _ctx_attn_call(q, k, v)
    cache_kt = jnp.transpose(cache_k[:, 0], (0, 2, 3, 1))
    cache_vt = jnp.transpose(cache_v[:, 0], (0, 2, 3, 1))
    o = _nbr_attn_call(q, k, v, cache_kt, cache_vt, pair_table, o)
    x, w_in, w_out, w_gu1, w_down1 = _post_call(
        x, o, mod[0], norm_g[0, 1].reshape(1, D), w_o, w_gu0, w_down0, final_g2, False,
        ((lru_w_in, 0), (lru_w_out, 0), (ffn_w_gu, 1), (ffn_w_down, 1)))

    gg, xr = _lru_in_call(x, mod[1], norm_g[1, 0].reshape(1, D), w_in)
    lru_p = (lru_conv_w[0], lru_conv_b[0].reshape(1, D), lru_w_a[0], lru_w_i[0], lru_b_a[0], lru_b_i[0], lru_lam[0])
    y, _ = _lru_scan_call(B_LAT, T_LAT, 0, gg, xr, jnp.swapaxes(state_h[:, 0], 0, 1), *lru_p)
    y, h_ctx = _lru_scan_call(B_CTX, T_CTX, N_LAT_ROWS // N_CTX_ROWS, gg, xr,
                              jnp.zeros((2, B_CTX, D), F32), *lru_p, y_prev=y)
    out_lat, out_ctx = _post_call(x, y, mod[1], norm_g[1, 1].reshape(1, D), w_out, w_gu1, w_down1, final_g2, True)

    y_sample = out_lat.reshape(B_LAT, T_LAT, D)
    y_prompt = out_ctx.reshape(B_CTX, T_CTX, D)
    new_k = jnp.transpose(k_ctx, (0, 3, 1, 2))[:, None]
    new_v = jnp.transpose(v_ctx, (0, 3, 1, 2))[:, None]
    new_h = jnp.swapaxes(h_ctx, 0, 1)[:, None]
    return y_prompt, y_sample, new_k, new_v, new_h
```

```python
import functools
import math

import jax
import jax.numpy as jnp
from jax import lax
from jax.experimental import pallas as pl
from jax.experimental.pallas import tpu as pltpu

F32 = jnp.float32
BF16 = jnp.bfloat16

D = 1024
B_CTX, T_CTX = 16, 256
B_LAT, T_LAT = 8, 1024
PAST = 512
GRID_W = 64
GRID_ROWS = T_LAT // GRID_W
N_HEADS = 16
HEAD_DIM = D // N_HEADS
ATTN_SCALE = HEAD_DIM ** -0.5
LOG2E = math.log2(math.e)
WIN_ROWS, WIN_COLS = 8, 16
N_DR = 2 * WIN_ROWS - 1
N_DC = 2 * WIN_COLS - 1
LRU_BLOCKS = 8
LRU_BW = D // LRU_BLOCKS
CONV_W = 4
LRU_C = 8.0
D_FF = 2816
N_MOD = 6
RMS_EPS = 1e-6
NEG_INF = -1e30
LANES = 128

N_LAT_ROWS = B_LAT * T_LAT
N_CTX_ROWS = B_CTX * T_CTX
N_ROWS = N_LAT_ROWS + N_CTX_ROWS
N_COND = 16

TM = 512
N_LAT_TILES = N_LAT_ROWS // TM
N_CTX_TILES = N_CTX_ROWS // TM
N_TILES = N_LAT_TILES + N_CTX_TILES
TILES_PER_LAT_SAMPLE = T_LAT // TM
MOD_TN = 1536
FF_CHUNK = 256
Q_BLK_ROWS = 4
N_PAIR_TYPES = 16
CAST_CHUNKS = 16
SCAN_CHUNK_ROWS = 512
VMEM_LIMIT = 56 * 1024 * 1024


def _sigmoid(x):
    return 0.5 * jnp.tanh(0.5 * x) + 0.5


def _modulate(x, g, shift, scale):
    ms = jnp.mean(x * x, axis=-1, keepdims=True)
    return (x * lax.rsqrt(ms + RMS_EPS)) * g * (1.0 + scale) + shift


def _cond_row_of_tile(i):
    return jnp.where(i < N_LAT_TILES, i // TILES_PER_LAT_SAMPLE, B_LAT)


def _const_spec(shape):
    nd = len(shape)
    return pl.BlockSpec(shape, lambda *_: (0,) * nd, pipeline_mode=pl.Buffered(1))


def _params(sem):
    return pltpu.CompilerParams(dimension_semantics=sem, vmem_limit_bytes=VMEM_LIMIT)


_ROW_SPEC = pl.BlockSpec((TM, D), lambda i: (i, 0))
_LAT_SPEC = pl.BlockSpec((TM, D), lambda i: (jnp.minimum(i, N_LAT_TILES - 1), 0))
_CTX_SPEC = pl.BlockSpec((TM, D), lambda i: (jnp.maximum(i - N_LAT_TILES, 0), 0))
_MOD_SPEC = pl.BlockSpec((None, N_MOD, 1, D), lambda i: (_cond_row_of_tile(i), 0, 0, 0))


def _cast_specs(w, layer):
    _, rows, cols = w.shape
    chunk = lambda i: jnp.minimum(i, CAST_CHUNKS - 1)
    return (pl.BlockSpec((None, rows // CAST_CHUNKS, cols), lambda i: (layer, chunk(i), 0)),
            pl.BlockSpec((rows // CAST_CHUNKS, cols), lambda i: (chunk(i), 0)),
            jax.ShapeDtypeStruct((rows, cols), BF16))


def _cast_chunks(in_refs, out_refs):
    @pl.when(pl.program_id(0) < CAST_CHUNKS)
    def _():
        for src, dst in zip(in_refs, out_refs):
            dst[...] = src[...].astype(BF16)


def _pair_table_blocks(r_ref, o_ref):
    shape = (GRID_W, 2 * GRID_W)
    ck = lax.broadcasted_iota(jnp.int32, shape, 0)
    lane = lax.broadcasted_iota(jnp.int32, shape, 1)
    cq = jnp.bitwise_and(lane, GRID_W - 1)
    cs = jnp.clip(cq - WIN_COLS // 2, 0, GRID_W - WIN_COLS)
    in_win = jnp.logical_and(ck >= cs, ck < cs + WIN_COLS)

    def toeplitz(h, d, lane0):
        base = jnp.broadcast_to(r_ref[h, d:d + 1, :], shape)
        return pltpu.roll(base, (lane0 - (WIN_COLS - 1)) % LANES, 1, stride=1, stride_axis=0)

    neg = jnp.full(shape, NEG_INF, F32)
    for h in range(r_ref.shape[0]):
        for t in range(N_PAIR_TYPES):
            d_left, d_right = (t + 1, t) if t < 14 else ((3, None) if t == 14 else (None, 10))
            left = neg if d_left is None else toeplitz(h, d_left, 0)
            right = neg if d_right is None else toeplitz(h, d_right, GRID_W)
            val = jnp.where(lane < GRID_W, left, right)
            o_ref[h, t] = jnp.where(in_win, val * LOG2E, NEG_INF)


def _prologue_kernel(cond_ref, w_ref, b_ref, r_ref, wqkv_ref, mod_ref, pt_ref, wq_ref):
    c = cond_ref[...]
    s = (c * _sigmoid(c)).astype(BF16)
    mod_ref[...] = jnp.dot(s, w_ref[...].astype(BF16), preferred_element_type=F32) + b_ref[...]
    _pair_table_blocks(r_ref, pt_ref)
    wq_ref[...] = wqkv_ref[...].astype(BF16)


def _prologue_call(cond, w_mod, b_mod, rpb, w_qkv):
    assert 2 * GRID_W == LANES
    depth = w_mod.shape[0]
    n = N_MOD * D
    cols_steps = n // MOD_TN
    steps = depth * cols_steps
    heads = N_HEADS // steps
    w_rows = D // steps
    step = lambda l, j: l * cols_steps + j
    r = jnp.pad(rpb[:, :, ::-1], ((0, 0), (0, 0), (0, LANES - N_DC)))
    return pl.pallas_call(
        _prologue_kernel,
        grid=(depth, cols_steps),
        in_specs=[
            pl.BlockSpec((N_COND, D), lambda l, j: (0, 0)),
            pl.BlockSpec((None, D, MOD_TN), lambda l, j: (l, 0, j)),
            pl.BlockSpec((None, 1, MOD_TN), lambda l, j: (l, 0, j)),
            pl.BlockSpec((heads, N_DR, LANES), lambda l, j: (step(l, j), 0, 0)),
            pl.BlockSpec((None, w_rows, 3 * D), lambda l, j: (0, step(l, j), 0)),
        ],
        out_specs=[
            pl.BlockSpec((None, N_COND, MOD_TN), lambda l, j: (l, 0, j)),
            pl.BlockSpec((heads, N_PAIR_TYPES, GRID_W, LANES), lambda l, j: (step(l, j), 0, 0, 0)),
            pl.BlockSpec((w_rows, 3 * D), lambda l, j: (step(l, j), 0)),
        ],
        out_shape=[jax.ShapeDtypeStruct((depth, N_COND, n), F32),
                   jax.ShapeDtypeStruct((N_HEADS, N_PAIR_TYPES, GRID_W, LANES), F32),
                   jax.ShapeDtypeStruct((D, 3 * D), BF16)],
        compiler_params=_params(("arbitrary", "arbitrary")),
        name="prologue",
    )(cond, w_mod, b_mod.reshape(depth, 1, n), r, w_qkv)


def _row_window_start(r):
    return min(max(r - WIN_ROWS // 2, 0), GRID_ROWS - WIN_ROWS)


def _pair_type(kr, r0):
    in0 = _row_window_start(r0) <= kr < _row_window_start(r0) + WIN_ROWS
    in1 = _row_window_start(r0 + 1) <= kr < _row_window_start(r0 + 1) + WIN_ROWS
    if in0 and in1:
        dl = kr - r0
        assert -6 <= dl <= 7
        return dl + 6
    if in0:
        assert kr - r0 == -(WIN_ROWS // 2)
        return 14
    if in1:
        assert kr - (r0 + 1) == WIN_ROWS // 2 - 1
        return 15
    return None


def _qkv_kernel(n_cast, xl_ref, xc_ref, m_ref, g_ref, w_ref, *refs):
    cast_in = refs[:n_cast]
    q_ref, k_ref, v_ref, kf_ref, vf_ref, x_ref = refs[n_cast:n_cast + 6]
    cast_out = refs[n_cast + 6:]
    _cast_chunks(cast_in, cast_out)
    x = jnp.where(pl.program_id(0) < N_LAT_TILES, xl_ref[...], xc_ref[...])
    x_ref[...] = x
    xm = _modulate(x, g_ref[...], m_ref[0], m_ref[1])
    qkv = jnp.dot(xm.astype(BF16), w_ref[...], preferred_element_type=F32)
    q_ref[...] = (qkv[:, :D] * (ATTN_SCALE * LOG2E)).astype(BF16)
    k = qkv[:, D:2 * D]
    v = qkv[:, 2 * D:]
    k_ref[...] = k.astype(BF16)
    v_ref[...] = v.astype(BF16)

    @pl.when(pl.program_id(0) >= N_LAT_TILES)
    def _():
        for src, dst in ((k, kf_ref), (v, vf_ref)):
            t = src.T.reshape(N_HEADS, HEAD_DIM, TM)
            for s in range(TM // T_CTX):
                dst[s] = t[:, :, s * T_CTX:(s + 1) * T_CTX]


def _qkv_call(x_lat, x_ctx, mod, g, w_qkv, cast_weights):
    cast_in, cast_out, cast_shapes = zip(*[_cast_specs(w, l) for w, l in cast_weights])
    new_cache = pl.BlockSpec((TM // T_CTX, N_HEADS, HEAD_DIM, T_CTX),
                             lambda i: (jnp.maximum(i - N_LAT_TILES, 0), 0, 0, 0))
    return pl.pallas_call(
        functools.partial(_qkv_kernel, len(cast_weights)),
        grid=(N_TILES,),
        in_specs=[_LAT_SPEC, _CTX_SPEC, _MOD_SPEC, _const_spec((1, D)), _const_spec((D, 3 * D)), *cast_in],
        out_specs=[_ROW_SPEC, _ROW_SPEC, _ROW_SPEC, new_cache, new_cache, _ROW_SPEC, *cast_out],
        out_shape=[jax.ShapeDtypeStruct((N_ROWS, D), BF16)] * 3
        + [jax.ShapeDtypeStruct((B_CTX, N_HEADS, HEAD_DIM, T_CTX), F32)] * 2
        + [jax.ShapeDtypeStruct((N_ROWS, D), F32)] + list(cast_shapes),
        compiler_params=_params(("arbitrary",)),
        name="qkv_proj",
    )(x_lat, x_ctx, mod, g, w_qkv, *[w for w, _ in cast_weights])


ATTN_SLOTS = 3


def _kq(k, q):
    return lax.dot_general(k, q, (((1,), (1,)), ((), ())), preferred_element_type=F32)


def _attn_pipeline(n_units, scores_fn, probs_fn, out_fn):
    assert ATTN_SLOTS == 3
    col_max = {n: scores_fn(n) for n in range(min(2, n_units))}
    col_sum = {}
    outs = []
    for n in range(n_units):
        if n + 2 < n_units:
            col_max[n + 2] = scores_fn(n + 2)
        col_sum[n] = probs_fn(n, col_max[n])
        if n >= 1:
            outs.append(out_fn(n - 1, col_sum[n - 1]))
    outs.append(out_fn(n_units - 1, col_sum[n_units - 1]))
    return outs


def _store_scores(s_ref, slot, row0, s):
    s_ref[slot, row0:row0 + s.shape[0], :] = s
    return s.max(axis=0, keepdims=True)


def _store_probs(s_ref, p_ref, slot, row0, rows, m):
    p = jnp.exp2(s_ref[slot, row0:row0 + rows, :] - m)
    p_ref[slot, row0:row0 + rows, :] = p.astype(BF16)
    return p.sum(axis=0, keepdims=True)


def _head_masks(n_q):
    lane = lax.broadcasted_iota(jnp.int32, (n_q, 2 * HEAD_DIM), 1)
    return lane < HEAD_DIM, lane >= HEAD_DIM


def _ctx_attn_kernel(q_ref, k_ref, v_ref, o_ref, s_ref, p_ref):
    v_t = v_ref[...].T
    masks = _head_masks(T_CTX)

    def scores(h):
        cols = slice((h // 2) * 2 * HEAD_DIM, (h // 2 + 1) * 2 * HEAD_DIM)
        q = q_ref[:, cols]
        qm = jnp.where(masks[h % 2], q, jnp.zeros_like(q))
        return _store_scores(s_ref, h % ATTN_SLOTS, 0, _kq(k_ref[:, cols], qm))

    def probs(h, m):
        return _store_probs(s_ref, p_ref, h % ATTN_SLOTS, 0, T_CTX, m)

    def out(h, l):
        o = jnp.dot(v_t[h * HEAD_DIM:(h + 1) * HEAD_DIM, :], p_ref[h % ATTN_SLOTS], preferred_element_type=F32)
        return o / l

    outs = _attn_pipeline(N_HEADS, scores, probs, out)
    o_ref[...] = jnp.concatenate(outs, axis=0).T.astype(BF16)


def _ctx_attn_call(q, k, v):
    blk = pl.BlockSpec((T_CTX, D), lambda b: (N_LAT_ROWS // T_CTX + b, 0))
    return pl.pallas_call(
        _ctx_attn_kernel,
        grid=(B_CTX,),
        in_specs=[blk, blk, blk],
        out_specs=blk,
        out_shape=jax.ShapeDtypeStruct((N_ROWS, D), BF16),
        scratch_shapes=[pltpu.VMEM((ATTN_SLOTS, T_CTX, T_CTX), F32), pltpu.VMEM((ATTN_SLOTS, T_CTX, T_CTX), BF16)],
        compiler_params=_params(("arbitrary",)),
        name="ctx_attn",
    )(q, k, v)


def _key_rows_of_block(i):
    r_first = Q_BLK_ROWS * i
    return _row_window_start(r_first), _row_window_start(r_first + Q_BLK_ROWS - 1) + WIN_ROWS


N_Q_BLOCKS = GRID_ROWS // Q_BLK_ROWS
MAX_LAT_KEYS = max(kr1 - kr0 for kr0, kr1 in map(_key_rows_of_block, range(N_Q_BLOCKS))) * GRID_W


NBR_HEADS = 4


def _nbr_attn_kernel(q_ref, k_ref, v_ref, ckt_ref, cvt_ref, pt_ref, o_in_ref, o_ref, s_ref, p_ref):
    del o_in_ref
    pw = 2 * HEAD_DIM
    v_t = v_ref[...].T
    ck = ckt_ref[...].reshape(NBR_HEADS * HEAD_DIM, PAST).astype(BF16).T
    cv_t = cvt_ref[...].reshape(NBR_HEADS * HEAD_DIM, PAST).astype(BF16)
    n_q = Q_BLK_ROWS * GRID_W
    masks = _head_masks(n_q)
    neg_blk = jnp.full((GRID_W, 2 * GRID_W), NEG_INF, F32)

    def unit(n):
        h, i = divmod(n, N_Q_BLOCKS)
        return (h, i, n % ATTN_SLOTS) + _key_rows_of_block(i)

    def scores(n):
        h, i, slot, kr0, kr1 = unit(n)
        cols = slice((h // 2) * pw, (h // 2 + 1) * pw)
        r_first = Q_BLK_ROWS * i
        q = q_ref[r_first * GRID_W:(r_first + Q_BLK_ROWS) * GRID_W, cols]
        qm = jnp.where(masks[h % 2], q, jnp.zeros_like(q))
        bias_rows = []
        for kr in range(kr0, kr1):
            blks = []
            for r0 in range(r_first, r_first + Q_BLK_ROWS, 2):
                typ = _pair_type(kr, r0)
                blks.append(neg_blk if typ is None else pt_ref[h, typ])
            bias_rows.append(jnp.concatenate(blks, axis=1))
        s_lat = _kq(k_ref[kr0 * GRID_W:kr1 * GRID_W, cols], qm) + jnp.concatenate(bias_rows, axis=0)
        m_lat = _store_scores(s_ref, slot, 0, s_lat)
        m_ctx = _store_scores(s_ref, slot, MAX_LAT_KEYS, _kq(ck[:, cols], qm))
        return jnp.maximum(m_lat, m_ctx)

    def probs(n, m):
        _, _, slot, kr0, kr1 = unit(n)
        return (_store_probs(s_ref, p_ref, slot, 0, (kr1 - kr0) * GRID_W, m)
                + _store_probs(s_ref, p_ref, slot, MAX_LAT_KEYS, PAST, m))

    def out(n, l):
        h, _, slot, kr0, kr1 = unit(n)
        hs = slice(h * HEAD_DIM, (h + 1) * HEAD_DIM)
        lk = (kr1 - kr0) * GRID_W
        o = (jnp.dot(v_t[hs, kr0 * GRID_W:kr1 * GRID_W], p_ref[slot, 0:lk, :], preferred_element_type=F32)
             + jnp.dot(cv_t[hs, :], p_ref[slot, MAX_LAT_KEYS:MAX_LAT_KEYS + PAST, :], preferred_element_type=F32))
        return o / l

    outs = _attn_pipeline(NBR_HEADS * N_Q_BLOCKS, scores, probs, out)
    o_t = jnp.concatenate([jnp.concatenate(outs[h * N_Q_BLOCKS:(h + 1) * N_Q_BLOCKS], axis=1)
                           for h in range(NBR_HEADS)], axis=0)
    o_ref[...] = o_t.T.astype(BF16)


def _nbr_attn_call(q, k, v, cache_k, cache_v, pair_table, o_ctx):
    assert NBR_HEADS % 2 == 0
    blk = pl.BlockSpec((T_LAT, NBR_HEADS * HEAD_DIM), lambda b, g: (b, g))
    cache = pl.BlockSpec((None, NBR_HEADS, HEAD_DIM, PAST), lambda b, g: (b, g, 0, 0))
    return pl.pallas_call(
        _nbr_attn_kernel,
        grid=(B_LAT, N_HEADS // NBR_HEADS),
        in_specs=[blk, blk, blk, cache, cache,
                  pl.BlockSpec((NBR_HEADS, N_PAIR_TYPES, GRID_W, 2 * GRID_W), lambda b, g: (g, 0, 0, 0)),
                  pl.BlockSpec(memory_space=pl.ANY)],
        out_specs=blk,
        out_shape=jax.ShapeDtypeStruct((N_ROWS, D), BF16),
        input_output_aliases={6: 0},
        scratch_shapes=[pltpu.VMEM((ATTN_SLOTS, MAX_LAT_KEYS + PAST, Q_BLK_ROWS * GRID_W), F32),
                        pltpu.VMEM((ATTN_SLOTS, MAX_LAT_KEYS + PAST, Q_BLK_ROWS * GRID_W), BF16)],
        compiler_params=_params(("arbitrary", "arbitrary")),
        name="nbr_attn",
    )(q, k, v, cache_k, cache_v, pair_table, o_ctx)


def _post_kernel(final_norm, n_cast, x_ref, o_ref, m_ref, g_ref, wo_ref, wgu_ref, wd_ref, fg_ref, *refs):
    n_out = 2 if final_norm else 1
    cast_in = refs[:n_cast]
    out_refs = refs[n_cast:n_cast + n_out]
    cast_out = refs[n_cast + n_out:2 * n_cast + n_out]
    acc_ref, x1_ref = refs[2 * n_cast + n_out:]

    _cast_chunks(cast_in, cast_out)
    mix = jnp.dot(o_ref[...], wo_ref[...], preferred_element_type=F32)
    x1 = x_ref[...] + m_ref[2] * mix
    x1_ref[...] = x1
    h = _modulate(x1, g_ref[...], m_ref[3], m_ref[4]).astype(BF16)
    for c in range(D_FF // FF_CHUNK):
        cs = slice(c * FF_CHUNK, (c + 1) * FF_CHUNK)
        us = slice(D_FF + c * FF_CHUNK, D_FF + (c + 1) * FF_CHUNK)
        gate = jnp.dot(h, wgu_ref[:, cs], preferred_element_type=F32)
        up = jnp.dot(h, wgu_ref[:, us], preferred_element_type=F32)
        a = (gate * _sigmoid(gate) * up).astype(BF16)
        part = jnp.dot(a, wd_ref[cs, :], preferred_element_type=F32)
        if c == 0:
            acc_ref[...] = part
        else:
            acc_ref[...] += part
    def finish(out_ref):
        x2 = x1_ref[...] + m_ref[5] * acc_ref[...]
        if final_norm:
            ms = jnp.mean(x2 * x2, axis=-1, keepdims=True)
            x2 = x2 * lax.rsqrt(ms + RMS_EPS) * fg_ref[...]
        out_ref[...] = x2

    if not final_norm:
        finish(out_refs[0])
        return
    finish(x1_ref)
    is_lat = pl.program_id(0) < N_LAT_TILES

    @pl.when(is_lat)
    def _():
        out_refs[0][...] = x1_ref[...]

    @pl.when(jnp.logical_not(is_lat))
    def _():
        out_refs[1][...] = x1_ref[...]


def _post_call(x, o, mod, g, w_o, w_gu, w_down, final_g, final_norm, cast_weights=()):
    cast_in, cast_out, cast_shapes = (zip(*[_cast_specs(w, l) for w, l in cast_weights])
                                      if cast_weights else ((), (), ()))
    if final_norm:
        out_specs = [_LAT_SPEC, _CTX_SPEC]
        out_shape = [jax.ShapeDtypeStruct((N_LAT_ROWS, D), F32), jax.ShapeDtypeStruct((N_CTX_ROWS, D), F32)]
    else:
        out_specs = [_ROW_SPEC]
        out_shape = [jax.ShapeDtypeStruct((N_ROWS, D), F32)]
    return pl.pallas_call(
        functools.partial(_post_kernel, final_norm, len(cast_weights)),
        grid=(N_TILES,),
        in_specs=[
            _ROW_SPEC, _ROW_SPEC, _MOD_SPEC,
            _const_spec((1, D)),
            _const_spec((D, D)),
            _const_spec((D, 2 * D_FF)),
            _const_spec((D_FF, D)),
            _const_spec((1, D)),
            *cast_in,
        ],
        out_specs=[*out_specs, *cast_out],
        out_shape=[*out_shape, *cast_shapes],
        scratch_shapes=[pltpu.VMEM((TM, D), F32), pltpu.VMEM((TM, D), F32)],
        compiler_params=_params(("arbitrary",)),
        name="mix_out_ffn",
    )(x, o, mod, g, w_o, w_gu, w_down, final_g, *[w for w, _ in cast_weights])


def _lru_in_kernel(x_ref, m_ref, g_ref, w_ref, gate_ref, xr_ref):
    xm = _modulate(x_ref[...], g_ref[...], m_ref[0], m_ref[1])
    y = jnp.dot(xm.astype(BF16), w_ref[...], preferred_element_type=F32)
    gate_ref[...] = jax.nn.gelu(y[:, :D]).astype(BF16)
    xr_ref[...] = y[:, D:]


def _lru_in_call(x, mod, g, w_in):
    return pl.pallas_call(
        _lru_in_kernel,
        grid=(N_TILES,),
        in_specs=[_ROW_SPEC, _MOD_SPEC, _const_spec((1, D)), _const_spec((D, 2 * D))],
        out_specs=[_ROW_SPEC, _ROW_SPEC],
        out_shape=[jax.ShapeDtypeStruct((N_ROWS, D), BF16), jax.ShapeDtypeStruct((N_ROWS, D), F32)],
        compiler_params=_params(("arbitrary",)),
        name="lru_in_proj",
    )(x, mod, g, w_in)


def _lru_scan_kernel(nb, nt, aliased, xr_ref, gg_ref, cw_ref, cb_ref, wa_ref, wi_ref, ba_ref, bi_ref, lam_ref,
                     h0_ref, *refs):
    y_ref, ht_ref, xt_ref, hf_ref, ab_ref, bb_ref, af0, bf0, af1, bf1 = refs[1:] if aliased else refs
    fwd_slots = ((af0, bf0), (af1, bf1))
    left = CONV_W // 2
    pad_rows = (CONV_W - 1) * nb
    xt_ref[pl.ds(0, left * nb), :] = jnp.zeros((left * nb, LRU_BW), F32)
    xt_ref[pl.ds((left + nt) * nb, pad_rows - left * nb), :] = jnp.zeros((pad_rows - left * nb, LRU_BW), F32)
    for s in range(nb):
        xt_ref[pl.ds(left * nb + s, nt, stride=nb), :] = xr_ref[pl.ds(s * nt, nt), :]

    cw = cw_ref[...]
    cb = cb_ref[...]
    w4 = (jnp.concatenate([wa_ref[0], wi_ref[0], wa_ref[1], wi_ref[1]], axis=1) * 0.5).astype(BF16)
    half_ba = 0.5 * ba_ref[...]
    half_bi = 0.5 * bi_ref[...]
    neg_lam = -lam_ref[...]
    sp = jnp.maximum(neg_lam, 0.0) + jnp.log1p(jnp.exp(-jnp.abs(neg_lam)))
    c1 = (-0.5 * LRU_C * LOG2E) * sp
    crows = SCAN_CHUNK_ROWS
    ct = crows // nb
    n_chunks = nt // ct
    assert n_chunks % 2 == 0 and ct % 2 == 0

    def chunk_rows(c):
        return pl.ds(pl.multiple_of(c * crows, 8), crows)

    def coeffs(c, slot):
        t0 = c * ct
        xc = cb
        for j in range(CONV_W):
            xc = xc + xt_ref[pl.ds(pl.multiple_of((t0 + j) * nb, 8), crows), :] * cw[j:j + 1]
        half_xc = 0.5 * xc
        pre = jnp.dot(xc.astype(BF16), w4, preferred_element_type=F32)

        def direction(d):
            o = 2 * d * LRU_BW
            t_r = jnp.tanh(pre[:, o:o + LRU_BW] + half_ba[d:d + 1])
            a = jnp.exp2(c1[d:d + 1] * t_r + c1[d:d + 1])
            t_i = jnp.tanh(pre[:, o + LRU_BW:o + 2 * LRU_BW] + half_bi[d:d + 1])
            y = 1.0 - a * a
            root = jnp.where(y > 0.0, y * lax.rsqrt(y), 0.0)
            return a, root * (half_xc * t_i + half_xc)

        a_ref, b_ref = fwd_slots[slot]
        a_ref[...], b_ref[...] = direction(0)
        ab_ref[chunk_rows(c), :], bb_ref[chunk_rows(c), :] = direction(1)

    def scan_fwd(c, slot, h):
        a_ref, b_ref = fwd_slots[slot]
        for s in range(ct):
            rows = slice(s * nb, (s + 1) * nb)
            h = a_ref[rows, :] * h + b_ref[rows, :]
            hf_ref[pl.ds(pl.multiple_of(c * crows + s * nb, 8), nb), :] = h
        return h

    def scan_bwd(c, h):
        for s in range(ct - 1, 0, -2):
            rows1 = pl.ds(pl.multiple_of(c * crows + s * nb, 8), nb)
            rows0 = pl.ds(pl.multiple_of(c * crows + (s - 1) * nb, 8), nb)
            a1, b1 = ab_ref[rows1, :], bb_ref[rows1, :]
            a0, b0 = ab_ref[rows0, :], bb_ref[rows0, :]
            bb_ref[rows1, :] = a1 * h + b1
            h = (a0 * a1) * h + (a0 * b1 + b0)
            bb_ref[rows0, :] = h
        return h

    def emit(c):
        for s in range(nb):
            src = pl.ds(c * crows + s, ct, stride=nb)
            dst = pl.ds(pl.multiple_of(s * nt + c * ct, ct), ct)
            hsum = hf_ref[src, :] + bb_ref[src, :]
            y_ref[dst, :] = (hsum * gg_ref[dst, :].astype(F32)).astype(BF16)

    coeffs(0, 0)

    def pass1(k, h):
        c = 2 * k
        h = scan_fwd(c, 0, h)
        coeffs(c + 1, 1)
        h = scan_fwd(c + 1, 1, h)
        coeffs(jnp.minimum(c + 2, n_chunks - 1), 0)
        return h

    ht_ref[0] = lax.fori_loop(0, n_chunks // 2, pass1, h0_ref[0])

    hb = scan_bwd(n_chunks - 1, h0_ref[1])

    def pass2(k, h):
        c = n_chunks - 2 - k
        emit(c + 1)
        return scan_bwd(c, h)

    ht_ref[1] = lax.fori_loop(0, n_chunks - 1, pass2, hb)
    emit(0)


def _lru_scan_call(nb, nt, row_block0, gg, xr, h0, conv_w, conv_b, w_a, w_i, b_a, b_i, lam, y_prev=None):
    rows = nb * nt
    col = pl.BlockSpec((rows, LRU_BW), lambda n: (row_block0, n))
    vec2 = pl.BlockSpec((2, LRU_BW), lambda n: (0, n))
    wblk = pl.BlockSpec((2, None, LRU_BW, LRU_BW), lambda n: (0, n, 0, 0))
    state = pl.BlockSpec((2, nb, LRU_BW), lambda n: (0, 0, n))
    tm_rows = (nt + CONV_W - 1) * nb
    crows = SCAN_CHUNK_ROWS
    aliased = y_prev is not None
    return pl.pallas_call(
        functools.partial(_lru_scan_kernel, nb, nt, aliased),
        grid=(LRU_BLOCKS,),
        in_specs=[
            col, col,
            pl.BlockSpec((CONV_W, LRU_BW), lambda n: (0, n)),
            pl.BlockSpec((1, LRU_BW), lambda n: (0, n)),
            wblk, wblk, vec2, vec2, vec2, state,
        ] + ([pl.BlockSpec(memory_space=pl.ANY)] if aliased else []),
        out_specs=[col, state],
        out_shape=[jax.ShapeDtypeStruct((N_ROWS, D), BF16), jax.ShapeDtypeStruct((2, nb, D), F32)],
        input_output_aliases={10: 0} if aliased else {},
        scratch_shapes=[pltpu.VMEM((tm_rows, LRU_BW), F32)] + [pltpu.VMEM((rows, LRU_BW), F32)] * 3
        + [pltpu.VMEM((crows, LRU_BW), F32)] * 4,
        compiler_params=_params(("arbitrary",)),
        name=f"lru_scan_{nb}x{nt}",
    )(xr, gg, conv_w, conv_b, w_a, w_i, b_a, b_i, lam, h0, *([y_prev] if aliased else []))


def kernel(x_prompt, x_sample, c, cache_k, cache_v, state_h, c_ctx, norm_g, w_mod, b_mod, attn_w_qkv, attn_w_o,
           attn_rpb, lru_w_in, lru_conv_w, lru_conv_b, lru_w_a, lru_b_a, lru_w_i, lru_b_i, lru_lam, lru_w_out,
           ffn_w_gu, ffn_w_down, final_g):
    x_lat = x_sample.reshape(N_LAT_ROWS, D)
    x_ctx = x_prompt.reshape(N_CTX_ROWS, D)
    cond = jnp.concatenate([c, c_ctx[None, :], jnp.zeros((N_COND - B_LAT - 1, D), F32)], axis=0)
    mod, pair_table, w_qkv = _prologue_call(cond, w_mod, b_mod, attn_rpb[0], attn_w_qkv)
    mod = mod.reshape(2, N_COND, N_MOD, 1, D)
    final_g2 = final_g.reshape(1, D)

    q, k, v, k_ctx, v_ctx, x, w_o, w_gu0, w_down0 = _qkv_call(
        x_lat, x_ctx, mod[0], norm_g[0, 0].reshape(1, D), w_qkv,
        ((attn_w_o, 0), (ffn_w_gu, 0), (ffn_w_down, 0)))
    o = _ctx_attn_call(q, k, v)
    cache_kt = jnp.transpose(cache_k[:, 0], (0, 2, 3, 1))
    cache_vt = jnp.transpose(cache_v[:, 0], (0, 2, 3, 1))
    o = _nbr_attn_call(q, k, v, cache_kt, cache_vt, pair_table, o)
    x, w_in, w_out, w_gu1, w_down1 = _post_call(
        x, o, mod[0], norm_g[0, 1].reshape(1, D), w_o, w_gu0, w_down0, final_g2, False,
        ((lru_w_in, 0), (lru_w_out, 0), (ffn_w_gu, 1), (ffn_w_down, 1)))

    gg, xr = _lru_in_call(x, mod[1], norm_g[1, 0].reshape(1, D), w_in)
    lru_p = (lru_conv_w[0], lru_conv_b[0].reshape(1, D), lru_w_a[0], lru_w_i[0], lru_b_a[0], lru_b_i[0], lru_lam[0])
    y, _ = _lru_scan_call(B_LAT, T_LAT, 0, gg, xr, jnp.swapaxes(state_h[:, 0], 0, 1), *lru_p)
    y, h_ctx = _lru_scan_call(B_CTX, T_CTX, N_LAT_ROWS // N_CTX_ROWS, gg, xr,
                              jnp.zeros((2, B_CTX, D), F32), *lru_p, y_prev=y)
    out_lat, out_ctx = _post_call(x, y, mod[1], norm_g[1, 1].reshape(1, D), w_out, w_gu1, w_down1, final_g2, True)

    y_sample = out_lat.reshape(B_LAT, T_LAT, D)
    y_prompt = out_ctx.reshape(B_CTX, T_CTX, D)
    new_k = jnp.transpose(k_ctx, (0, 3, 1, 2))[:, None]
    new_v = jnp.transpose(v_ctx, (0, 3, 1, 2))[:, None]
    new_h = jnp.swapaxes(h_ctx, 0, 1)[:, None]
    return y_prompt, y_sample, new_k, new_v, new_h
```

```python
import functools
import math

import jax
import jax.numpy as jnp
from jax import lax
from jax.experimental import pallas as pl
from jax.experimental.pallas import tpu as pltpu

F32 = jnp.float32
BF16 = jnp.bfloat16

D = 1024
B_CTX, T_CTX = 16, 256
B_LAT, T_LAT = 8, 1024
PAST = 512
GRID_W = 64
GRID_ROWS = T_LAT // GRID_W
N_HEADS = 16
HEAD_DIM = D // N_HEADS
ATTN_SCALE = HEAD_DIM ** -0.5
LOG2E = math.log2(math.e)
WIN_ROWS, WIN_COLS = 8, 16
N_DR = 2 * WIN_ROWS - 1
N_DC = 2 * WIN_COLS - 1
LRU_BLOCKS = 8
LRU_BW = D // LRU_BLOCKS
CONV_W = 4
LRU_C = 8.0
D_FF = 2816
N_MOD = 6
RMS_EPS = 1e-6
NEG_INF = -1e30
LANES = 128

N_LAT_ROWS = B_LAT * T_LAT
N_CTX_ROWS = B_CTX * T_CTX
N_ROWS = N_LAT_ROWS + N_CTX_ROWS
N_COND = 16

TM = 512
N_LAT_TILES = N_LAT_ROWS // TM
N_CTX_TILES = N_CTX_ROWS // TM
N_TILES = N_LAT_TILES + N_CTX_TILES
TILES_PER_LAT_SAMPLE = T_LAT // TM
MOD_TN = 1536
FF_CHUNK = 256
Q_BLK_ROWS = 4
N_PAIR_TYPES = 16
CAST_CHUNKS = 16
SCAN_CHUNK_ROWS = 512
VMEM_LIMIT = 56 * 1024 * 1024


def _sigmoid(x):
    return 0.5 * jnp.tanh(0.5 * x) + 0.5


def _modulate(x, g, shift, scale):
    ms = jnp.mean(x * x, axis=-1, keepdims=True)
    return (x * lax.rsqrt(ms + RMS_EPS)) * g * (1.0 + scale) + shift


def _cond_row_of_tile(i):
    return jnp.where(i < N_LAT_TILES, i // TILES_PER_LAT_SAMPLE, B_LAT)


def _const_spec(shape):
    nd = len(shape)
    return pl.BlockSpec(shape, lambda *_: (0,) * nd, pipeline_mode=pl.Buffered(1))


def _params(sem):
    return pltpu.CompilerParams(dimension_semantics=sem, vmem_limit_bytes=VMEM_LIMIT)


_ROW_SPEC = pl.BlockSpec((TM, D), lambda i: (i, 0))
_LAT_SPEC = pl.BlockSpec((TM, D), lambda i: (jnp.minimum(i, N_LAT_TILES - 1), 0))
_CTX_SPEC = pl.BlockSpec((TM, D), lambda i: (jnp.maximum(i - N_LAT_TILES, 0), 0))
_MOD_SPEC = pl.BlockSpec((None, N_MOD, 1, D), lambda i: (_cond_row_of_tile(i), 0, 0, 0))


def _cast_specs(w, layer):
    _, rows, cols = w.shape
    chunk = lambda i: jnp.minimum(i, CAST_CHUNKS - 1)
    return (pl.BlockSpec((None, rows // CAST_CHUNKS, cols), lambda i: (layer, chunk(i), 0)),
            pl.BlockSpec((rows // CAST_CHUNKS, cols), lambda i: (chunk(i), 0)),
            jax.ShapeDtypeStruct((rows, cols), BF16))


def _cast_chunks(in_refs, out_refs):
    @pl.when(pl.program_id(0) < CAST_CHUNKS)
    def _():
        for src, dst in zip(in_refs, out_refs):
            dst[...] = src[...].astype(BF16)


def _pair_table_blocks(r_ref, o_ref):
    shape = (GRID_W, 2 * GRID_W)
    ck = lax.broadcasted_iota(jnp.int32, shape, 0)
    lane = lax.broadcasted_iota(jnp.int32, shape, 1)
    cq = jnp.bitwise_and(lane, GRID_W - 1)
    cs = jnp.clip(cq - WIN_COLS // 2, 0, GRID_W - WIN_COLS)
    in_win = jnp.logical_and(ck >= cs, ck < cs + WIN_COLS)

    def toeplitz(h, d, lane0):
        base = jnp.broadcast_to(r_ref[h, d:d + 1, :], shape)
        return pltpu.roll(base, (lane0 - (WIN_COLS - 1)) % LANES, 1, stride=1, stride_axis=0)

    neg = jnp.full(shape, NEG_INF, F32)
    for h in range(r_ref.shape[0]):
        for t in range(N_PAIR_TYPES):
            d_left, d_right = (t + 1, t) if t < 14 else ((3, None) if t == 14 else (None, 10))
            left = neg if d_left is None else toeplitz(h, d_left, 0)
            right = neg if d_right is None else toeplitz(h, d_right, GRID_W)
            val = jnp.where(lane < GRID_W, left, right)
            o_ref[h, t] = jnp.where(in_win, val * LOG2E, NEG_INF)


def _prologue_kernel(cond_ref, w_ref, b_ref, r_ref, wqkv_ref, mod_ref, pt_ref, wq_ref):
    c = cond_ref[...]
    s = (c * _sigmoid(c)).astype(BF16)
    mod_ref[...] = jnp.dot(s, w_ref[...].astype(BF16), preferred_element_type=F32) + b_ref[...]
    _pair_table_blocks(r_ref, pt_ref)
    wq_ref[...] = wqkv_ref[...].astype(BF16)


def _prologue_call(cond, w_mod, b_mod, rpb, w_qkv):
    assert 2 * GRID_W == LANES
    depth = w_mod.shape[0]
    n = N_MOD * D
    cols_steps = n // MOD_TN
    steps = depth * cols_steps
    heads = N_HEADS // steps
    w_rows = D // steps
    step = lambda l, j: l * cols_steps + j
    r = jnp.pad(rpb[:, :, ::-1], ((0, 0), (0, 0), (0, LANES - N_DC)))
    return pl.pallas_call(
        _prologue_kernel,
        grid=(depth, cols_steps),
        in_specs=[
            pl.BlockSpec((N_COND, D), lambda l, j: (0, 0)),
            pl.BlockSpec((None, D, MOD_TN), lambda l, j: (l, 0, j)),
            pl.BlockSpec((None, 1, MOD_TN), lambda l, j: (l, 0, j)),
            pl.BlockSpec((heads, N_DR, LANES), lambda l, j: (step(l, j), 0, 0)),
            pl.BlockSpec((None, w_rows, 3 * D), lambda l, j: (0, step(l, j), 0)),
        ],
        out_specs=[
            pl.BlockSpec((None, N_COND, MOD_TN), lambda l, j: (l, 0, j)),
            pl.BlockSpec((heads, N_PAIR_TYPES, GRID_W, LANES), lambda l, j: (step(l, j), 0, 0, 0)),
            pl.BlockSpec((w_rows, 3 * D), lambda l, j: (step(l, j), 0)),
        ],
        out_shape=[jax.ShapeDtypeStruct((depth, N_COND, n), F32),
                   jax.ShapeDtypeStruct((N_HEADS, N_PAIR_TYPES, GRID_W, LANES), F32),
                   jax.ShapeDtypeStruct((D, 3 * D), BF16)],
        compiler_params=_params(("arbitrary", "arbitrary")),
        name="prologue",
    )(cond, w_mod, b_mod.reshape(depth, 1, n), r, w_qkv)


def _row_window_start(r):
    return min(max(r - WIN_ROWS // 2, 0), GRID_ROWS - WIN_ROWS)


def _pair_type(kr, r0):
    in0 = _row_window_start(r0) <= kr < _row_window_start(r0) + WIN_ROWS
    in1 = _row_window_start(r0 + 1) <= kr < _row_window_start(r0 + 1) + WIN_ROWS
    if in0 and in1:
        dl = kr - r0
        assert -6 <= dl <= 7
        return dl + 6
    if in0:
        assert kr - r0 == -(WIN_ROWS // 2)
        return 14
    if in1:
        assert kr - (r0 + 1) == WIN_ROWS // 2 - 1
        return 15
    return None


def _qkv_kernel(n_cast, xl_ref, xc_ref, m_ref, g_ref, w_ref, *refs):
    cast_in = refs[:n_cast]
    q_ref, k_ref, v_ref, kf_ref, vf_ref, x_ref = refs[n_cast:n_cast + 6]
    cast_out = refs[n_cast + 6:]
    _cast_chunks(cast_in, cast_out)
    x = jnp.where(pl.program_id(0) < N_LAT_TILES, xl_ref[...], xc_ref[...])
    x_ref[...] = x
    xm = _modulate(x, g_ref[...], m_ref[0], m_ref[1])
    qkv = jnp.dot(xm.astype(BF16), w_ref[...], preferred_element_type=F32)
    q_ref[...] = (qkv[:, :D] * (ATTN_SCALE * LOG2E)).astype(BF16)
    k = qkv[:, D:2 * D]
    v = qkv[:, 2 * D:]
    k_ref[...] = k.astype(BF16)
    v_ref[...] = v.astype(BF16)

    for src, dst in ((k, kf_ref), (v, vf_ref)):
        t = src.T.reshape(N_HEADS, HEAD_DIM, TM)
        for s in range(TM // T_CTX):
            dst[s] = t[:, :, s * T_CTX:(s + 1) * T_CTX]


def _qkv_call(x_lat, x_ctx, mod, g, w_qkv, cast_weights):
    cast_in, cast_out, cast_shapes = zip(*[_cast_specs(w, l) for w, l in cast_weights])
    new_cache = pl.BlockSpec((TM // T_CTX, N_HEADS, HEAD_DIM, T_CTX),
                             lambda i: (jnp.maximum(i - N_LAT_TILES, 0), 0, 0, 0))
    return pl.pallas_call(
        functools.partial(_qkv_kernel, len(cast_weights)),
        grid=(N_TILES,),
        in_specs=[_LAT_SPEC, _CTX_SPEC, _MOD_SPEC, _const_spec((1, D)), _const_spec((D, 3 * D)), *cast_in],
        out_specs=[_ROW_SPEC, _ROW_SPEC, _ROW_SPEC, new_cache, new_cache, _ROW_SPEC, *cast_out],
        out_shape=[jax.ShapeDtypeStruct((N_ROWS, D), BF16)] * 3
        + [jax.ShapeDtypeStruct((B_CTX, N_HEADS, HEAD_DIM, T_CTX), F32)] * 2
        + [jax.ShapeDtypeStruct((N_ROWS, D), F32)] + list(cast_shapes),
        compiler_params=_params(("arbitrary",)),
        name="qkv_proj",
    )(x_lat, x_ctx, mod, g, w_qkv, *[w for w, _ in cast_weights])


ATTN_SLOTS = 3


def _kq(k, q):
    return lax.dot_general(k, q, (((1,), (1,)), ((), ())), preferred_element_type=F32)


def _attn_pipeline(n_units, scores_fn, probs_fn, out_fn):
    assert ATTN_SLOTS == 3
    col_max = {n: scores_fn(n) for n in range(min(2, n_units))}
    col_sum = {}
    outs = []
    for n in range(n_units):
        if n + 2 < n_units:
            col_max[n + 2] = scores_fn(n + 2)
        col_sum[n] = probs_fn(n, col_max[n])
        if n >= 1:
            outs.append(out_fn(n - 1, col_sum[n - 1]))
    outs.append(out_fn(n_units - 1, col_sum[n_units - 1]))
    return outs


def _store_scores(s_ref, slot, row0, s):
    s_ref[slot, row0:row0 + s.shape[0], :] = s
    return s.max(axis=0, keepdims=True)


def _store_probs(s_ref, p_ref, slot, row0, rows, m):
    p = jnp.exp2(s_ref[slot, row0:row0 + rows, :] - m)
    p_ref[slot, row0:row0 + rows, :] = p.astype(BF16)
    return p.sum(axis=0, keepdims=True)


def _head_masks(n_q):
    lane = lax.broadcasted_iota(jnp.int32, (n_q, 2 * HEAD_DIM), 1)
    return lane < HEAD_DIM, lane >= HEAD_DIM


def _ctx_attn_kernel(q_ref, k_ref, v_ref, o_ref, s_ref, p_ref):
    v_t = v_ref[...].T
    masks = _head_masks(T_CTX)

    def scores(h):
        cols = slice((h // 2) * 2 * HEAD_DIM, (h // 2 + 1) * 2 * HEAD_DIM)
        q = q_ref[:, cols]
        qm = jnp.where(masks[h % 2], q, jnp.zeros_like(q))
        return _store_scores(s_ref, h % ATTN_SLOTS, 0, _kq(k_ref[:, cols], qm))

    def probs(h, m):
        return _store_probs(s_ref, p_ref, h % ATTN_SLOTS, 0, T_CTX, m)

    def out(h, l):
        o = jnp.dot(v_t[h * HEAD_DIM:(h + 1) * HEAD_DIM, :], p_ref[h % ATTN_SLOTS], preferred_element_type=F32)
        return o / l

    outs = _attn_pipeline(N_HEADS, scores, probs, out)
    o_ref[...] = jnp.concatenate(outs, axis=0).T.astype(BF16)


def _ctx_attn_call(q, k, v):
    blk = pl.BlockSpec((T_CTX, D), lambda b: (N_LAT_ROWS // T_CTX + b, 0))
    return pl.pallas_call(
        _ctx_attn_kernel,
        grid=(B_CTX,),
        in_specs=[blk, blk, blk],
        out_specs=blk,
        out_shape=jax.ShapeDtypeStruct((N_ROWS, D), BF16),
        scratch_shapes=[pltpu.VMEM((ATTN_SLOTS, T_CTX, T_CTX), F32), pltpu.VMEM((ATTN_SLOTS, T_CTX, T_CTX), BF16)],
        compiler_params=_params(("arbitrary",)),
        name="ctx_attn",
    )(q, k, v)


def _key_rows_of_block(i):
    r_first = Q_BLK_ROWS * i
    return _row_window_start(r_first), _row_window_start(r_first + Q_BLK_ROWS - 1) + WIN_ROWS


N_Q_BLOCKS = GRID_ROWS // Q_BLK_ROWS
MAX_LAT_KEYS = max(kr1 - kr0 for kr0, kr1 in map(_key_rows_of_block, range(N_Q_BLOCKS))) * GRID_W


NBR_HEADS = 4


def _nbr_attn_kernel(q_ref, k_ref, v_ref, ckt_ref, cvt_ref, pt_ref, o_in_ref, o_ref, s_ref, p_ref):
    del o_in_ref
    pw = 2 * HEAD_DIM
    v_t = v_ref[...].T
    ck = ckt_ref[...].reshape(NBR_HEADS * HEAD_DIM, PAST).astype(BF16).T
    cv_t = cvt_ref[...].reshape(NBR_HEADS * HEAD_DIM, PAST).astype(BF16)
    n_q = Q_BLK_ROWS * GRID_W
    masks = _head_masks(n_q)
    neg_blk = jnp.full((GRID_W, 2 * GRID_W), NEG_INF, F32)

    def unit(n):
        h, i = divmod(n, N_Q_BLOCKS)
        return (h, i, n % ATTN_SLOTS) + _key_rows_of_block(i)

    def scores(n):
        h, i, slot, kr0, kr1 = unit(n)
        cols = slice((h // 2) * pw, (h // 2 + 1) * pw)
        r_first = Q_BLK_ROWS * i
        q = q_ref[r_first * GRID_W:(r_first + Q_BLK_ROWS) * GRID_W, cols]
        qm = jnp.where(masks[h % 2], q, jnp.zeros_like(q))
        bias_rows = []
        for kr in range(kr0, kr1):
            blks = []
            for r0 in range(r_first, r_first + Q_BLK_ROWS, 2):
                typ = _pair_type(kr, r0)
                blks.append(neg_blk if typ is None else pt_ref[h, typ])
            bias_rows.append(jnp.concatenate(blks, axis=1))
        s_lat = _kq(k_ref[kr0 * GRID_W:kr1 * GRID_W, cols], qm) + jnp.concatenate(bias_rows, axis=0)
        m_lat = _store_scores(s_ref, slot, 0, s_lat)
        m_ctx = _store_scores(s_ref, slot, MAX_LAT_KEYS, _kq(ck[:, cols], qm))
        return jnp.maximum(m_lat, m_ctx)

    def probs(n, m):
        _, _, slot, kr0, kr1 = unit(n)
        return (_store_probs(s_ref, p_ref, slot, 0, (kr1 - kr0) * GRID_W, m)
                + _store_probs(s_ref, p_ref, slot, MAX_LAT_KEYS, PAST, m))

    def out(n, l):
        h, _, slot, kr0, kr1 = unit(n)
        hs = slice(h * HEAD_DIM, (h + 1) * HEAD_DIM)
        lk = (kr1 - kr0) * GRID_W
        o = (jnp.dot(v_t[hs, kr0 * GRID_W:kr1 * GRID_W], p_ref[slot, 0:lk, :], preferred_element_type=F32)
             + jnp.dot(cv_t[hs, :], p_ref[slot, MAX_LAT_KEYS:MAX_LAT_KEYS + PAST, :], preferred_element_type=F32))
        return o / l

    outs = _attn_pipeline(NBR_HEADS * N_Q_BLOCKS, scores, probs, out)
    o_t = jnp.concatenate([jnp.concatenate(outs[h * N_Q_BLOCKS:(h + 1) * N_Q_BLOCKS], axis=1)
                           for h in range(NBR_HEADS)], axis=0)
    o_ref[...] = o_t.T.astype(BF16)


def _nbr_attn_call(q, k, v, cache_k, cache_v, pair_table, o_ctx):
    assert NBR_HEADS % 2 == 0
    blk = pl.BlockSpec((T_LAT, NBR_HEADS * HEAD_DIM), lambda b, g: (b, g))
    cache = pl.BlockSpec((None, NBR_HEADS, HEAD_DIM, PAST), lambda b, g: (b, g, 0, 0))
    return pl.pallas_call(
        _nbr_attn_kernel,
        grid=(B_LAT, N_HEADS // NBR_HEADS),
        in_specs=[blk, blk, blk, cache, cache,
                  pl.BlockSpec((NBR_HEADS, N_PAIR_TYPES, GRID_W, 2 * GRID_W), lambda b, g: (g, 0, 0, 0)),
                  pl.BlockSpec(memory_space=pl.ANY)],
        out_specs=blk,
        out_shape=jax.ShapeDtypeStruct((N_ROWS, D), BF16),
        input_output_aliases={6: 0},
        scratch_shapes=[pltpu.VMEM((ATTN_SLOTS, MAX_LAT_KEYS + PAST, Q_BLK_ROWS * GRID_W), F32),
                        pltpu.VMEM((ATTN_SLOTS, MAX_LAT_KEYS + PAST, Q_BLK_ROWS * GRID_W), BF16)],
        compiler_params=_params(("arbitrary", "arbitrary")),
        name="nbr_attn",
    )(q, k, v, cache_k, cache_v, pair_table, o_ctx)


def _post_kernel(final_norm, n_cast, x_ref, o_ref, m_ref, g_ref, wo_ref, wgu_ref, wd_ref, fg_ref, *refs):
    n_out = 2 if final_norm else 1
    cast_in = refs[:n_cast]
    out_refs = refs[n_cast:n_cast + n_out]
    cast_out = refs[n_cast + n_out:2 * n_cast + n_out]
    acc_ref, x1_ref = refs[2 * n_cast + n_out:]

    _cast_chunks(cast_in, cast_out)
    mix = jnp.dot(o_ref[...], wo_ref[...], preferred_element_type=F32)
    x1 = x_ref[...] + m_ref[2] * mix
    x1_ref[...] = x1
    h = _modulate(x1, g_ref[...], m_ref[3], m_ref[4]).astype(BF16)
    for c in range(D_FF // FF_CHUNK):
        cs = slice(c * FF_CHUNK, (c + 1) * FF_CHUNK)
        us = slice(D_FF + c * FF_CHUNK, D_FF + (c + 1) * FF_CHUNK)
        gate = jnp.dot(h, wgu_ref[:, cs], preferred_element_type=F32)
        up = jnp.dot(h, wgu_ref[:, us], preferred_element_type=F32)
        a = (gate * _sigmoid(gate) * up).astype(BF16)
        part = jnp.dot(a, wd_ref[cs, :], preferred_element_type=F32)
        if c == 0:
            acc_ref[...] = part
        else:
            acc_ref[...] += part
    def finish(out_ref):
        x2 = x1_ref[...] + m_ref[5] * acc_ref[...]
        if final_norm:
            ms = jnp.mean(x2 * x2, axis=-1, keepdims=True)
            x2 = x2 * lax.rsqrt(ms + RMS_EPS) * fg_ref[...]
        out_ref[...] = x2

    if not final_norm:
        finish(out_refs[0])
        return
    finish(x1_ref)
    is_lat = pl.program_id(0) < N_LAT_TILES

    @pl.when(is_lat)
    def _():
        out_refs[0][...] = x1_ref[...]

    @pl.when(jnp.logical_not(is_lat))
    def _():
        out_refs[1][...] = x1_ref[...]


def _post_call(x, o, mod, g, w_o, w_gu, w_down, final_g, final_norm, cast_weights=()):
    cast_in, cast_out, cast_shapes = (zip(*[_cast_specs(w, l) for w, l in cast_weights])
                                      if cast_weights else ((), (), ()))
    if final_norm:
        out_specs = [_LAT_SPEC, _CTX_SPEC]
        out_shape = [jax.ShapeDtypeStruct((N_LAT_ROWS, D), F32), jax.ShapeDtypeStruct((N_CTX_ROWS, D), F32)]
    else:
        out_specs = [_ROW_SPEC]
        out_shape = [jax.ShapeDtypeStruct((N_ROWS, D), F32)]
    return pl.pallas_call(
        functools.partial(_post_kernel, final_norm, len(cast_weights)),
        grid=(N_TILES,),
        in_specs=[
            _ROW_SPEC, _ROW_SPEC, _MOD_SPEC,
            _const_spec((1, D)),
            _const_spec((D, D)),
            _const_spec((D, 2 * D_FF)),
            _const_spec((D_FF, D)),
            _const_spec((1, D)),
            *cast_in,
        ],
        out_specs=[*out_specs, *cast_out],
        out_shape=[*out_shape, *cast_shapes],
        scratch_shapes=[pltpu.VMEM((TM, D), F32), pltpu.VMEM((TM, D), F32)],
        compiler_params=_params(("arbitrary",)),
        name="mix_out_ffn",
    )(x, o, mod, g, w_o, w_gu, w_down, final_g, *[w for w, _ in cast_weights])


def _lru_in_kernel(x_ref, m_ref, g_ref, w_ref, gate_ref, xr_ref):
    xm = _modulate(x_ref[...], g_ref[...], m_ref[0], m_ref[1])
    y = jnp.dot(xm.astype(BF16), w_ref[...], preferred_element_type=F32)
    gate_ref[...] = jax.nn.gelu(y[:, :D]).astype(BF16)
    xr_ref[...] = y[:, D:]


def _lru_in_call(x, mod, g, w_in):
    return pl.pallas_call(
        _lru_in_kernel,
        grid=(N_TILES,),
        in_specs=[_ROW_SPEC, _MOD_SPEC, _const_spec((1, D)), _const_spec((D, 2 * D))],
        out_specs=[_ROW_SPEC, _ROW_SPEC],
        out_shape=[jax.ShapeDtypeStruct((N_ROWS, D), BF16), jax.ShapeDtypeStruct((N_ROWS, D), F32)],
        compiler_params=_params(("arbitrary",)),
        name="lru_in_proj",
    )(x, mod, g, w_in)


def _lru_scan_kernel(nb, nt, aliased, xr_ref, gg_ref, cw_ref, cb_ref, wa_ref, wi_ref, ba_ref, bi_ref, lam_ref,
                     h0_ref, *refs):
    y_ref, ht_ref, xt_ref, hf_ref, ab_ref, bb_ref, af0, bf0, af1, bf1 = refs[1:] if aliased else refs
    fwd_slots = ((af0, bf0), (af1, bf1))
    left = CONV_W // 2
    pad_rows = (CONV_W - 1) * nb
    xt_ref[pl.ds(0, left * nb), :] = jnp.zeros((left * nb, LRU_BW), F32)
    xt_ref[pl.ds((left + nt) * nb, pad_rows - left * nb), :] = jnp.zeros((pad_rows - left * nb, LRU_BW), F32)
    for s in range(nb):
        xt_ref[pl.ds(left * nb + s, nt, stride=nb), :] = xr_ref[pl.ds(s * nt, nt), :]

    cw = cw_ref[...]
    cb = cb_ref[...]
    w4 = (jnp.concatenate([wa_ref[0], wi_ref[0], wa_ref[1], wi_ref[1]], axis=1) * 0.5).astype(BF16)
    half_ba = 0.5 * ba_ref[...]
    half_bi = 0.5 * bi_ref[...]
    neg_lam = -lam_ref[...]
    sp = jnp.maximum(neg_lam, 0.0) + jnp.log1p(jnp.exp(-jnp.abs(neg_lam)))
    c1 = (-0.5 * LRU_C * LOG2E) * sp
    crows = SCAN_CHUNK_ROWS
    ct = crows // nb
    n_chunks = nt // ct
    assert n_chunks % 2 == 0 and ct % 2 == 0

    def chunk_rows(c):
        return pl.ds(pl.multiple_of(c * crows, 8), crows)

    def coeffs(c, slot):
        t0 = c * ct
        xc = cb
        for j in range(CONV_W):
            xc = xc + xt_ref[pl.ds(pl.multiple_of((t0 + j) * nb, 8), crows), :] * cw[j:j + 1]
        half_xc = 0.5 * xc
        pre = jnp.dot(xc.astype(BF16), w4, preferred_element_type=F32)

        def direction(d):
            o = 2 * d * LRU_BW
            t_r = jnp.tanh(pre[:, o:o + LRU_BW] + half_ba[d:d + 1])
            a = jnp.exp2(c1[d:d + 1] * t_r + c1[d:d + 1])
            t_i = jnp.tanh(pre[:, o + LRU_BW:o + 2 * LRU_BW] + half_bi[d:d + 1])
            y = 1.0 - a * a
            root = jnp.where(y > 0.0, y * lax.rsqrt(y), 0.0)
            return a, root * (half_xc * t_i + half_xc)

        a_ref, b_ref = fwd_slots[slot]
        a_ref[...], b_ref[...] = direction(0)
        ab_ref[chunk_rows(c), :], bb_ref[chunk_rows(c), :] = direction(1)

    def scan_fwd(c, slot, h):
        a_ref, b_ref = fwd_slots[slot]
        for s in range(ct):
            rows = slice(s * nb, (s + 1) * nb)
            h = a_ref[rows, :] * h + b_ref[rows, :]
            hf_ref[pl.ds(pl.multiple_of(c * crows + s * nb, 8), nb), :] = h
        return h

    def scan_bwd(c, h):
        for s in range(ct - 1, 0, -2):
            rows1 = pl.ds(pl.multiple_of(c * crows + s * nb, 8), nb)
            rows0 = pl.ds(pl.multiple_of(c * crows + (s - 1) * nb, 8), nb)
            a1, b1 = ab_ref[rows1, :], bb_ref[rows1, :]
            a0, b0 = ab_ref[rows0, :], bb_ref[rows0, :]
            bb_ref[rows1, :] = a1 * h + b1
            h = (a0 * a1) * h + (a0 * b1 + b0)
            bb_ref[rows0, :] = h
        return h

    def emit(c):
        for s in range(nb):
            src = pl.ds(c * crows + s, ct, stride=nb)
            dst = pl.ds(pl.multiple_of(s * nt + c * ct, ct), ct)
            hsum = hf_ref[src, :] + bb_ref[src, :]
            y_ref[dst, :] = (hsum * gg_ref[dst, :].astype(F32)).astype(BF16)

    coeffs(0, 0)

    def pass1(k, h):
        c = 2 * k
        h = scan_fwd(c, 0, h)
        coeffs(c + 1, 1)
        h = scan_fwd(c + 1, 1, h)
        coeffs(jnp.minimum(c + 2, n_chunks - 1), 0)
        return h

    ht_ref[0] = lax.fori_loop(0, n_chunks // 2, pass1, h0_ref[0])

    hb = scan_bwd(n_chunks - 1, h0_ref[1])

    def pass2(k, h):
        c = n_chunks - 2 - k
        emit(c + 1)
        return scan_bwd(c, h)

    ht_ref[1] = lax.fori_loop(0, n_chunks - 1, pass2, hb)
    emit(0)


def _lru_scan_call(nb, nt, row_block0, gg, xr, h0, conv_w, conv_b, w_a, w_i, b_a, b_i, lam, y_prev=None):
    rows = nb * nt
    col = pl.BlockSpec((rows, LRU_BW), lambda n: (row_block0, n))
    vec2 = pl.BlockSpec((2, LRU_BW), lambda n: (0, n))
    wblk = pl.BlockSpec((2, None, LRU_BW, LRU_BW), lambda n: (0, n, 0, 0))
    state = pl.BlockSpec((2, nb, LRU_BW), lambda n: (0, 0, n))
    tm_rows = (nt + CONV_W - 1) * nb
    crows = SCAN_CHUNK_ROWS
    aliased = y_prev is not None
    return pl.pallas_call(
        functools.partial(_lru_scan_kernel, nb, nt, aliased),
        grid=(LRU_BLOCKS,),
        in_specs=[
            col, col,
            pl.BlockSpec((CONV_W, LRU_BW), lambda n: (0, n)),
            pl.BlockSpec((1, LRU_BW), lambda n: (0, n)),
            wblk, wblk, vec2, vec2, vec2, state,
        ] + ([pl.BlockSpec(memory_space=pl.ANY)] if aliased else []),
        out_specs=[col, state],
        out_shape=[jax.ShapeDtypeStruct((N_ROWS, D), BF16), jax.ShapeDtypeStruct((2, nb, D), F32)],
        input_output_aliases={10: 0} if aliased else {},
        scratch_shapes=[pltpu.VMEM((tm_rows, LRU_BW), F32)] + [pltpu.VMEM((rows, LRU_BW), F32)] * 3
        + [pltpu.VMEM((crows, LRU_BW), F32)] * 4,
        compiler_params=_params(("arbitrary",)),
        name=f"lru_scan_{nb}x{nt}",
    )(xr, gg, conv_w, conv_b, w_a, w_i, b_a, b_i, lam, h0, *([y_prev] if aliased else []))


def kernel(x_prompt, x_sample, c, cache_k, cache_v, state_h, c_ctx, norm_g, w_mod, b_mod, attn_w_qkv, attn_w_o,
           attn_rpb, lru_w_in, lru_conv_w, lru_conv_b, lru_w_a, lru_b_a, lru_w_i, lru_b_i, lru_lam, lru_w_out,
           ffn_w_gu, ffn_w_down, final_g):
    x_lat = x_sample.reshape(N_LAT_ROWS, D)
    x_ctx = x_prompt.reshape(N_CTX_ROWS, D)
    cond = jnp.concatenate([c, c_ctx[None, :], jnp.zeros((N_COND - B_LAT - 1, D), F32)], axis=0)
    mod, pair_table, w_qkv = _prologue_call(cond, w_mod, b_mod, attn_rpb[0], attn_w_qkv)
    mod = mod.reshape(2, N_COND, N_MOD, 1, D)
    final_g2 = final_g.reshape(1, D)

    q, k, v, k_ctx, v_ctx, x, w_o, w_gu0, w_down0 = _qkv_call(
        x_lat, x_ctx, mod[0], norm_g[0, 0].reshape(1, D), w_qkv,
        ((attn_w_o, 0), (ffn_w_gu, 0), (ffn_w_down, 0)))
    o = _ctx_attn_call(q, k, v)
    cache_kt = jnp.transpose(cache_k[:, 0], (0, 2, 3, 1))
    cache_vt = jnp.transpose(cache_v[:, 0], (0, 2, 3, 1))
    o = _nbr_attn_call(q, k, v, cache_kt, cache_vt, pair_table, o)
    x, w_in, w_out, w_gu1, w_down1 = _post_call(
        x, o, mod[0], norm_g[0, 1].reshape(1, D), w_o, w_gu0, w_down0, final_g2, False,
        ((lru_w_in, 0), (lru_w_out, 0), (ffn_w_gu, 1), (ffn_w_down, 1)))

    gg, xr = _lru_in_call(x, mod[1], norm_g[1, 0].reshape(1, D), w_in)
    lru_p = (lru_conv_w[0], lru_conv_b[0].reshape(1, D), lru_w_a[0], lru_w_i[0], lru_b_a[0], lru_b_i[0], lru_lam[0])
    y, _ = _lru_scan_call(B_LAT, T_LAT, 0, gg, xr, jnp.swapaxes(state_h[:, 0], 0, 1), *lru_p)
    y, h_ctx = _lru_scan_call(B_CTX, T_CTX, N_LAT_ROWS // N_CTX_ROWS, gg, xr,
                              jnp.zeros((2, B_CTX, D), F32), *lru_p, y_prev=y)
    out_lat, out_ctx = _post_call(x, y, mod[1], norm_g[1, 1].reshape(1, D), w_out, w_gu1, w_down1, final_g2, True)

    y_sample = out_lat.reshape(B_LAT, T_LAT, D)
    y_prompt = out_ctx.reshape(B_CTX, T_CTX, D)
    new_k = jnp.transpose(k_ctx, (0, 3, 1, 2))[:, None]
    new_v = jnp.transpose(v_ctx, (0, 3, 1, 2))[:, None]
    new_h = jnp.swapaxes(h_ctx, 0, 1)[:, None]
    return y_prompt, y_sample, new_k, new_v, new_h
```

```python
import functools
import math

import jax
import jax.numpy as jnp
from jax import lax
from jax.experimental import pallas as pl
from jax.experimental.pallas import tpu as pltpu

F32 = jnp.float32
BF16 = jnp.bfloat16

D = 1024
B_CTX, T_CTX = 16, 256
B_LAT, T_LAT = 8, 1024
PAST = 512
GRID_W = 64
GRID_ROWS = T_LAT // GRID_W
N_HEADS = 16
HEAD_DIM = D // N_HEADS
ATTN_SCALE = HEAD_DIM ** -0.5
LOG2E = math.log2(math.e)
WIN_ROWS, WIN_COLS = 8, 16
N_DR = 2 * WIN_ROWS - 1
N_DC = 2 * WIN_COLS - 1
N_BOTH_TYPES = N_DR - 1
TYPE_LEFT_ONLY = N_BOTH_TYPES
TYPE_RIGHT_ONLY = N_BOTH_TYPES + 1
N_PAIR_TYPES = N_BOTH_TYPES + 2
FIRST_WIN_DR = -(WIN_ROWS // 2)
LAST_WIN_DR = WIN_ROWS // 2 - 1
LRU_BLOCKS = 8
LRU_BW = D // LRU_BLOCKS
CONV_W = 4
LRU_C = 8.0
D_FF = 2816
N_MOD = 6
RMS_EPS = 1e-6
NEG_INF = -1e30
LANES = 128

N_LAT_ROWS = B_LAT * T_LAT
N_CTX_ROWS = B_CTX * T_CTX
N_ROWS = N_LAT_ROWS + N_CTX_ROWS
N_COND = 16

TM = 512
N_LAT_TILES = N_LAT_ROWS // TM
N_CTX_TILES = N_CTX_ROWS // TM
N_TILES = N_LAT_TILES + N_CTX_TILES
TILES_PER_LAT_SAMPLE = T_LAT // TM
MOD_TN = 1536
FF_CHUNK = 256
Q_BLK_ROWS = 4
CAST_CHUNKS = 16
SCAN_CHUNK_ROWS = 512
V7X_VMEM_BYTES = 64 * 1024 * 1024
VMEM_LIMIT = V7X_VMEM_BYTES // 8 * 7


def _sigmoid(x):
    return 0.5 * jnp.tanh(0.5 * x) + 0.5


def _modulate(x, g, shift, scale):
    ms = jnp.mean(x * x, axis=-1, keepdims=True)
    return (x * lax.rsqrt(ms + RMS_EPS)) * g * (1.0 + scale) + shift


def _cond_row_of_tile(i):
    return jnp.where(i < N_LAT_TILES, i // TILES_PER_LAT_SAMPLE, B_LAT)


def _const_spec(shape):
    nd = len(shape)
    return pl.BlockSpec(shape, lambda *_: (0,) * nd, pipeline_mode=pl.Buffered(1))


def _params(sem):
    return pltpu.CompilerParams(dimension_semantics=sem, vmem_limit_bytes=VMEM_LIMIT)


_ROW_SPEC = pl.BlockSpec((TM, D), lambda i: (i, 0))
_LAT_SPEC = pl.BlockSpec((TM, D), lambda i: (jnp.minimum(i, N_LAT_TILES - 1), 0))
_CTX_SPEC = pl.BlockSpec((TM, D), lambda i: (jnp.maximum(i - N_LAT_TILES, 0), 0))
_MOD_SPEC = pl.BlockSpec((None, N_MOD, 1, D), lambda i: (_cond_row_of_tile(i), 0, 0, 0))


def _cast_specs(w, layer):
    _, rows, cols = w.shape
    chunk = lambda i: jnp.minimum(i, CAST_CHUNKS - 1)
    return (pl.BlockSpec((None, rows // CAST_CHUNKS, cols), lambda i: (layer, chunk(i), 0)),
            pl.BlockSpec((rows // CAST_CHUNKS, cols), lambda i: (chunk(i), 0)),
            jax.ShapeDtypeStruct((rows, cols), BF16))


def _cast_chunks(in_refs, out_refs):
    @pl.when(pl.program_id(0) < CAST_CHUNKS)
    def _():
        for src, dst in zip(in_refs, out_refs):
            dst[...] = src[...].astype(BF16)


def _pair_table_blocks(r_ref, o_ref):
    shape = (GRID_W, 2 * GRID_W)
    ck = lax.broadcasted_iota(jnp.int32, shape, 0)
    lane = lax.broadcasted_iota(jnp.int32, shape, 1)
    cq = jnp.bitwise_and(lane, GRID_W - 1)
    cs = jnp.clip(cq - WIN_COLS // 2, 0, GRID_W - WIN_COLS)
    in_win = jnp.logical_and(ck >= cs, ck < cs + WIN_COLS)

    def toeplitz(h, d, lane0):
        base = jnp.broadcast_to(r_ref[h, d:d + 1, :], shape)
        return pltpu.roll(base, (lane0 - (WIN_COLS - 1)) % LANES, 1, stride=1, stride_axis=0)

    neg = jnp.full(shape, NEG_INF, F32)
    for h in range(r_ref.shape[0]):
        for t in range(N_PAIR_TYPES):
            if t < N_BOTH_TYPES:
                d_left, d_right = t + 1, t
            elif t == TYPE_LEFT_ONLY:
                d_left, d_right = FIRST_WIN_DR + WIN_ROWS - 1, None
            else:
                d_left, d_right = None, LAST_WIN_DR + WIN_ROWS - 1
            left = neg if d_left is None else toeplitz(h, d_left, 0)
            right = neg if d_right is None else toeplitz(h, d_right, GRID_W)
            val = jnp.where(lane < GRID_W, left, right)
            o_ref[h, t] = jnp.where(in_win, val * LOG2E, NEG_INF)


def _prologue_kernel(cond_ref, w_ref, b_ref, r_ref, wqkv_ref, mod_ref, pt_ref, wq_ref):
    c = cond_ref[...]
    s = (c * _sigmoid(c)).astype(BF16)
    mod_ref[...] = jnp.dot(s, w_ref[...].astype(BF16), preferred_element_type=F32) + b_ref[...]
    _pair_table_blocks(r_ref, pt_ref)
    wq_ref[...] = wqkv_ref[...].astype(BF16)


def _prologue_call(cond, w_mod, b_mod, rpb, w_qkv):
    assert 2 * GRID_W == LANES
    depth = w_mod.shape[0]
    n = N_MOD * D
    cols_steps = n // MOD_TN
    steps = depth * cols_steps
    heads = N_HEADS // steps
    w_rows = D // steps
    step = lambda l, j: l * cols_steps + j
    r = jnp.pad(rpb[:, :, ::-1], ((0, 0), (0, 0), (0, LANES - N_DC)))
    return pl.pallas_call(
        _prologue_kernel,
        grid=(depth, cols_steps),
        in_specs=[
            pl.BlockSpec((N_COND, D), lambda l, j: (0, 0)),
            pl.BlockSpec((None, D, MOD_TN), lambda l, j: (l, 0, j)),
            pl.BlockSpec((None, 1, MOD_TN), lambda l, j: (l, 0, j)),
            pl.BlockSpec((heads, N_DR, LANES), lambda l, j: (step(l, j), 0, 0)),
            pl.BlockSpec((None, w_rows, 3 * D), lambda l, j: (0, step(l, j), 0)),
        ],
        out_specs=[
            pl.BlockSpec((None, N_COND, MOD_TN), lambda l, j: (l, 0, j)),
            pl.BlockSpec((heads, N_PAIR_TYPES, GRID_W, LANES), lambda l, j: (step(l, j), 0, 0, 0)),
            pl.BlockSpec((w_rows, 3 * D), lambda l, j: (step(l, j), 0)),
        ],
        out_shape=[jax.ShapeDtypeStruct((depth, N_COND, n), F32),
                   jax.ShapeDtypeStruct((N_HEADS, N_PAIR_TYPES, GRID_W, LANES), F32),
                   jax.ShapeDtypeStruct((D, 3 * D), BF16)],
        compiler_params=_params(("arbitrary", "arbitrary")),
        name="prologue",
    )(cond, w_mod, b_mod.reshape(depth, 1, n), r, w_qkv)


def _row_window_start(r):
    return min(max(r - WIN_ROWS // 2, 0), GRID_ROWS - WIN_ROWS)


def _pair_type(kr, r0):
    in0 = _row_window_start(r0) <= kr < _row_window_start(r0) + WIN_ROWS
    in1 = _row_window_start(r0 + 1) <= kr < _row_window_start(r0 + 1) + WIN_ROWS
    if in0 and in1:
        dl = kr - r0
        assert 0 <= dl + WIN_ROWS - 2 < N_BOTH_TYPES
        return dl + WIN_ROWS - 2
    if in0:
        assert kr - r0 == FIRST_WIN_DR
        return TYPE_LEFT_ONLY
    if in1:
        assert kr - (r0 + 1) == LAST_WIN_DR
        return TYPE_RIGHT_ONLY
    return None


def _qkv_kernel(n_cast, xl_ref, xc_ref, m_ref, g_ref, w_ref, *refs):
    cast_in = refs[:n_cast]
    q_ref, k_ref, v_ref, kf_ref, vf_ref, x_ref = refs[n_cast:n_cast + 6]
    cast_out = refs[n_cast + 6:]
    _cast_chunks(cast_in, cast_out)
    x = jnp.where(pl.program_id(0) < N_LAT_TILES, xl_ref[...], xc_ref[...])
    x_ref[...] = x
    xm = _modulate(x, g_ref[...], m_ref[0], m_ref[1])
    qkv = jnp.dot(xm.astype(BF16), w_ref[...], preferred_element_type=F32)
    q_ref[...] = (qkv[:, :D] * (ATTN_SCALE * LOG2E)).astype(BF16)
    k = qkv[:, D:2 * D]
    v = qkv[:, 2 * D:]
    k_ref[...] = k.astype(BF16)
    v_ref[...] = v.astype(BF16)

    for src, dst in ((k, kf_ref), (v, vf_ref)):
        t = src.T.reshape(N_HEADS, HEAD_DIM, TM)
        for s in range(TM // T_CTX):
            dst[s] = t[:, :, s * T_CTX:(s + 1) * T_CTX]


def _qkv_call(x_lat, x_ctx, mod, g, w_qkv, cast_weights):
    cast_in, cast_out, cast_shapes = zip(*[_cast_specs(w, l) for w, l in cast_weights])
    new_cache = pl.BlockSpec((TM // T_CTX, N_HEADS, HEAD_DIM, T_CTX),
                             lambda i: (jnp.maximum(i - N_LAT_TILES, 0), 0, 0, 0))
    return pl.pallas_call(
        functools.partial(_qkv_kernel, len(cast_weights)),
        grid=(N_TILES,),
        in_specs=[_LAT_SPEC, _CTX_SPEC, _MOD_SPEC, _const_spec((1, D)), _const_spec((D, 3 * D)), *cast_in],
        out_specs=[_ROW_SPEC, _ROW_SPEC, _ROW_SPEC, new_cache, new_cache, _ROW_SPEC, *cast_out],
        out_shape=[jax.ShapeDtypeStruct((N_ROWS, D), BF16)] * 3
        + [jax.ShapeDtypeStruct((B_CTX, N_HEADS, HEAD_DIM, T_CTX), F32)] * 2
        + [jax.ShapeDtypeStruct((N_ROWS, D), F32)] + list(cast_shapes),
        compiler_params=_params(("arbitrary",)),
        name="qkv_proj",
    )(x_lat, x_ctx, mod, g, w_qkv, *[w for w, _ in cast_weights])


ATTN_SLOTS = 3


def _kq(k, q):
    return lax.dot_general(k, q, (((1,), (1,)), ((), ())), preferred_element_type=F32)


def _attn_pipeline(n_units, scores_fn, probs_fn, out_fn):
    assert ATTN_SLOTS == 3
    col_max = {n: scores_fn(n) for n in range(min(2, n_units))}
    col_sum = {}
    outs = []
    for n in range(n_units):
        if n + 2 < n_units:
            col_max[n + 2] = scores_fn(n + 2)
        col_sum[n] = probs_fn(n, col_max[n])
        if n >= 1:
            outs.append(out_fn(n - 1, col_sum[n - 1]))
    outs.append(out_fn(n_units - 1, col_sum[n_units - 1]))
    return outs


def _store_scores(s_ref, slot, row0, s):
    s_ref[slot, row0:row0 + s.shape[0], :] = s
    return s.max(axis=0, keepdims=True)


def _store_probs(s_ref, p_ref, slot, row0, rows, m):
    p = jnp.exp2(s_ref[slot, row0:row0 + rows, :] - m)
    p_ref[slot, row0:row0 + rows, :] = p.astype(BF16)
    return p.sum(axis=0, keepdims=True)


def _head_masks(n_q):
    lane = lax.broadcasted_iota(jnp.int32, (n_q, 2 * HEAD_DIM), 1)
    return lane < HEAD_DIM, lane >= HEAD_DIM


CTX_SAMPLES = 4


def _ctx_attn_kernel(q_ref, k_ref, v_ref, o_ref, s_ref, p_ref):
    v_t = v_ref[...].T
    masks = _head_masks(T_CTX)

    def unit(n):
        smp, h = divmod(n, N_HEADS)
        return slice(smp * T_CTX, (smp + 1) * T_CTX), h, n % ATTN_SLOTS

    def scores(n):
        rows, h, slot = unit(n)
        cols = slice((h // 2) * 2 * HEAD_DIM, (h // 2 + 1) * 2 * HEAD_DIM)
        q = q_ref[rows, cols]
        qm = jnp.where(masks[h % 2], q, jnp.zeros_like(q))
        return _store_scores(s_ref, slot, 0, _kq(k_ref[rows, cols], qm))

    def probs(n, m):
        return _store_probs(s_ref, p_ref, unit(n)[2], 0, T_CTX, m)

    def out(n, l):
        rows, h, slot = unit(n)
        o = jnp.dot(v_t[h * HEAD_DIM:(h + 1) * HEAD_DIM, rows], p_ref[slot], preferred_element_type=F32)
        return o / l

    outs = _attn_pipeline(CTX_SAMPLES * N_HEADS, scores, probs, out)
    o_t = jnp.concatenate([jnp.concatenate(outs[smp * N_HEADS:(smp + 1) * N_HEADS], axis=0)
                           for smp in range(CTX_SAMPLES)], axis=1)
    o_ref[...] = o_t.T.astype(BF16)


def _ctx_attn_call(q, k, v):
    rows = CTX_SAMPLES * T_CTX
    blk = pl.BlockSpec((rows, D), lambda b: (N_LAT_ROWS // rows + b, 0))
    return pl.pallas_call(
        _ctx_attn_kernel,
        grid=(B_CTX // CTX_SAMPLES,),
        in_specs=[blk, blk, blk],
        out_specs=blk,
        out_shape=jax.ShapeDtypeStruct((N_ROWS, D), BF16),
        scratch_shapes=[pltpu.VMEM((ATTN_SLOTS, T_CTX, T_CTX), F32), pltpu.VMEM((ATTN_SLOTS, T_CTX, T_CTX), BF16)],
        compiler_params=_params(("arbitrary",)),
        name="ctx_attn",
    )(q, k, v)


def _key_rows_of_block(i):
    r_first = Q_BLK_ROWS * i
    return _row_window_start(r_first), _row_window_start(r_first + Q_BLK_ROWS - 1) + WIN_ROWS


N_Q_BLOCKS = GRID_ROWS // Q_BLK_ROWS
MAX_LAT_KEYS = max(kr1 - kr0 for kr0, kr1 in map(_key_rows_of_block, range(N_Q_BLOCKS))) * GRID_W


NBR_HEADS = 4


def _nbr_attn_kernel(q_ref, k_ref, v_ref, ckt_ref, cvt_ref, pt_ref, o_in_ref, o_ref, s_ref, p_ref):
    del o_in_ref
    pw = 2 * HEAD_DIM
    v_t = v_ref[...].T
    ck = ckt_ref[...].reshape(NBR_HEADS * HEAD_DIM, PAST).astype(BF16).T
    cv_t = cvt_ref[...].reshape(NBR_HEADS * HEAD_DIM, PAST).astype(BF16)
    n_q = Q_BLK_ROWS * GRID_W
    masks = _head_masks(n_q)
    neg_blk = jnp.full((GRID_W, 2 * GRID_W), NEG_INF, F32)

    def unit(n):
        h, i = divmod(n, N_Q_BLOCKS)
        return (h, i, n % ATTN_SLOTS) + _key_rows_of_block(i)

    def scores(n):
        h, i, slot, kr0, kr1 = unit(n)
        cols = slice((h // 2) * pw, (h // 2 + 1) * pw)
        r_first = Q_BLK_ROWS * i
        q = q_ref[r_first * GRID_W:(r_first + Q_BLK_ROWS) * GRID_W, cols]
        qm = jnp.where(masks[h % 2], q, jnp.zeros_like(q))
        bias_rows = []
        for kr in range(kr0, kr1):
            blks = []
            for r0 in range(r_first, r_first + Q_BLK_ROWS, 2):
                typ = _pair_type(kr, r0)
                blks.append(neg_blk if typ is None else pt_ref[h, typ])
            bias_rows.append(jnp.concatenate(blks, axis=1))
        s_lat = _kq(k_ref[kr0 * GRID_W:kr1 * GRID_W, cols], qm) + jnp.concatenate(bias_rows, axis=0)
        m_lat = _store_scores(s_ref, slot, 0, s_lat)
        m_ctx = _store_scores(s_ref, slot, MAX_LAT_KEYS, _kq(ck[:, cols], qm))
        return jnp.maximum(m_lat, m_ctx)

    def probs(n, m):
        _, _, slot, kr0, kr1 = unit(n)
        return (_store_probs(s_ref, p_ref, slot, 0, (kr1 - kr0) * GRID_W, m)
                + _store_probs(s_ref, p_ref, slot, MAX_LAT_KEYS, PAST, m))

    def out(n, l):
        h, _, slot, kr0, kr1 = unit(n)
        hs = slice(h * HEAD_DIM, (h + 1) * HEAD_DIM)
        lk = (kr1 - kr0) * GRID_W
        o = (jnp.dot(v_t[hs, kr0 * GRID_W:kr1 * GRID_W], p_ref[slot, 0:lk, :], preferred_element_type=F32)
             + jnp.dot(cv_t[hs, :], p_ref[slot, MAX_LAT_KEYS:MAX_LAT_KEYS + PAST, :], preferred_element_type=F32))
        return o / l

    outs = _attn_pipeline(NBR_HEADS * N_Q_BLOCKS, scores, probs, out)
    o_t = jnp.concatenate([jnp.concatenate(outs[h * N_Q_BLOCKS:(h + 1) * N_Q_BLOCKS], axis=1)
                           for h in range(NBR_HEADS)], axis=0)
    o_ref[...] = o_t.T.astype(BF16)


def _nbr_attn_call(q, k, v, cache_k, cache_v, pair_table, o_ctx):
    assert NBR_HEADS % 2 == 0
    blk = pl.BlockSpec((T_LAT, NBR_HEADS * HEAD_DIM), lambda b, g: (b, g))
    cache = pl.BlockSpec((None, NBR_HEADS, HEAD_DIM, PAST), lambda b, g: (b, g, 0, 0))
    return pl.pallas_call(
        _nbr_attn_kernel,
        grid=(B_LAT, N_HEADS // NBR_HEADS),
        in_specs=[blk, blk, blk, cache, cache,
                  pl.BlockSpec((NBR_HEADS, N_PAIR_TYPES, GRID_W, 2 * GRID_W), lambda b, g: (g, 0, 0, 0)),
                  pl.BlockSpec(memory_space=pl.ANY)],
        out_specs=blk,
        out_shape=jax.ShapeDtypeStruct((N_ROWS, D), BF16),
        input_output_aliases={6: 0},
        scratch_shapes=[pltpu.VMEM((ATTN_SLOTS, MAX_LAT_KEYS + PAST, Q_BLK_ROWS * GRID_W), F32),
                        pltpu.VMEM((ATTN_SLOTS, MAX_LAT_KEYS + PAST, Q_BLK_ROWS * GRID_W), BF16)],
        compiler_params=_params(("arbitrary", "arbitrary")),
        name="nbr_attn",
    )(q, k, v, cache_k, cache_v, pair_table, o_ctx)


def _post_kernel(final_norm, n_cast, x_ref, o_ref, m_ref, g_ref, wo_ref, wgu_ref, wd_ref, fg_ref, *refs):
    n_out = 2 if final_norm else 1
    cast_in = refs[:n_cast]
    out_refs = refs[n_cast:n_cast + n_out]
    cast_out = refs[n_cast + n_out:2 * n_cast + n_out]
    acc_ref, x1_ref = refs[2 * n_cast + n_out:]

    _cast_chunks(cast_in, cast_out)
    mix = jnp.dot(o_ref[...], wo_ref[...], preferred_element_type=F32)
    x1 = x_ref[...] + m_ref[2] * mix
    x1_ref[...] = x1
    h = _modulate(x1, g_ref[...], m_ref[3], m_ref[4]).astype(BF16)
    for c in range(D_FF // FF_CHUNK):
        cs = slice(c * FF_CHUNK, (c + 1) * FF_CHUNK)
        us = slice(D_FF + c * FF_CHUNK, D_FF + (c + 1) * FF_CHUNK)
        gate = jnp.dot(h, wgu_ref[:, cs], preferred_element_type=F32)
        up = jnp.dot(h, wgu_ref[:, us], preferred_element_type=F32)
        a = (gate * _sigmoid(gate) * up).astype(BF16)
        part = jnp.dot(a, wd_ref[cs, :], preferred_element_type=F32)
        if c == 0:
            acc_ref[...] = part
        else:
            acc_ref[...] += part
    def finish(out_ref):
        x2 = x1_ref[...] + m_ref[5] * acc_ref[...]
        if final_norm:
            ms = jnp.mean(x2 * x2, axis=-1, keepdims=True)
            x2 = x2 * lax.rsqrt(ms + RMS_EPS) * fg_ref[...]
        out_ref[...] = x2

    if not final_norm:
        finish(out_refs[0])
        return
    finish(x1_ref)
    is_lat = pl.program_id(0) < N_LAT_TILES

    @pl.when(is_lat)
    def _():
        out_refs[0][...] = x1_ref[...]

    @pl.when(jnp.logical_not(is_lat))
    def _():
        out_refs[1][...] = x1_ref[...]


def _post_call(x, o, mod, g, w_o, w_gu, w_down, final_g, final_norm, cast_weights=()):
    cast_in, cast_out, cast_shapes = (zip(*[_cast_specs(w, l) for w, l in cast_weights])
                                      if cast_weights else ((), (), ()))
    if final_norm:
        out_specs = [_LAT_SPEC, _CTX_SPEC]
        out_shape = [jax.ShapeDtypeStruct((N_LAT_ROWS, D), F32), jax.ShapeDtypeStruct((N_CTX_ROWS, D), F32)]
    else:
        out_specs = [_ROW_SPEC]
        out_shape = [jax.ShapeDtypeStruct((N_ROWS, D), F32)]
    return pl.pallas_call(
        functools.partial(_post_kernel, final_norm, len(cast_weights)),
        grid=(N_TILES,),
        in_specs=[
            _ROW_SPEC, _ROW_SPEC, _MOD_SPEC,
            _const_spec((1, D)),
            _const_spec((D, D)),
            _const_spec((D, 2 * D_FF)),
            _const_spec((D_FF, D)),
            _const_spec((1, D)),
            *cast_in,
        ],
        out_specs=[*out_specs, *cast_out],
        out_shape=[*out_shape, *cast_shapes],
        scratch_shapes=[pltpu.VMEM((TM, D), F32), pltpu.VMEM((TM, D), F32)],
        compiler_params=_params(("arbitrary",)),
        name="mix_out_ffn",
    )(x, o, mod, g, w_o, w_gu, w_down, final_g, *[w for w, _ in cast_weights])


def _lru_in_kernel(x_ref, m_ref, g_ref, w_ref, gate_ref, xr_ref):
    xm = _modulate(x_ref[...], g_ref[...], m_ref[0], m_ref[1])
    y = jnp.dot(xm.astype(BF16), w_ref[...], preferred_element_type=F32)
    gate_ref[...] = jax.nn.gelu(y[:, :D]).astype(BF16)
    xr_ref[...] = y[:, D:]


def _lru_in_call(x, mod, g, w_in):
    return pl.pallas_call(
        _lru_in_kernel,
        grid=(N_TILES,),
        in_specs=[_ROW_SPEC, _MOD_SPEC, _const_spec((1, D)), _const_spec((D, 2 * D))],
        out_specs=[_ROW_SPEC, _ROW_SPEC],
        out_shape=[jax.ShapeDtypeStruct((N_ROWS, D), BF16), jax.ShapeDtypeStruct((N_ROWS, D), F32)],
        compiler_params=_params(("arbitrary",)),
        name="lru_in_proj",
    )(x, mod, g, w_in)


def _lru_scan_kernel(nb, nt, aliased, xr_ref, gg_ref, cw_ref, cb_ref, wa_ref, wi_ref, ba_ref, bi_ref, lam_ref,
                     h0_ref, *refs):
    y_ref, ht_ref, xt_ref, hf_ref, ab_ref, bb_ref, af0, bf0, af1, bf1 = refs[1:] if aliased else refs
    fwd_slots = ((af0, bf0), (af1, bf1))
    left = CONV_W // 2
    pad_rows = (CONV_W - 1) * nb
    xt_ref[pl.ds(0, left * nb), :] = jnp.zeros((left * nb, LRU_BW), F32)
    xt_ref[pl.ds((left + nt) * nb, pad_rows - left * nb), :] = jnp.zeros((pad_rows - left * nb, LRU_BW), F32)
    for s in range(nb):
        xt_ref[pl.ds(left * nb + s, nt, stride=nb), :] = xr_ref[pl.ds(s * nt, nt), :]

    cw = cw_ref[...]
    cb = cb_ref[...]
    w4 = (jnp.concatenate([wa_ref[0], wi_ref[0], wa_ref[1], wi_ref[1]], axis=1) * 0.5).astype(BF16)
    half_ba = 0.5 * ba_ref[...]
    half_bi = 0.5 * bi_ref[...]
    neg_lam = -lam_ref[...]
    sp = jnp.maximum(neg_lam, 0.0) + jnp.log1p(jnp.exp(-jnp.abs(neg_lam)))
    c1 = (-0.5 * LRU_C * LOG2E) * sp
    crows = SCAN_CHUNK_ROWS
    ct = crows // nb
    n_chunks = nt // ct
    assert n_chunks % 2 == 0 and ct % 2 == 0

    def chunk_rows(c):
        return pl.ds(pl.multiple_of(c * crows, 8), crows)

    def coeffs(c, slot):
        t0 = c * ct
        xc = cb
        for j in range(CONV_W):
            xc = xc + xt_ref[pl.ds(pl.multiple_of((t0 + j) * nb, 8), crows), :] * cw[j:j + 1]
        half_xc = 0.5 * xc
        pre = jnp.dot(xc.astype(BF16), w4, preferred_element_type=F32)

        def direction(d):
            o = 2 * d * LRU_BW
            t_r = jnp.tanh(pre[:, o:o + LRU_BW] + half_ba[d:d + 1])
            a = jnp.exp2(c1[d:d + 1] * t_r + c1[d:d + 1])
            t_i = jnp.tanh(pre[:, o + LRU_BW:o + 2 * LRU_BW] + half_bi[d:d + 1])
            y = 1.0 - a * a
            root = jnp.where(y > 0.0, y * lax.rsqrt(y), 0.0)
            return a, root * (half_xc * t_i + half_xc)

        a_ref, b_ref = fwd_slots[slot]
        a_ref[...], b_ref[...] = direction(0)
        ab_ref[chunk_rows(c), :], bb_ref[chunk_rows(c), :] = direction(1)

    def scan_fwd(c, slot, h):
        a_ref, b_ref = fwd_slots[slot]
        for s in range(ct):
            rows = slice(s * nb, (s + 1) * nb)
            h = a_ref[rows, :] * h + b_ref[rows, :]
            hf_ref[pl.ds(pl.multiple_of(c * crows + s * nb, 8), nb), :] = h
        return h

    def scan_bwd(c, h):
        for s in range(ct - 1, 0, -2):
            rows1 = pl.ds(pl.multiple_of(c * crows + s * nb, 8), nb)
            rows0 = pl.ds(pl.multiple_of(c * crows + (s - 1) * nb, 8), nb)
            a1, b1 = ab_ref[rows1, :], bb_ref[rows1, :]
            a0, b0 = ab_ref[rows0, :], bb_ref[rows0, :]
            bb_ref[rows1, :] = a1 * h + b1
            h = (a0 * a1) * h + (a0 * b1 + b0)
            bb_ref[rows0, :] = h
        return h

    def emit(c):
        for s in range(nb):
            src = pl.ds(c * crows + s, ct, stride=nb)
            dst = pl.ds(pl.multiple_of(s * nt + c * ct, ct), ct)
            hsum = hf_ref[src, :] + bb_ref[src, :]
            y_ref[dst, :] = (hsum * gg_ref[dst, :].astype(F32)).astype(BF16)

    coeffs(0, 0)

    def pass1(k, h):
        c = 2 * k
        h = scan_fwd(c, 0, h)
        coeffs(c + 1, 1)
        h = scan_fwd(c + 1, 1, h)
        coeffs(jnp.minimum(c + 2, n_chunks - 1), 0)
        return h

    ht_ref[0] = lax.fori_loop(0, n_chunks // 2, pass1, h0_ref[0])

    hb = scan_bwd(n_chunks - 1, h0_ref[1])

    def pass2(k, h):
        c = n_chunks - 2 - k
        emit(c + 1)
        return scan_bwd(c, h)

    ht_ref[1] = lax.fori_loop(0, n_chunks - 1, pass2, hb)
    emit(0)


def _lru_scan_call(nb, nt, row_block0, gg, xr, h0, conv_w, conv_b, w_a, w_i, b_a, b_i, lam, y_prev=None):
    rows = nb * nt
    col = pl.BlockSpec((rows, LRU_BW), lambda n: (row_block0, n))
    vec2 = pl.BlockSpec((2, LRU_BW), lambda n: (0, n))
    wblk = pl.BlockSpec((2, None, LRU_BW, LRU_BW), lambda n: (0, n, 0, 0))
    state = pl.BlockSpec((2, nb, LRU_BW), lambda n: (0, 0, n))
    tm_rows = (nt + CONV_W - 1) * nb
    crows = SCAN_CHUNK_ROWS
    aliased = y_prev is not None
    return pl.pallas_call(
        functools.partial(_lru_scan_kernel, nb, nt, aliased),
        grid=(LRU_BLOCKS,),
        in_specs=[
            col, col,
            pl.BlockSpec((CONV_W, LRU_BW), lambda n: (0, n)),
            pl.BlockSpec((1, LRU_BW), lambda n: (0, n)),
            wblk, wblk, vec2, vec2, vec2, state,
        ] + ([pl.BlockSpec(memory_space=pl.ANY)] if aliased else []),
        out_specs=[col, state],
        out_shape=[jax.ShapeDtypeStruct((N_ROWS, D), BF16), jax.ShapeDtypeStruct((2, nb, D), F32)],
        input_output_aliases={10: 0} if aliased else {},
        scratch_shapes=[pltpu.VMEM((tm_rows, LRU_BW), F32)] + [pltpu.VMEM((rows, LRU_BW), F32)] * 3
        + [pltpu.VMEM((crows, LRU_BW), F32)] * 4,
        compiler_params=_params(("arbitrary",)),
        name=f"lru_scan_{nb}x{nt}",
    )(xr, gg, conv_w, conv_b, w_a, w_i, b_a, b_i, lam, h0, *([y_prev] if aliased else []))


def kernel(x_prompt, x_sample, c, cache_k, cache_v, state_h, c_ctx, norm_g, w_mod, b_mod, attn_w_qkv, attn_w_o,
           attn_rpb, lru_w_in, lru_conv_w, lru_conv_b, lru_w_a, lru_b_a, lru_w_i, lru_b_i, lru_lam, lru_w_out,
           ffn_w_gu, ffn_w_down, final_g):
    x_lat = x_sample.reshape(N_LAT_ROWS, D)
    x_ctx = x_prompt.reshape(N_CTX_ROWS, D)
    cond = jnp.concatenate([c, c_ctx[None, :], jnp.zeros((N_COND - B_LAT - 1, D), F32)], axis=0)
    mod, pair_table, w_qkv = _prologue_call(cond, w_mod, b_mod, attn_rpb[0], attn_w_qkv)
    mod = mod.reshape(2, N_COND, N_MOD, 1, D)
    final_g2 = final_g.reshape(1, D)

    q, k, v, k_ctx, v_ctx, x, w_o, w_gu0, w_down0 = _qkv_call(
        x_lat, x_ctx, mod[0], norm_g[0, 0].reshape(1, D), w_qkv,
        ((attn_w_o, 0), (ffn_w_gu, 0), (ffn_w_down, 0)))
    o = _ctx_attn_call(q, k, v)
    cache_kt = jnp.transpose(cache_k[:, 0], (0, 2, 3, 1))
    cache_vt = jnp.transpose(cache_v[:, 0], (0, 2, 3, 1))
    o = _nbr_attn_call(q, k, v, cache_kt, cache_vt, pair_table, o)
    x, w_in, w_out, w_gu1, w_down1 = _post_call(
        x, o, mod[0], norm_g[0, 1].reshape(1, D), w_o, w_gu0, w_down0, final_g2, False,
        ((lru_w_in, 0), (lru_w_out, 0), (ffn_w_gu, 1), (ffn_w_down, 1)))

    gg, xr = _lru_in_call(x, mod[1], norm_g[1, 0].reshape(1, D), w_in)
    lru_p = (lru_conv_w[0], lru_conv_b[0].reshape(1, D), lru_w_a[0], lru_w_i[0], lru_b_a[0], lru_b_i[0], lru_lam[0])
    y, _ = _lru_scan_call(B_LAT, T_LAT, 0, gg, xr, jnp.swapaxes(state_h[:, 0], 0, 1), *lru_p)
    y, h_ctx = _lru_scan_call(B_CTX, T_CTX, N_LAT_ROWS // N_CTX_ROWS, gg, xr,
                              jnp.zeros((2, B_CTX, D), F32), *lru_p, y_prev=y)
    out_lat, out_ctx = _post_call(x, y, mod[1], norm_g[1, 1].reshape(1, D), w_out, w_gu1, w_down1, final_g2, True)

    y_sample = out_lat.reshape(B_LAT, T_LAT, D)
    y_prompt = out_ctx.reshape(B_CTX, T_CTX, D)
    new_k = jnp.transpose(k_ctx, (0, 3, 1, 2))[:, None]
    new_v = jnp.transpose(v_ctx, (0, 3, 1, 2))[:, None]
    new_h = jnp.swapaxes(h_ctx, 0, 1)[:, None]
    return y_prompt, y_sample, new_k, new_v, new_h
```

```python
import functools
import math

import jax
import jax.numpy as jnp
from jax import lax
from jax.experimental import pallas as pl
from jax.experimental.pallas import tpu as pltpu

F32 = jnp.float32
BF16 = jnp.bfloat16

D = 1024
B_CTX, T_CTX = 16, 256
B_LAT, T_LAT = 8, 1024
PAST = 512
GRID_W = 64
GRID_ROWS = T_LAT // GRID_W
N_HEADS = 16
HEAD_DIM = D // N_HEADS
ATTN_SCALE = HEAD_DIM ** -0.5
LOG2E = math.log2(math.e)
WIN_ROWS, WIN_COLS = 8, 16
N_DR = 2 * WIN_ROWS - 1
N_DC = 2 * WIN_COLS - 1
N_BOTH_TYPES = N_DR - 1
TYPE_LEFT_ONLY = N_BOTH_TYPES
TYPE_RIGHT_ONLY = N_BOTH_TYPES + 1
N_PAIR_TYPES = N_BOTH_TYPES + 2
FIRST_WIN_DR = -(WIN_ROWS // 2)
LAST_WIN_DR = WIN_ROWS // 2 - 1
LRU_BLOCKS = 8
LRU_BW = D // LRU_BLOCKS
CONV_W = 4
LRU_C = 8.0
D_FF = 2816
N_MOD = 6
RMS_EPS = 1e-6
NEG_INF = -1e30
LANES = 128

N_LAT_ROWS = B_LAT * T_LAT
N_CTX_ROWS = B_CTX * T_CTX
N_ROWS = N_LAT_ROWS + N_CTX_ROWS
N_COND = 16

TM = 512
N_LAT_TILES = N_LAT_ROWS // TM
N_CTX_TILES = N_CTX_ROWS // TM
N_TILES = N_LAT_TILES + N_CTX_TILES
TILES_PER_LAT_SAMPLE = T_LAT // TM
MOD_TN = 3072
FF_CHUNK = 256
Q_BLK_ROWS = 4
CAST_CHUNKS = 16
SCAN_CHUNK_ROWS = 512
V7X_VMEM_BYTES = 64 * 1024 * 1024
VMEM_LIMIT = V7X_VMEM_BYTES // 8 * 7


def _sigmoid(x):
    return 0.5 * jnp.tanh(0.5 * x) + 0.5


def _modulate(x, g, shift, scale):
    ms = jnp.mean(x * x, axis=-1, keepdims=True)
    return (x * lax.rsqrt(ms + RMS_EPS)) * g * (1.0 + scale) + shift


def _cond_row_of_tile(i):
    return jnp.where(i < N_LAT_TILES, i // TILES_PER_LAT_SAMPLE, B_LAT)


def _const_spec(shape):
    nd = len(shape)
    return pl.BlockSpec(shape, lambda *_: (0,) * nd, pipeline_mode=pl.Buffered(1))


def _params(sem):
    return pltpu.CompilerParams(dimension_semantics=sem, vmem_limit_bytes=VMEM_LIMIT)


_ROW_SPEC = pl.BlockSpec((TM, D), lambda i: (i, 0))
_LAT_SPEC = pl.BlockSpec((TM, D), lambda i: (jnp.minimum(i, N_LAT_TILES - 1), 0))
_CTX_SPEC = pl.BlockSpec((TM, D), lambda i: (jnp.maximum(i - N_LAT_TILES, 0), 0))
_MOD_SPEC = pl.BlockSpec((None, N_MOD, 1, D), lambda i: (_cond_row_of_tile(i), 0, 0, 0))


def _cast_specs(w, layer):
    _, rows, cols = w.shape
    chunk = lambda i: jnp.minimum(i, CAST_CHUNKS - 1)
    return (pl.BlockSpec((None, rows // CAST_CHUNKS, cols), lambda i: (layer, chunk(i), 0)),
            pl.BlockSpec((rows // CAST_CHUNKS, cols), lambda i: (chunk(i), 0)),
            jax.ShapeDtypeStruct((rows, cols), BF16))


def _cast_chunks(in_refs, out_refs):
    @pl.when(pl.program_id(0) < CAST_CHUNKS)
    def _():
        for src, dst in zip(in_refs, out_refs):
            dst[...] = src[...].astype(BF16)


def _pair_table_blocks(r_ref, o_ref):
    shape = (GRID_W, 2 * GRID_W)
    ck = lax.broadcasted_iota(jnp.int32, shape, 0)
    lane = lax.broadcasted_iota(jnp.int32, shape, 1)
    cq = jnp.bitwise_and(lane, GRID_W - 1)
    cs = jnp.clip(cq - WIN_COLS // 2, 0, GRID_W - WIN_COLS)
    in_win = jnp.logical_and(ck >= cs, ck < cs + WIN_COLS)

    def toeplitz(h, d, lane0):
        base = jnp.broadcast_to(r_ref[h, d:d + 1, :], shape)
        return pltpu.roll(base, (lane0 - (WIN_COLS - 1)) % LANES, 1, stride=1, stride_axis=0)

    neg = jnp.full(shape, NEG_INF, F32)
    for h in range(r_ref.shape[0]):
        for t in range(N_PAIR_TYPES):
            if t < N_BOTH_TYPES:
                d_left, d_right = t + 1, t
            elif t == TYPE_LEFT_ONLY:
                d_left, d_right = FIRST_WIN_DR + WIN_ROWS - 1, None
            else:
                d_left, d_right = None, LAST_WIN_DR + WIN_ROWS - 1
            left = neg if d_left is None else toeplitz(h, d_left, 0)
            right = neg if d_right is None else toeplitz(h, d_right, GRID_W)
            val = jnp.where(lane < GRID_W, left, right)
            o_ref[h, t] = jnp.where(in_win, val * LOG2E, NEG_INF)


def _prologue_kernel(cond_ref, w_ref, b_ref, r_ref, wqkv_ref, mod_ref, pt_ref, wq_ref):
    c = cond_ref[...]
    s = (c * _sigmoid(c)).astype(BF16)
    mod_ref[...] = jnp.dot(s, w_ref[...].astype(BF16), preferred_element_type=F32) + b_ref[...]
    _pair_table_blocks(r_ref, pt_ref)
    wq_ref[...] = wqkv_ref[...].astype(BF16)


def _prologue_call(cond, w_mod, b_mod, rpb, w_qkv):
    assert 2 * GRID_W == LANES
    depth = w_mod.shape[0]
    n = N_MOD * D
    cols_steps = n // MOD_TN
    steps = depth * cols_steps
    heads = N_HEADS // steps
    w_rows = D // steps
    step = lambda l, j: l * cols_steps + j
    r = jnp.pad(rpb[:, :, ::-1], ((0, 0), (0, 0), (0, LANES - N_DC)))
    return pl.pallas_call(
        _prologue_kernel,
        grid=(depth, cols_steps),
        in_specs=[
            pl.BlockSpec((N_COND, D), lambda l, j: (0, 0)),
            pl.BlockSpec((None, D, MOD_TN), lambda l, j: (l, 0, j)),
            pl.BlockSpec((None, 1, MOD_TN), lambda l, j: (l, 0, j)),
            pl.BlockSpec((heads, N_DR, LANES), lambda l, j: (step(l, j), 0, 0)),
            pl.BlockSpec((None, w_rows, 3 * D), lambda l, j: (0, step(l, j), 0)),
        ],
        out_specs=[
            pl.BlockSpec((None, N_COND, MOD_TN), lambda l, j: (l, 0, j)),
            pl.BlockSpec((heads, N_PAIR_TYPES, GRID_W, LANES), lambda l, j: (step(l, j), 0, 0, 0)),
            pl.BlockSpec((w_rows, 3 * D), lambda l, j: (step(l, j), 0)),
        ],
        out_shape=[jax.ShapeDtypeStruct((depth, N_COND, n), F32),
                   jax.ShapeDtypeStruct((N_HEADS, N_PAIR_TYPES, GRID_W, LANES), F32),
                   jax.ShapeDtypeStruct((D, 3 * D), BF16)],
        compiler_params=_params(("arbitrary", "arbitrary")),
        name="prologue",
    )(cond, w_mod, b_mod.reshape(depth, 1, n), r, w_qkv)


def _row_window_start(r):
    return min(max(r - WIN_ROWS // 2, 0), GRID_ROWS - WIN_ROWS)


def _pair_type(kr, r0):
    in0 = _row_window_start(r0) <= kr < _row_window_start(r0) + WIN_ROWS
    in1 = _row_window_start(r0 + 1) <= kr < _row_window_start(r0 + 1) + WIN_ROWS
    if in0 and in1:
        dl = kr - r0
        assert 0 <= dl + WIN_ROWS - 2 < N_BOTH_TYPES
        return dl + WIN_ROWS - 2
    if in0:
        assert kr - r0 == FIRST_WIN_DR
        return TYPE_LEFT_ONLY
    if in1:
        assert kr - (r0 + 1) == LAST_WIN_DR
        return TYPE_RIGHT_ONLY
    return None


def _qkv_kernel(n_cast, xl_ref, xc_ref, m_ref, g_ref, w_ref, *refs):
    cast_in = refs[:n_cast]
    q_ref, k_ref, v_ref, kf_ref, vf_ref, x_ref = refs[n_cast:n_cast + 6]
    cast_out = refs[n_cast + 6:]
    _cast_chunks(cast_in, cast_out)
    x = jnp.where(pl.program_id(0) < N_LAT_TILES, xl_ref[...], xc_ref[...])
    x_ref[...] = x
    xm = _modulate(x, g_ref[...], m_ref[0], m_ref[1])
    qkv = jnp.dot(xm.astype(BF16), w_ref[...], preferred_element_type=F32)
    q_ref[...] = (qkv[:, :D] * (ATTN_SCALE * LOG2E)).astype(BF16)
    k = qkv[:, D:2 * D]
    v = qkv[:, 2 * D:]
    k_ref[...] = k.astype(BF16)
    v_ref[...] = v.astype(BF16)

    for src, dst in ((k, kf_ref), (v, vf_ref)):
        t = src.T.reshape(N_HEADS, HEAD_DIM, TM)
        for s in range(TM // T_CTX):
            dst[s] = t[:, :, s * T_CTX:(s + 1) * T_CTX]


def _qkv_call(x_lat, x_ctx, mod, g, w_qkv, cast_weights):
    cast_in, cast_out, cast_shapes = zip(*[_cast_specs(w, l) for w, l in cast_weights])
    new_cache = pl.BlockSpec((TM // T_CTX, N_HEADS, HEAD_DIM, T_CTX),
                             lambda i: (jnp.maximum(i - N_LAT_TILES, 0), 0, 0, 0))
    return pl.pallas_call(
        functools.partial(_qkv_kernel, len(cast_weights)),
        grid=(N_TILES,),
        in_specs=[_LAT_SPEC, _CTX_SPEC, _MOD_SPEC, _const_spec((1, D)), _const_spec((D, 3 * D)), *cast_in],
        out_specs=[_ROW_SPEC, _ROW_SPEC, _ROW_SPEC, new_cache, new_cache, _ROW_SPEC, *cast_out],
        out_shape=[jax.ShapeDtypeStruct((N_ROWS, D), BF16)] * 3
        + [jax.ShapeDtypeStruct((B_CTX, N_HEADS, HEAD_DIM, T_CTX), F32)] * 2
        + [jax.ShapeDtypeStruct((N_ROWS, D), F32)] + list(cast_shapes),
        compiler_params=_params(("arbitrary",)),
        name="qkv_proj",
    )(x_lat, x_ctx, mod, g, w_qkv, *[w for w, _ in cast_weights])


ATTN_SLOTS = 3


def _kq(k, q):
    return lax.dot_general(k, q, (((1,), (1,)), ((), ())), preferred_element_type=F32)


def _attn_pipeline(n_units, scores_fn, probs_fn, out_fn):
    assert ATTN_SLOTS == 3
    col_max = {n: scores_fn(n) for n in range(min(2, n_units))}
    col_sum = {}
    outs = []
    for n in range(n_units):
        if n + 2 < n_units:
            col_max[n + 2] = scores_fn(n + 2)
        col_sum[n] = probs_fn(n, col_max[n])
        if n >= 1:
            outs.append(out_fn(n - 1, col_sum[n - 1]))
    outs.append(out_fn(n_units - 1, col_sum[n_units - 1]))
    return outs


def _store_scores(s_ref, slot, row0, s):
    s_ref[slot, row0:row0 + s.shape[0], :] = s
    return s.max(axis=0, keepdims=True)


def _store_probs(s_ref, p_ref, slot, row0, rows, m):
    p = jnp.exp2(s_ref[slot, row0:row0 + rows, :] - m)
    p_ref[slot, row0:row0 + rows, :] = p.astype(BF16)
    return p.sum(axis=0, keepdims=True)


def _head_masks(n_q):
    lane = lax.broadcasted_iota(jnp.int32, (n_q, 2 * HEAD_DIM), 1)
    return lane < HEAD_DIM, lane >= HEAD_DIM


def _ctx_attn_kernel(q_ref, k_ref, v_ref, o_ref, s_ref, p_ref):
    v_t = v_ref[...].T
    masks = _head_masks(T_CTX)

    def scores(h):
        cols = slice((h // 2) * 2 * HEAD_DIM, (h // 2 + 1) * 2 * HEAD_DIM)
        q = q_ref[:, cols]
        qm = jnp.where(masks[h % 2], q, jnp.zeros_like(q))
        return _store_scores(s_ref, h % ATTN_SLOTS, 0, _kq(k_ref[:, cols], qm))

    def probs(h, m):
        return _store_probs(s_ref, p_ref, h % ATTN_SLOTS, 0, T_CTX, m)

    def out(h, l):
        o = jnp.dot(v_t[h * HEAD_DIM:(h + 1) * HEAD_DIM, :], p_ref[h % ATTN_SLOTS], preferred_element_type=F32)
        return o / l

    outs = _attn_pipeline(N_HEADS, scores, probs, out)
    o_ref[...] = jnp.concatenate(outs, axis=0).T.astype(BF16)


def _ctx_attn_call(q, k, v):
    blk = pl.BlockSpec((T_CTX, D), lambda b: (N_LAT_ROWS // T_CTX + b, 0))
    return pl.pallas_call(
        _ctx_attn_kernel,
        grid=(B_CTX,),
        in_specs=[blk, blk, blk],
        out_specs=blk,
        out_shape=jax.ShapeDtypeStruct((N_ROWS, D), BF16),
        scratch_shapes=[pltpu.VMEM((ATTN_SLOTS, T_CTX, T_CTX), F32), pltpu.VMEM((ATTN_SLOTS, T_CTX, T_CTX), BF16)],
        compiler_params=_params(("arbitrary",)),
        name="ctx_attn",
    )(q, k, v)


def _key_rows_of_block(i):
    r_first = Q_BLK_ROWS * i
    return _row_window_start(r_first), _row_window_start(r_first + Q_BLK_ROWS - 1) + WIN_ROWS


N_Q_BLOCKS = GRID_ROWS // Q_BLK_ROWS
MAX_LAT_KEYS = max(kr1 - kr0 for kr0, kr1 in map(_key_rows_of_block, range(N_Q_BLOCKS))) * GRID_W


NBR_HEADS = 4


def _nbr_attn_kernel(q_ref, k_ref, v_ref, ckt_ref, cvt_ref, pt_ref, o_in_ref, o_ref, s_ref, p_ref):
    del o_in_ref
    pw = 2 * HEAD_DIM
    v_t = v_ref[...].T
    ck = ckt_ref[...].reshape(NBR_HEADS * HEAD_DIM, PAST).astype(BF16).T
    cv_t = cvt_ref[...].reshape(NBR_HEADS * HEAD_DIM, PAST).astype(BF16)
    n_q = Q_BLK_ROWS * GRID_W
    masks = _head_masks(n_q)
    neg_blk = jnp.full((GRID_W, 2 * GRID_W), NEG_INF, F32)

    def unit(n):
        h, i = divmod(n, N_Q_BLOCKS)
        return (h, i, n % ATTN_SLOTS) + _key_rows_of_block(i)

    def scores(n):
        h, i, slot, kr0, kr1 = unit(n)
        cols = slice((h // 2) * pw, (h // 2 + 1) * pw)
        r_first = Q_BLK_ROWS * i
        q = q_ref[r_first * GRID_W:(r_first + Q_BLK_ROWS) * GRID_W, cols]
        qm = jnp.where(masks[h % 2], q, jnp.zeros_like(q))
        bias_rows = []
        for kr in range(kr0, kr1):
            blks = []
            for r0 in range(r_first, r_first + Q_BLK_ROWS, 2):
                typ = _pair_type(kr, r0)
                blks.append(neg_blk if typ is None else pt_ref[h, typ])
            bias_rows.append(jnp.concatenate(blks, axis=1))
        s_lat = _kq(k_ref[kr0 * GRID_W:kr1 * GRID_W, cols], qm) + jnp.concatenate(bias_rows, axis=0)
        m_lat = _store_scores(s_ref, slot, 0, s_lat)
        m_ctx = _store_scores(s_ref, slot, MAX_LAT_KEYS, _kq(ck[:, cols], qm))
        return jnp.maximum(m_lat, m_ctx)

    def probs(n, m):
        _, _, slot, kr0, kr1 = unit(n)
        return (_store_probs(s_ref, p_ref, slot, 0, (kr1 - kr0) * GRID_W, m)
                + _store_probs(s_ref, p_ref, slot, MAX_LAT_KEYS, PAST, m))

    def out(n, l):
        h, _, slot, kr0, kr1 = unit(n)
        hs = slice(h * HEAD_DIM, (h + 1) * HEAD_DIM)
        lk = (kr1 - kr0) * GRID_W
        o = (jnp.dot(v_t[hs, kr0 * GRID_W:kr1 * GRID_W], p_ref[slot, 0:lk, :], preferred_element_type=F32)
             + jnp.dot(cv_t[hs, :], p_ref[slot, MAX_LAT_KEYS:MAX_LAT_KEYS + PAST, :], preferred_element_type=F32))
        return o / l

    outs = _attn_pipeline(NBR_HEADS * N_Q_BLOCKS, scores, probs, out)
    o_t = jnp.concatenate([jnp.concatenate(outs[h * N_Q_BLOCKS:(h + 1) * N_Q_BLOCKS], axis=1)
                           for h in range(NBR_HEADS)], axis=0)
    o_ref[...] = o_t.T.astype(BF16)


def _nbr_attn_call(q, k, v, cache_k, cache_v, pair_table, o_ctx):
    assert NBR_HEADS % 2 == 0
    blk = pl.BlockSpec((T_LAT, NBR_HEADS * HEAD_DIM), lambda b, g: (b, g))
    cache = pl.BlockSpec((None, NBR_HEADS, HEAD_DIM, PAST), lambda b, g: (b, g, 0, 0))
    return pl.pallas_call(
        _nbr_attn_kernel,
        grid=(B_LAT, N_HEADS // NBR_HEADS),
        in_specs=[blk, blk, blk, cache, cache,
                  pl.BlockSpec((NBR_HEADS, N_PAIR_TYPES, GRID_W, 2 * GRID_W), lambda b, g: (g, 0, 0, 0)),
                  pl.BlockSpec(memory_space=pl.ANY)],
        out_specs=blk,
        out_shape=jax.ShapeDtypeStruct((N_ROWS, D), BF16),
        input_output_aliases={6: 0},
        scratch_shapes=[pltpu.VMEM((ATTN_SLOTS, MAX_LAT_KEYS + PAST, Q_BLK_ROWS * GRID_W), F32),
                        pltpu.VMEM((ATTN_SLOTS, MAX_LAT_KEYS + PAST, Q_BLK_ROWS * GRID_W), BF16)],
        compiler_params=_params(("arbitrary", "arbitrary")),
        name="nbr_attn",
    )(q, k, v, cache_k, cache_v, pair_table, o_ctx)


def _post_kernel(final_norm, n_cast, x_ref, o_ref, m_ref, g_ref, wo_ref, wgu_ref, wd_ref, fg_ref, *refs):
    n_out = 2 if final_norm else 1
    cast_in = refs[:n_cast]
    out_refs = refs[n_cast:n_cast + n_out]
    cast_out = refs[n_cast + n_out:2 * n_cast + n_out]
    acc_ref, x1_ref = refs[2 * n_cast + n_out:]

    _cast_chunks(cast_in, cast_out)
    mix = jnp.dot(o_ref[...], wo_ref[...], preferred_element_type=F32)
    x1 = x_ref[...] + m_ref[2] * mix
    x1_ref[...] = x1
    h = _modulate(x1, g_ref[...], m_ref[3], m_ref[4]).astype(BF16)
    for c in range(D_FF // FF_CHUNK):
        cs = slice(c * FF_CHUNK, (c + 1) * FF_CHUNK)
        us = slice(D_FF + c * FF_CHUNK, D_FF + (c + 1) * FF_CHUNK)
        gate = jnp.dot(h, wgu_ref[:, cs], preferred_element_type=F32)
        up = jnp.dot(h, wgu_ref[:, us], preferred_element_type=F32)
        a = (gate * _sigmoid(gate) * up).astype(BF16)
        part = jnp.dot(a, wd_ref[cs, :], preferred_element_type=F32)
        if c == 0:
            acc_ref[...] = part
        else:
            acc_ref[...] += part
    def finish(out_ref):
        x2 = x1_ref[...] + m_ref[5] * acc_ref[...]
        if final_norm:
            ms = jnp.mean(x2 * x2, axis=-1, keepdims=True)
            x2 = x2 * lax.rsqrt(ms + RMS_EPS) * fg_ref[...]
        out_ref[...] = x2

    if not final_norm:
        finish(out_refs[0])
        return
    finish(x1_ref)
    is_lat = pl.program_id(0) < N_LAT_TILES

    @pl.when(is_lat)
    def _():
        out_refs[0][...] = x1_ref[...]

    @pl.when(jnp.logical_not(is_lat))
    def _():
        out_refs[1][...] = x1_ref[...]


def _post_call(x, o, mod, g, w_o, w_gu, w_down, final_g, final_norm, cast_weights=()):
    cast_in, cast_out, cast_shapes = (zip(*[_cast_specs(w, l) for w, l in cast_weights])
                                      if cast_weights else ((), (), ()))
    if final_norm:
        out_specs = [_LAT_SPEC, _CTX_SPEC]
        out_shape = [jax.ShapeDtypeStruct((N_LAT_ROWS, D), F32), jax.ShapeDtypeStruct((N_CTX_ROWS, D), F32)]
    else:
        out_specs = [_ROW_SPEC]
        out_shape = [jax.ShapeDtypeStruct((N_ROWS, D), F32)]
    return pl.pallas_call(
        functools.partial(_post_kernel, final_norm, len(cast_weights)),
        grid=(N_TILES,),
        in_specs=[
            _ROW_SPEC, _ROW_SPEC, _MOD_SPEC,
            _const_spec((1, D)),
            _const_spec((D, D)),
            _const_spec((D, 2 * D_FF)),
            _const_spec((D_FF, D)),
            _const_spec((1, D)),
            *cast_in,
        ],
        out_specs=[*out_specs, *cast_out],
        out_shape=[*out_shape, *cast_shapes],
        scratch_shapes=[pltpu.VMEM((TM, D), F32), pltpu.VMEM((TM, D), F32)],
        compiler_params=_params(("arbitrary",)),
        name="mix_out_ffn",
    )(x, o, mod, g, w_o, w_gu, w_down, final_g, *[w for w, _ in cast_weights])


def _lru_in_kernel(x_ref, m_ref, g_ref, w_ref, gate_ref, xr_ref):
    xm = _modulate(x_ref[...], g_ref[...], m_ref[0], m_ref[1])
    y = jnp.dot(xm.astype(BF16), w_ref[...], preferred_element_type=F32)
    gate_ref[...] = jax.nn.gelu(y[:, :D]).astype(BF16)
    xr_ref[...] = y[:, D:]


def _lru_in_call(x, mod, g, w_in):
    return pl.pallas_call(
        _lru_in_kernel,
        grid=(N_TILES,),
        in_specs=[_ROW_SPEC, _MOD_SPEC, _const_spec((1, D)), _const_spec((D, 2 * D))],
        out_specs=[_ROW_SPEC, _ROW_SPEC],
        out_shape=[jax.ShapeDtypeStruct((N_ROWS, D), BF16), jax.ShapeDtypeStruct((N_ROWS, D), F32)],
        compiler_params=_params(("arbitrary",)),
        name="lru_in_proj",
    )(x, mod, g, w_in)


def _lru_scan_kernel(nb, nt, aliased, xr_ref, gg_ref, cw_ref, cb_ref, wa_ref, wi_ref, ba_ref, bi_ref, lam_ref,
                     h0_ref, *refs):
    y_ref, ht_ref, xt_ref, hf_ref, ab_ref, bb_ref, af0, bf0, af1, bf1 = refs[1:] if aliased else refs
    fwd_slots = ((af0, bf0), (af1, bf1))
    left = CONV_W // 2
    pad_rows = (CONV_W - 1) * nb
    xt_ref[pl.ds(0, left * nb), :] = jnp.zeros((left * nb, LRU_BW), F32)
    xt_ref[pl.ds((left + nt) * nb, pad_rows - left * nb), :] = jnp.zeros((pad_rows - left * nb, LRU_BW), F32)
    for s in range(nb):
        xt_ref[pl.ds(left * nb + s, nt, stride=nb), :] = xr_ref[pl.ds(s * nt, nt), :]

    cw = cw_ref[...]
    cb = cb_ref[...]
    w4 = (jnp.concatenate([wa_ref[0], wi_ref[0], wa_ref[1], wi_ref[1]], axis=1) * 0.5).astype(BF16)
    half_ba = 0.5 * ba_ref[...]
    half_bi = 0.5 * bi_ref[...]
    neg_lam = -lam_ref[...]
    sp = jnp.maximum(neg_lam, 0.0) + jnp.log1p(jnp.exp(-jnp.abs(neg_lam)))
    c1 = (-0.5 * LRU_C * LOG2E) * sp
    crows = SCAN_CHUNK_ROWS
    ct = crows // nb
    n_chunks = nt // ct
    assert n_chunks % 2 == 0 and ct % 2 == 0

    def chunk_rows(c):
        return pl.ds(pl.multiple_of(c * crows, 8), crows)

    def coeffs(c, slot):
        t0 = c * ct
        xc = cb
        for j in range(CONV_W):
            xc = xc + xt_ref[pl.ds(pl.multiple_of((t0 + j) * nb, 8), crows), :] * cw[j:j + 1]
        half_xc = 0.5 * xc
        pre = jnp.dot(xc.astype(BF16), w4, preferred_element_type=F32)

        def direction(d):
            o = 2 * d * LRU_BW
            t_r = jnp.tanh(pre[:, o:o + LRU_BW] + half_ba[d:d + 1])
            a = jnp.exp2(c1[d:d + 1] * t_r + c1[d:d + 1])
            t_i = jnp.tanh(pre[:, o + LRU_BW:o + 2 * LRU_BW] + half_bi[d:d + 1])
            y = 1.0 - a * a
            root = jnp.where(y > 0.0, y * lax.rsqrt(y), 0.0)
            return a, root * (half_xc * t_i + half_xc)

        a_ref, b_ref = fwd_slots[slot]
        a_ref[...], b_ref[...] = direction(0)
        ab_ref[chunk_rows(c), :], bb_ref[chunk_rows(c), :] = direction(1)

    def scan_fwd(c, slot, h):
        a_ref, b_ref = fwd_slots[slot]
        for s in range(ct):
            rows = slice(s * nb, (s + 1) * nb)
            h = a_ref[rows, :] * h + b_ref[rows, :]
            hf_ref[pl.ds(pl.multiple_of(c * crows + s * nb, 8), nb), :] = h
        return h

    def scan_bwd(c, h):
        for s in range(ct - 1, 0, -2):
            rows1 = pl.ds(pl.multiple_of(c * crows + s * nb, 8), nb)
            rows0 = pl.ds(pl.multiple_of(c * crows + (s - 1) * nb, 8), nb)
            a1, b1 = ab_ref[rows1, :], bb_ref[rows1, :]
            a0, b0 = ab_ref[rows0, :], bb_ref[rows0, :]
            bb_ref[rows1, :] = a1 * h + b1
            h = (a0 * a1) * h + (a0 * b1 + b0)
            bb_ref[rows0, :] = h
        return h

    def emit(c):
        for s in range(nb):
            src = pl.ds(c * crows + s, ct, stride=nb)
            dst = pl.ds(pl.multiple_of(s * nt + c * ct, ct), ct)
            hsum = hf_ref[src, :] + bb_ref[src, :]
            y_ref[dst, :] = (hsum * gg_ref[dst, :].astype(F32)).astype(BF16)

    coeffs(0, 0)

    def pass1(k, h):
        c = 2 * k
        h = scan_fwd(c, 0, h)
        coeffs(c + 1, 1)
        h = scan_fwd(c + 1, 1, h)
        coeffs(jnp.minimum(c + 2, n_chunks - 1), 0)
        return h

    ht_ref[0] = lax.fori_loop(0, n_chunks // 2, pass1, h0_ref[0])

    hb = scan_bwd(n_chunks - 1, h0_ref[1])

    def pass2(k, h):
        c = n_chunks - 2 - k
        emit(c + 1)
        return scan_bwd(c, h)

    ht_ref[1] = lax.fori_loop(0, n_chunks - 1, pass2, hb)
    emit(0)


def _lru_scan_call(nb, nt, row_block0, gg, xr, h0, conv_w, conv_b, w_a, w_i, b_a, b_i, lam, y_prev=None):
    rows = nb * nt
    col = pl.BlockSpec((rows, LRU_BW), lambda n: (row_block0, n))
    vec2 = pl.BlockSpec((2, LRU_BW), lambda n: (0, n))
    wblk = pl.BlockSpec((2, None, LRU_BW, LRU_BW), lambda n: (0, n, 0, 0))
    state = pl.BlockSpec((2, nb, LRU_BW), lambda n: (0, 0, n))
    tm_rows = (nt + CONV_W - 1) * nb
    crows = SCAN_CHUNK_ROWS
    aliased = y_prev is not None
    return pl.pallas_call(
        functools.partial(_lru_scan_kernel, nb, nt, aliased),
        grid=(LRU_BLOCKS,),
        in_specs=[
            col, col,
            pl.BlockSpec((CONV_W, LRU_BW), lambda n: (0, n)),
            pl.BlockSpec((1, LRU_BW), lambda n: (0, n)),
            wblk, wblk, vec2, vec2, vec2, state,
        ] + ([pl.BlockSpec(memory_space=pl.ANY)] if aliased else []),
        out_specs=[col, state],
        out_shape=[jax.ShapeDtypeStruct((N_ROWS, D), BF16), jax.ShapeDtypeStruct((2, nb, D), F32)],
        input_output_aliases={10: 0} if aliased else {},
        scratch_shapes=[pltpu.VMEM((tm_rows, LRU_BW), F32)] + [pltpu.VMEM((rows, LRU_BW), F32)] * 3
        + [pltpu.VMEM((crows, LRU_BW), F32)] * 4,
        compiler_params=_params(("arbitrary",)),
        name=f"lru_scan_{nb}x{nt}",
    )(xr, gg, conv_w, conv_b, w_a, w_i, b_a, b_i, lam, h0, *([y_prev] if aliased else []))


def kernel(x_prompt, x_sample, c, cache_k, cache_v, state_h, c_ctx, norm_g, w_mod, b_mod, attn_w_qkv, attn_w_o,
           attn_rpb, lru_w_in, lru_conv_w, lru_conv_b, lru_w_a, lru_b_a, lru_w_i, lru_b_i, lru_lam, lru_w_out,
           ffn_w_gu, ffn_w_down, final_g):
    x_lat = x_sample.reshape(N_LAT_ROWS, D)
    x_ctx = x_prompt.reshape(N_CTX_ROWS, D)
    cond = jnp.concatenate([c, c_ctx[None, :], jnp.zeros((N_COND - B_LAT - 1, D), F32)], axis=0)
    mod, pair_table, w_qkv = _prologue_call(cond, w_mod, b_mod, attn_rpb[0], attn_w_qkv)
    mod = mod.reshape(2, N_COND, N_MOD, 1, D)
    final_g2 = final_g.reshape(1, D)

    q, k, v, k_ctx, v_ctx, x, w_o, w_gu0, w_down0 = _qkv_call(
        x_lat, x_ctx, mod[0], norm_g[0, 0].reshape(1, D), w_qkv,
        ((attn_w_o, 0), (ffn_w_gu, 0), (ffn_w_down, 0)))
    o = _ctx_attn_call(q, k, v)
    cache_kt = jnp.transpose(cache_k[:, 0], (0, 2, 3, 1))
    cache_vt = jnp.transpose(cache_v[:, 0], (0, 2, 3, 1))
    o = _nbr_attn_call(q, k, v, cache_kt, cache_vt, pair_table, o)
    x, w_in, w_out, w_gu1, w_down1 = _post_call(
        x, o, mod[0], norm_g[0, 1].reshape(1, D), w_o, w_gu0, w_down0, final_g2, False,
        ((lru_w_in, 0), (lru_w_out, 0), (ffn_w_gu, 1), (ffn_w_down, 1)))

    gg, xr = _lru_in_call(x, mod[1], norm_g[1, 0].reshape(1, D), w_in)
    lru_p = (lru_conv_w[0], lru_conv_b[0].reshape(1, D), lru_w_a[0], lru_w_i[0], lru_b_a[0], lru_b_i[0], lru_lam[0])
    y, _ = _lru_scan_call(B_LAT, T_LAT, 0, gg, xr, jnp.swapaxes(state_h[:, 0], 0, 1), *lru_p)
    y, h_ctx = _lru_scan_call(B_CTX, T_CTX, N_LAT_ROWS // N_CTX_ROWS, gg, xr,
                              jnp.zeros((2, B_CTX, D), F32), *lru_p, y_prev=y)
    out_lat, out_ctx = _post_call(x, y, mod[1], norm_g[1, 1].reshape(1, D), w_out, w_gu1, w_down1, final_g2, True)

    y_sample = out_lat.reshape(B_LAT, T_LAT, D)
    y_prompt = out_ctx.reshape(B_CTX, T_CTX, D)
    new_k = jnp.transpose(k_ctx, (0, 3, 1, 2))[:, None]
    new_v = jnp.transpose(v_ctx, (0, 3, 1, 2))[:, None]
    new_h = jnp.swapaxes(h_ctx, 0, 1)[:, None]
    return y_prompt, y_sample, new_k, new_v, new_h
```

```python
import functools
import math

import jax
import jax.numpy as jnp
from jax import lax
from jax.experimental import pallas as pl
from jax.experimental.pallas import tpu as pltpu

F32 = jnp.float32
BF16 = jnp.bfloat16

D = 1024
B_CTX, T_CTX = 16, 256
B_LAT, T_LAT = 8, 1024
PAST = 512
GRID_W = 64
GRID_ROWS = T_LAT // GRID_W
N_HEADS = 16
HEAD_DIM = D // N_HEADS
ATTN_SCALE = HEAD_DIM ** -0.5
LOG2E = math.log2(math.e)
WIN_ROWS, WIN_COLS = 8, 16
N_DR = 2 * WIN_ROWS - 1
N_DC = 2 * WIN_COLS - 1
N_BOTH_TYPES = N_DR - 1
TYPE_LEFT_ONLY = N_BOTH_TYPES
TYPE_RIGHT_ONLY = N_BOTH_TYPES + 1
N_PAIR_TYPES = N_BOTH_TYPES + 2
FIRST_WIN_DR = -(WIN_ROWS // 2)
LAST_WIN_DR = WIN_ROWS // 2 - 1
LRU_BLOCKS = 8
LRU_BW = D // LRU_BLOCKS
CONV_W = 4
LRU_C = 8.0
D_FF = 2816
N_MOD = 6
RMS_EPS = 1e-6
NEG_INF = -1e30
LANES = 128

N_LAT_ROWS = B_LAT * T_LAT
N_CTX_ROWS = B_CTX * T_CTX
N_ROWS = N_LAT_ROWS + N_CTX_ROWS
N_COND = 16

TM = 512
N_LAT_TILES = N_LAT_ROWS // TM
N_CTX_TILES = N_CTX_ROWS // TM
N_TILES = N_LAT_TILES + N_CTX_TILES
TILES_PER_LAT_SAMPLE = T_LAT // TM
MOD_TN = 3072
FF_CHUNK = 256
Q_BLK_ROWS = 4
CAST_CHUNKS = 16
SCAN_CHUNK_ROWS = 512
V7X_VMEM_BYTES = 64 * 1024 * 1024
VMEM_LIMIT = V7X_VMEM_BYTES // 8 * 7


def _sigmoid(x):
    return 0.5 * jnp.tanh(0.5 * x) + 0.5


def _modulate(x, g, shift, scale):
    ms = jnp.mean(x * x, axis=-1, keepdims=True)
    return (x * lax.rsqrt(ms + RMS_EPS)) * g * (1.0 + scale) + shift


def _cond_row_of_tile(i):
    return jnp.where(i < N_LAT_TILES, i // TILES_PER_LAT_SAMPLE, B_LAT)


def _const_spec(shape):
    nd = len(shape)
    return pl.BlockSpec(shape, lambda *_: (0,) * nd, pipeline_mode=pl.Buffered(1))


def _params(sem):
    return pltpu.CompilerParams(dimension_semantics=sem, vmem_limit_bytes=VMEM_LIMIT)


_ROW_SPEC = pl.BlockSpec((TM, D), lambda i: (i, 0))
_LAT_SPEC = pl.BlockSpec((TM, D), lambda i: (jnp.minimum(i, N_LAT_TILES - 1), 0))
_CTX_SPEC = pl.BlockSpec((TM, D), lambda i: (jnp.maximum(i - N_LAT_TILES, 0), 0))
_MOD_SPEC = pl.BlockSpec((None, N_MOD, 1, D), lambda i: (_cond_row_of_tile(i), 0, 0, 0))


def _cast_specs(w, layer):
    _, rows, cols = w.shape
    chunk = lambda i: jnp.minimum(i, CAST_CHUNKS - 1)
    return (pl.BlockSpec((None, rows // CAST_CHUNKS, cols), lambda i: (layer, chunk(i), 0)),
            pl.BlockSpec((rows // CAST_CHUNKS, cols), lambda i: (chunk(i), 0)),
            jax.ShapeDtypeStruct((rows, cols), BF16))


def _cast_chunks(in_refs, out_refs):
    @pl.when(pl.program_id(0) < CAST_CHUNKS)
    def _():
        for src, dst in zip(in_refs, out_refs):
            dst[...] = src[...].astype(BF16)


def _pair_table_blocks(r_ref, o_ref):
    shape = (GRID_W, 2 * GRID_W)
    ck = lax.broadcasted_iota(jnp.int32, shape, 0)
    lane = lax.broadcasted_iota(jnp.int32, shape, 1)
    cq = jnp.bitwise_and(lane, GRID_W - 1)
    cs = jnp.clip(cq - WIN_COLS // 2, 0, GRID_W - WIN_COLS)
    in_win = jnp.logical_and(ck >= cs, ck < cs + WIN_COLS)

    def toeplitz(h, d, lane0):
        base = jnp.broadcast_to(r_ref[h, d:d + 1, :], shape)
        return pltpu.roll(base, (lane0 - (WIN_COLS - 1)) % LANES, 1, stride=1, stride_axis=0)

    neg = jnp.full(shape, NEG_INF, F32)
    for h in range(r_ref.shape[0]):
        for t in range(N_PAIR_TYPES):
            if t < N_BOTH_TYPES:
                d_left, d_right = t + 1, t
            elif t == TYPE_LEFT_ONLY:
                d_left, d_right = FIRST_WIN_DR + WIN_ROWS - 1, None
            else:
                d_left, d_right = None, LAST_WIN_DR + WIN_ROWS - 1
            left = neg if d_left is None else toeplitz(h, d_left, 0)
            right = neg if d_right is None else toeplitz(h, d_right, GRID_W)
            val = jnp.where(lane < GRID_W, left, right)
            o_ref[h, t] = jnp.where(in_win, val * LOG2E, NEG_INF)


def _prologue_kernel(cond_ref, w_ref, b_ref, r_ref, wqkv_ref, mod_ref, pt_ref, wq_ref):
    c = cond_ref[...]
    s = (c * _sigmoid(c)).astype(BF16)
    mod_ref[...] = jnp.dot(s, w_ref[...].astype(BF16), preferred_element_type=F32) + b_ref[...]
    _pair_table_blocks(r_ref, pt_ref)
    wq_ref[...] = wqkv_ref[...].astype(BF16)


def _prologue_call(cond, w_mod, b_mod, rpb, w_qkv):
    assert 2 * GRID_W == LANES
    depth = w_mod.shape[0]
    n = N_MOD * D
    cols_steps = n // MOD_TN
    steps = depth * cols_steps
    heads = N_HEADS // steps
    w_rows = D // steps
    step = lambda l, j: l * cols_steps + j
    r = jnp.pad(rpb[:, :, ::-1], ((0, 0), (0, 0), (0, LANES - N_DC)))
    return pl.pallas_call(
        _prologue_kernel,
        grid=(depth, cols_steps),
        in_specs=[
            pl.BlockSpec((N_COND, D), lambda l, j: (0, 0)),
            pl.BlockSpec((None, D, MOD_TN), lambda l, j: (l, 0, j)),
            pl.BlockSpec((None, 1, MOD_TN), lambda l, j: (l, 0, j)),
            pl.BlockSpec((heads, N_DR, LANES), lambda l, j: (step(l, j), 0, 0)),
            pl.BlockSpec((None, w_rows, 3 * D), lambda l, j: (0, step(l, j), 0)),
        ],
        out_specs=[
            pl.BlockSpec((None, N_COND, MOD_TN), lambda l, j: (l, 0, j)),
            pl.BlockSpec((heads, N_PAIR_TYPES, GRID_W, LANES), lambda l, j: (step(l, j), 0, 0, 0)),
            pl.BlockSpec((w_rows, 3 * D), lambda l, j: (step(l, j), 0)),
        ],
        out_shape=[jax.ShapeDtypeStruct((depth, N_COND, n), F32),
                   jax.ShapeDtypeStruct((N_HEADS, N_PAIR_TYPES, GRID_W, LANES), F32),
                   jax.ShapeDtypeStruct((D, 3 * D), BF16)],
        compiler_params=_params(("arbitrary", "arbitrary")),
        name="prologue",
    )(cond, w_mod, b_mod.reshape(depth, 1, n), r, w_qkv)


def _row_window_start(r):
    return min(max(r - WIN_ROWS // 2, 0), GRID_ROWS - WIN_ROWS)


def _pair_type(kr, r0):
    in0 = _row_window_start(r0) <= kr < _row_window_start(r0) + WIN_ROWS
    in1 = _row_window_start(r0 + 1) <= kr < _row_window_start(r0 + 1) + WIN_ROWS
    if in0 and in1:
        dl = kr - r0
        assert 0 <= dl + WIN_ROWS - 2 < N_BOTH_TYPES
        return dl + WIN_ROWS - 2
    if in0:
        assert kr - r0 == FIRST_WIN_DR
        return TYPE_LEFT_ONLY
    if in1:
        assert kr - (r0 + 1) == LAST_WIN_DR
        return TYPE_RIGHT_ONLY
    return None


def _qkv_kernel(n_cast, xl_ref, xc_ref, m_ref, g_ref, w_ref, *refs):
    cast_in = refs[:n_cast]
    q_ref, k_ref, v_ref, kf_ref, vf_ref, x_ref = refs[n_cast:n_cast + 6]
    cast_out = refs[n_cast + 6:]
    _cast_chunks(cast_in, cast_out)
    x = jnp.where(pl.program_id(0) < N_LAT_TILES, xl_ref[...], xc_ref[...])
    x_ref[...] = x
    xm = _modulate(x, g_ref[...], m_ref[0], m_ref[1])
    qkv = jnp.dot(xm.astype(BF16), w_ref[...], preferred_element_type=F32)
    q_ref[...] = (qkv[:, :D] * (ATTN_SCALE * LOG2E)).astype(BF16)
    k = qkv[:, D:2 * D]
    v = qkv[:, 2 * D:]
    k_ref[...] = k.astype(BF16)
    v_ref[...] = v.astype(BF16)

    for src, dst in ((k, kf_ref), (v, vf_ref)):
        t = src.T.reshape(N_HEADS, HEAD_DIM, TM)
        for s in range(TM // T_CTX):
            dst[s] = t[:, :, s * T_CTX:(s + 1) * T_CTX]


def _qkv_call(x_lat, x_ctx, mod, g, w_qkv, cast_weights):
    cast_in, cast_out, cast_shapes = zip(*[_cast_specs(w, l) for w, l in cast_weights])
    new_cache = pl.BlockSpec((TM // T_CTX, N_HEADS, HEAD_DIM, T_CTX),
                             lambda i: (jnp.maximum(i - N_LAT_TILES, 0), 0, 0, 0))
    return pl.pallas_call(
        functools.partial(_qkv_kernel, len(cast_weights)),
        grid=(N_TILES,),
        in_specs=[_LAT_SPEC, _CTX_SPEC, _MOD_SPEC, _const_spec((1, D)), _const_spec((D, 3 * D)), *cast_in],
        out_specs=[_ROW_SPEC, _ROW_SPEC, _ROW_SPEC, new_cache, new_cache, _ROW_SPEC, *cast_out],
        out_shape=[jax.ShapeDtypeStruct((N_ROWS, D), BF16)] * 3
        + [jax.ShapeDtypeStruct((B_CTX, N_HEADS, HEAD_DIM, T_CTX), F32)] * 2
        + [jax.ShapeDtypeStruct((N_ROWS, D), F32)] + list(cast_shapes),
        compiler_params=_params(("arbitrary",)),
        name="qkv_proj",
    )(x_lat, x_ctx, mod, g, w_qkv, *[w for w, _ in cast_weights])


ATTN_SLOTS = 3


def _kq(k, q):
    return lax.dot_general(k, q, (((1,), (1,)), ((), ())), preferred_element_type=F32)


def _attn_pipeline(n_units, scores_fn, probs_fn, out_fn):
    assert ATTN_SLOTS == 3
    col_max = {n: scores_fn(n) for n in range(min(2, n_units))}
    col_sum = {}
    outs = []
    for n in range(n_units):
        if n + 2 < n_units:
            col_max[n + 2] = scores_fn(n + 2)
        col_sum[n] = probs_fn(n, col_max[n])
        if n >= 1:
            outs.append(out_fn(n - 1, col_sum[n - 1]))
    outs.append(out_fn(n_units - 1, col_sum[n_units - 1]))
    return outs


def _store_scores(s_ref, slot, row0, s):
    s_ref[slot, row0:row0 + s.shape[0], :] = s
    return s.max(axis=0, keepdims=True)


def _store_probs(s_ref, p_ref, slot, row0, rows, m):
    p = jnp.exp2(s_ref[slot, row0:row0 + rows, :] - m)
    p_ref[slot, row0:row0 + rows, :] = p.astype(BF16)
    return p.sum(axis=0, keepdims=True)


def _head_masks(n_q):
    lane = lax.broadcasted_iota(jnp.int32, (n_q, 2 * HEAD_DIM), 1)
    return lane < HEAD_DIM, lane >= HEAD_DIM


def _ctx_attn_kernel(q_ref, k_ref, v_ref, o_ref, s_ref, p_ref):
    v_t = v_ref[...].T
    masks = _head_masks(T_CTX)

    def scores(h):
        cols = slice((h // 2) * 2 * HEAD_DIM, (h // 2 + 1) * 2 * HEAD_DIM)
        q = q_ref[:, cols]
        qm = jnp.where(masks[h % 2], q, jnp.zeros_like(q))
        return _store_scores(s_ref, h % ATTN_SLOTS, 0, _kq(k_ref[:, cols], qm))

    def probs(h, m):
        return _store_probs(s_ref, p_ref, h % ATTN_SLOTS, 0, T_CTX, m)

    def out(h, l):
        o = jnp.dot(v_t[h * HEAD_DIM:(h + 1) * HEAD_DIM, :], p_ref[h % ATTN_SLOTS], preferred_element_type=F32)
        return o / l

    outs = _attn_pipeline(N_HEADS, scores, probs, out)
    o_ref[...] = jnp.concatenate(outs, axis=0).T.astype(BF16)


def _ctx_attn_call(q, k, v):
    blk = pl.BlockSpec((T_CTX, D), lambda b: (N_LAT_ROWS // T_CTX + b, 0))
    return pl.pallas_call(
        _ctx_attn_kernel,
        grid=(B_CTX,),
        in_specs=[blk, blk, blk],
        out_specs=blk,
        out_shape=jax.ShapeDtypeStruct((N_ROWS, D), BF16),
        input_output_aliases={0: 0},
        scratch_shapes=[pltpu.VMEM((ATTN_SLOTS, T_CTX, T_CTX), F32), pltpu.VMEM((ATTN_SLOTS, T_CTX, T_CTX), BF16)],
        compiler_params=_params(("arbitrary",)),
        name="ctx_attn",
    )(q, k, v)


def _key_rows_of_block(i):
    r_first = Q_BLK_ROWS * i
    return _row_window_start(r_first), _row_window_start(r_first + Q_BLK_ROWS - 1) + WIN_ROWS


N_Q_BLOCKS = GRID_ROWS // Q_BLK_ROWS
MAX_LAT_KEYS = max(kr1 - kr0 for kr0, kr1 in map(_key_rows_of_block, range(N_Q_BLOCKS))) * GRID_W


NBR_HEADS = 4


def _nbr_attn_kernel(q_ref, k_ref, v_ref, ckt_ref, cvt_ref, pt_ref, o_ref, s_ref, p_ref):
    pw = 2 * HEAD_DIM
    v_t = v_ref[...].T
    ck = ckt_ref[...].reshape(NBR_HEADS * HEAD_DIM, PAST).astype(BF16).T
    cv_t = cvt_ref[...].reshape(NBR_HEADS * HEAD_DIM, PAST).astype(BF16)
    n_q = Q_BLK_ROWS * GRID_W
    masks = _head_masks(n_q)
    neg_blk = jnp.full((GRID_W, 2 * GRID_W), NEG_INF, F32)

    def unit(n):
        h, i = divmod(n, N_Q_BLOCKS)
        return (h, i, n % ATTN_SLOTS) + _key_rows_of_block(i)

    def scores(n):
        h, i, slot, kr0, kr1 = unit(n)
        cols = slice((h // 2) * pw, (h // 2 + 1) * pw)
        r_first = Q_BLK_ROWS * i
        q = q_ref[r_first * GRID_W:(r_first + Q_BLK_ROWS) * GRID_W, cols]
        qm = jnp.where(masks[h % 2], q, jnp.zeros_like(q))
        bias_rows = []
        for kr in range(kr0, kr1):
            blks = []
            for r0 in range(r_first, r_first + Q_BLK_ROWS, 2):
                typ = _pair_type(kr, r0)
                blks.append(neg_blk if typ is None else pt_ref[h, typ])
            bias_rows.append(jnp.concatenate(blks, axis=1))
        s_lat = _kq(k_ref[kr0 * GRID_W:kr1 * GRID_W, cols], qm) + jnp.concatenate(bias_rows, axis=0)
        m_lat = _store_scores(s_ref, slot, 0, s_lat)
        m_ctx = _store_scores(s_ref, slot, MAX_LAT_KEYS, _kq(ck[:, cols], qm))
        return jnp.maximum(m_lat, m_ctx)

    def probs(n, m):
        _, _, slot, kr0, kr1 = unit(n)
        return (_store_probs(s_ref, p_ref, slot, 0, (kr1 - kr0) * GRID_W, m)
                + _store_probs(s_ref, p_ref, slot, MAX_LAT_KEYS, PAST, m))

    def out(n, l):
        h, _, slot, kr0, kr1 = unit(n)
        hs = slice(h * HEAD_DIM, (h + 1) * HEAD_DIM)
        lk = (kr1 - kr0) * GRID_W
        o = (jnp.dot(v_t[hs, kr0 * GRID_W:kr1 * GRID_W], p_ref[slot, 0:lk, :], preferred_element_type=F32)
             + jnp.dot(cv_t[hs, :], p_ref[slot, MAX_LAT_KEYS:MAX_LAT_KEYS + PAST, :], preferred_element_type=F32))
        return o / l

    outs = _attn_pipeline(NBR_HEADS * N_Q_BLOCKS, scores, probs, out)
    o_t = jnp.concatenate([jnp.concatenate(outs[h * N_Q_BLOCKS:(h + 1) * N_Q_BLOCKS], axis=1)
                           for h in range(NBR_HEADS)], axis=0)
    o_ref[...] = o_t.T.astype(BF16)


def _nbr_attn_call(qo, k, v, cache_k, cache_v, pair_table):
    assert NBR_HEADS % 2 == 0
    blk = pl.BlockSpec((T_LAT, NBR_HEADS * HEAD_DIM), lambda b, g: (b, g))
    cache = pl.BlockSpec((None, NBR_HEADS, HEAD_DIM, PAST), lambda b, g: (b, g, 0, 0))
    return pl.pallas_call(
        _nbr_attn_kernel,
        grid=(B_LAT, N_HEADS // NBR_HEADS),
        in_specs=[blk, blk, blk, cache, cache,
                  pl.BlockSpec((NBR_HEADS, N_PAIR_TYPES, GRID_W, 2 * GRID_W), lambda b, g: (g, 0, 0, 0))],
        out_specs=blk,
        out_shape=jax.ShapeDtypeStruct((N_ROWS, D), BF16),
        input_output_aliases={0: 0},
        scratch_shapes=[pltpu.VMEM((ATTN_SLOTS, MAX_LAT_KEYS + PAST, Q_BLK_ROWS * GRID_W), F32),
                        pltpu.VMEM((ATTN_SLOTS, MAX_LAT_KEYS + PAST, Q_BLK_ROWS * GRID_W), BF16)],
        compiler_params=_params(("arbitrary", "arbitrary")),
        name="nbr_attn",
    )(qo, k, v, cache_k, cache_v, pair_table)


def _post_kernel(final_norm, n_cast, x_ref, o_ref, m_ref, g_ref, wo_ref, wgu_ref, wd_ref, fg_ref, *refs):
    n_out = 2 if final_norm else 1
    cast_in = refs[:n_cast]
    out_refs = refs[n_cast:n_cast + n_out]
    cast_out = refs[n_cast + n_out:2 * n_cast + n_out]
    acc_ref, x1_ref = refs[2 * n_cast + n_out:]

    _cast_chunks(cast_in, cast_out)
    mix = jnp.dot(o_ref[...], wo_ref[...], preferred_element_type=F32)
    x1 = x_ref[...] + m_ref[2] * mix
    x1_ref[...] = x1
    h = _modulate(x1, g_ref[...], m_ref[3], m_ref[4]).astype(BF16)
    for c in range(D_FF // FF_CHUNK):
        cs = slice(c * FF_CHUNK, (c + 1) * FF_CHUNK)
        us = slice(D_FF + c * FF_CHUNK, D_FF + (c + 1) * FF_CHUNK)
        gate = jnp.dot(h, wgu_ref[:, cs], preferred_element_type=F32)
        up = jnp.dot(h, wgu_ref[:, us], preferred_element_type=F32)
        a = (gate * _sigmoid(gate) * up).astype(BF16)
        part = jnp.dot(a, wd_ref[cs, :], preferred_element_type=F32)
        if c == 0:
            acc_ref[...] = part
        else:
            acc_ref[...] += part
    def finish(out_ref):
        x2 = x1_ref[...] + m_ref[5] * acc_ref[...]
        if final_norm:
            ms = jnp.mean(x2 * x2, axis=-1, keepdims=True)
            x2 = x2 * lax.rsqrt(ms + RMS_EPS) * fg_ref[...]
        out_ref[...] = x2

    if not final_norm:
        finish(out_refs[0])
        return
    finish(x1_ref)
    is_lat = pl.program_id(0) < N_LAT_TILES

    @pl.when(is_lat)
    def _():
        out_refs[0][...] = x1_ref[...]

    @pl.when(jnp.logical_not(is_lat))
    def _():
        out_refs[1][...] = x1_ref[...]


def _post_call(x, o, mod, g, w_o, w_gu, w_down, final_g, final_norm, cast_weights=()):
    cast_in, cast_out, cast_shapes = (zip(*[_cast_specs(w, l) for w, l in cast_weights])
                                      if cast_weights else ((), (), ()))
    if final_norm:
        out_specs = [_LAT_SPEC, _CTX_SPEC]
        out_shape = [jax.ShapeDtypeStruct((N_LAT_ROWS, D), F32), jax.ShapeDtypeStruct((N_CTX_ROWS, D), F32)]
    else:
        out_specs = [_ROW_SPEC]
        out_shape = [jax.ShapeDtypeStruct((N_ROWS, D), F32)]
    return pl.pallas_call(
        functools.partial(_post_kernel, final_norm, len(cast_weights)),
        grid=(N_TILES,),
        in_specs=[
            _ROW_SPEC, _ROW_SPEC, _MOD_SPEC,
            _const_spec((1, D)),
            _const_spec((D, D)),
            _const_spec((D, 2 * D_FF)),
            _const_spec((D_FF, D)),
            _const_spec((1, D)),
            *cast_in,
        ],
        out_specs=[*out_specs, *cast_out],
        out_shape=[*out_shape, *cast_shapes],
        scratch_shapes=[pltpu.VMEM((TM, D), F32), pltpu.VMEM((TM, D), F32)],
        compiler_params=_params(("arbitrary",)),
        name="mix_out_ffn",
    )(x, o, mod, g, w_o, w_gu, w_down, final_g, *[w for w, _ in cast_weights])


def _lru_in_kernel(x_ref, m_ref, g_ref, w_ref, gate_ref, xr_ref):
    xm = _modulate(x_ref[...], g_ref[...], m_ref[0], m_ref[1])
    y = jnp.dot(xm.astype(BF16), w_ref[...], preferred_element_type=F32)
    gate_ref[...] = jax.nn.gelu(y[:, :D]).astype(BF16)
    xr_ref[...] = y[:, D:]


def _lru_in_call(x, mod, g, w_in):
    return pl.pallas_call(
        _lru_in_kernel,
        grid=(N_TILES,),
        in_specs=[_ROW_SPEC, _MOD_SPEC, _const_spec((1, D)), _const_spec((D, 2 * D))],
        out_specs=[_ROW_SPEC, _ROW_SPEC],
        out_shape=[jax.ShapeDtypeStruct((N_ROWS, D), BF16), jax.ShapeDtypeStruct((N_ROWS, D), F32)],
        compiler_params=_params(("arbitrary",)),
        name="lru_in_proj",
    )(x, mod, g, w_in)


def _lru_scan_kernel(nb, nt, xr_ref, gg_ref, cw_ref, cb_ref, wa_ref, wi_ref, ba_ref, bi_ref, lam_ref,
                     h0_ref, *refs):
    y_ref, ht_ref, xt_ref, hf_ref, ab_ref, bb_ref, af0, bf0, af1, bf1 = refs
    fwd_slots = ((af0, bf0), (af1, bf1))
    left = CONV_W // 2
    pad_rows = (CONV_W - 1) * nb
    xt_ref[pl.ds(0, left * nb), :] = jnp.zeros((left * nb, LRU_BW), F32)
    xt_ref[pl.ds((left + nt) * nb, pad_rows - left * nb), :] = jnp.zeros((pad_rows - left * nb, LRU_BW), F32)
    for s in range(nb):
        xt_ref[pl.ds(left * nb + s, nt, stride=nb), :] = xr_ref[pl.ds(s * nt, nt), :]

    cw = cw_ref[...]
    cb = cb_ref[...]
    w4 = (jnp.concatenate([wa_ref[0], wi_ref[0], wa_ref[1], wi_ref[1]], axis=1) * 0.5).astype(BF16)
    half_ba = 0.5 * ba_ref[...]
    half_bi = 0.5 * bi_ref[...]
    neg_lam = -lam_ref[...]
    sp = jnp.maximum(neg_lam, 0.0) + jnp.log1p(jnp.exp(-jnp.abs(neg_lam)))
    c1 = (-0.5 * LRU_C * LOG2E) * sp
    crows = SCAN_CHUNK_ROWS
    ct = crows // nb
    n_chunks = nt // ct
    assert n_chunks % 2 == 0 and ct % 2 == 0

    def chunk_rows(c):
        return pl.ds(pl.multiple_of(c * crows, 8), crows)

    def coeffs(c, slot):
        t0 = c * ct
        xc = cb
        for j in range(CONV_W):
            xc = xc + xt_ref[pl.ds(pl.multiple_of((t0 + j) * nb, 8), crows), :] * cw[j:j + 1]
        half_xc = 0.5 * xc
        pre = jnp.dot(xc.astype(BF16), w4, preferred_element_type=F32)

        def direction(d):
            o = 2 * d * LRU_BW
            t_r = jnp.tanh(pre[:, o:o + LRU_BW] + half_ba[d:d + 1])
            a = jnp.exp2(c1[d:d + 1] * t_r + c1[d:d + 1])
            t_i = jnp.tanh(pre[:, o + LRU_BW:o + 2 * LRU_BW] + half_bi[d:d + 1])
            y = 1.0 - a * a
            root = jnp.where(y > 0.0, y * lax.rsqrt(y), 0.0)
            return a, root * (half_xc * t_i + half_xc)

        a_ref, b_ref = fwd_slots[slot]
        a_ref[...], b_ref[...] = direction(0)
        ab_ref[chunk_rows(c), :], bb_ref[chunk_rows(c), :] = direction(1)

    def scan_fwd(c, slot, h):
        a_ref, b_ref = fwd_slots[slot]
        for s in range(ct):
            rows = slice(s * nb, (s + 1) * nb)
            h = a_ref[rows, :] * h + b_ref[rows, :]
            hf_ref[pl.ds(pl.multiple_of(c * crows + s * nb, 8), nb), :] = h
        return h

    def scan_bwd(c, h):
        for s in range(ct - 1, 0, -2):
            rows1 = pl.ds(pl.multiple_of(c * crows + s * nb, 8), nb)
            rows0 = pl.ds(pl.multiple_of(c * crows + (s - 1) * nb, 8), nb)
            a1, b1 = ab_ref[rows1, :], bb_ref[rows1, :]
            a0, b0 = ab_ref[rows0, :], bb_ref[rows0, :]
            bb_ref[rows1, :] = a1 * h + b1
            h = (a0 * a1) * h + (a0 * b1 + b0)
            bb_ref[rows0, :] = h
        return h

    def emit(c):
        for s in range(nb):
            src = pl.ds(c * crows + s, ct, stride=nb)
            dst = pl.ds(pl.multiple_of(s * nt + c * ct, ct), ct)
            hsum = hf_ref[src, :] + bb_ref[src, :]
            y_ref[dst, :] = (hsum * gg_ref[dst, :].astype(F32)).astype(BF16)

    coeffs(0, 0)

    def pass1(k, h):
        c = 2 * k
        h = scan_fwd(c, 0, h)
        coeffs(c + 1, 1)
        h = scan_fwd(c + 1, 1, h)
        coeffs(jnp.minimum(c + 2, n_chunks - 1), 0)
        return h

    ht_ref[0] = lax.fori_loop(0, n_chunks // 2, pass1, h0_ref[0])

    hb = scan_bwd(n_chunks - 1, h0_ref[1])

    def pass2(k, h):
        c = n_chunks - 2 - k
        emit(c + 1)
        return scan_bwd(c, h)

    ht_ref[1] = lax.fori_loop(0, n_chunks - 1, pass2, hb)
    emit(0)


def _lru_scan_call(nb, nt, row_block0, gg, xr, h0, conv_w, conv_b, w_a, w_i, b_a, b_i, lam):
    rows = nb * nt
    col = pl.BlockSpec((rows, LRU_BW), lambda n: (row_block0, n))
    vec2 = pl.BlockSpec((2, LRU_BW), lambda n: (0, n))
    wblk = pl.BlockSpec((2, None, LRU_BW, LRU_BW), lambda n: (0, n, 0, 0))
    state = pl.BlockSpec((2, nb, LRU_BW), lambda n: (0, 0, n))
    tm_rows = (nt + CONV_W - 1) * nb
    crows = SCAN_CHUNK_ROWS
    return pl.pallas_call(
        functools.partial(_lru_scan_kernel, nb, nt),
        grid=(LRU_BLOCKS,),
        in_specs=[
            col, col,
            pl.BlockSpec((CONV_W, LRU_BW), lambda n: (0, n)),
            pl.BlockSpec((1, LRU_BW), lambda n: (0, n)),
            wblk, wblk, vec2, vec2, vec2, state,
        ],
        out_specs=[col, state],
        out_shape=[jax.ShapeDtypeStruct((N_ROWS, D), BF16), jax.ShapeDtypeStruct((2, nb, D), F32)],
        input_output_aliases={1: 0},
        scratch_shapes=[pltpu.VMEM((tm_rows, LRU_BW), F32)] + [pltpu.VMEM((rows, LRU_BW), F32)] * 3
        + [pltpu.VMEM((crows, LRU_BW), F32)] * 4,
        compiler_params=_params(("arbitrary",)),
        name=f"lru_scan_{nb}x{nt}",
    )(xr, gg, conv_w, conv_b, w_a, w_i, b_a, b_i, lam, h0)


def kernel(x_prompt, x_sample, c, cache_k, cache_v, state_h, c_ctx, norm_g, w_mod, b_mod, attn_w_qkv, attn_w_o,
           attn_rpb, lru_w_in, lru_conv_w, lru_conv_b, lru_w_a, lru_b_a, lru_w_i, lru_b_i, lru_lam, lru_w_out,
           ffn_w_gu, ffn_w_down, final_g):
    x_lat = x_sample.reshape(N_LAT_ROWS, D)
    x_ctx = x_prompt.reshape(N_CTX_ROWS, D)
    cond = jnp.concatenate([c, c_ctx[None, :], jnp.zeros((N_COND - B_LAT - 1, D), F32)], axis=0)
    mod, pair_table, w_qkv = _prologue_call(cond, w_mod, b_mod, attn_rpb[0], attn_w_qkv)
    mod = mod.reshape(2, N_COND, N_MOD, 1, D)
    final_g2 = final_g.reshape(1, D)

    q, k, v, k_ctx, v_ctx, x, w_o, w_gu0, w_down0 = _qkv_call(
        x_lat, x_ctx, mod[0], norm_g[0, 0].reshape(1, D), w_qkv,
        ((attn_w_o, 0), (ffn_w_gu, 0), (ffn_w_down, 0)))
    o = _ctx_attn_call(q, k, v)
    cache_kt = jnp.transpose(cache_k[:, 0], (0, 2, 3, 1))
    cache_vt = jnp.transpose(cache_v[:, 0], (0, 2, 3, 1))
    o = _nbr_attn_call(o, k, v, cache_kt, cache_vt, pair_table)
    x, w_in, w_out, w_gu1, w_down1 = _post_call(
        x, o, mod[0], norm_g[0, 1].reshape(1, D), w_o, w_gu0, w_down0, final_g2, False,
        ((lru_w_in, 0), (lru_w_out, 0), (ffn_w_gu, 1), (ffn_w_down, 1)))

    gg, xr = _lru_in_call(x, mod[1], norm_g[1, 0].reshape(1, D), w_in)
    lru_p = (lru_conv_w[0], lru_conv_b[0].reshape(1, D), lru_w_a[0], lru_w_i[0], lru_b_a[0], lru_b_i[0], lru_lam[0])
    y, _ = _lru_scan_call(B_LAT, T_LAT, 0, gg, xr, jnp.swapaxes(state_h[:, 0], 0, 1), *lru_p)
    y, h_ctx = _lru_scan_call(B_CTX, T_CTX, N_LAT_ROWS // N_CTX_ROWS, y, xr, jnp.zeros((2, B_CTX, D), F32), *lru_p)
    out_lat, out_ctx = _post_call(x, y, mod[1], norm_g[1, 1].reshape(1, D), w_out, w_gu1, w_down1, final_g2, True)

    y_sample = out_lat.reshape(B_LAT, T_LAT, D)
    y_prompt = out_ctx.reshape(B_CTX, T_CTX, D)
    new_k = jnp.transpose(k_ctx, (0, 3, 1, 2))[:, None]
    new_v = jnp.transpose(v_ctx, (0, 3, 1, 2))[:, None]
    new_h = jnp.swapaxes(h_ctx, 0, 1)[:, None]
    return y_prompt, y_sample, new_k, new_v, new_h
```
